```python
import jax, jax.numpy as jnp
from jax import lax
import numpy as np

D_MODEL = 2048
BATCH = 1
SEQ = 8192
DEPTH = 1

N_META = 16
D_MIX = D_MODEL
GLA_HEADS = 4
GLA_DK = 128
GLA_DV = 256
GLA_DK_TOT = GLA_HEADS * GLA_DK
GLA_DV_TOT = GLA_HEADS * GLA_DV
GLA_RANK = 16
GLA_TAU = 16.0
GLA_CHUNK = 64
FOX_HEADS = 8
FOX_HD = 128
FOX_W = FOX_HEADS * FOX_HD
FOX_BLOCK = 128
PAD_FRONT = FOX_BLOCK - N_META
N_GROUPS = 4
EXPERTS_PER_GROUP = 8
TOP_K = 2
D_EXPERT = 512
EPS = 1e-6
SPLIT_SIZES = (FOX_W, FOX_W, FOX_W, FOX_HEADS, GLA_DK_TOT, GLA_DK_TOT, GLA_DV_TOT, GLA_DV_TOT, GLA_RANK)
D_IN_PROJ = 3 * FOX_W + FOX_HEADS + 2 * GLA_DK_TOT + 2 * GLA_DV_TOT + GLA_RANK

kernel_name = "hymba_gla_fox_hiermoe_block"


def _rmsnorm(x, g):
    xf = x.astype(jnp.float32)
    y = xf * lax.rsqrt(jnp.mean(xf * xf, axis=-1, keepdims=True) + EPS)
    return (y * g.astype(jnp.float32)).astype(x.dtype)


def _pad_front(a):
    return jnp.pad(a, [(0, 0), (PAD_FRONT, 0)] + [(0, 0)] * (a.ndim - 2))


def _split_cols(proj):
    out = []
    start = 0
    for s in SPLIT_SIZES:
        out.append(proj[..., start:start + s])
        start += s
    return out


def gla_mixer(q, k, v, r, z, w_gate2, b_gate, norm_g):
    f32 = jnp.float32
    B, L, _ = q.shape
    T = L + PAD_FRONT
    C = GLA_CHUNK
    N = T // C
    log_a = jax.nn.log_sigmoid((z @ w_gate2).astype(f32) + b_gate.astype(f32)) / GLA_TAU

    def heads(a, d):
        return _pad_front(a.astype(f32).reshape(B, L, GLA_HEADS, d))

    def chunks(a):
        return a.reshape(B, N, C, GLA_HEADS, a.shape[-1])

    qc = chunks(heads(q, GLA_DK) * GLA_DK ** -0.5)
    kc = chunks(heads(k, GLA_DK))
    vc = chunks(heads(v, GLA_DV))
    gc = chunks(heads(log_a, GLA_DK))
    b = jnp.cumsum(gc, axis=2)
    b_ref = b[:, :, C // 2:C // 2 + 1]
    b_last = b[:, :, C - 1:]
    causal = jnp.tril(jnp.ones((C, C), dtype=bool))
    A = jnp.einsum('bnihd,bnjhd->bnhij', qc * jnp.exp(b - b_ref), kc * jnp.exp(b_ref - b))
    A = jnp.where(causal, A, 0.0)
    o_intra = jnp.einsum('bnhij,bnjhv->bnihv', A, vc)
    U = jnp.einsum('bnjhd,bnjhv->bnhdv', kc * jnp.exp(b_last - b), vc)
    decay = jnp.exp(b_last[:, :, 0])

    def step(S, inp):
        dec, u = inp
        return dec[..., None] * S + u, S

    S0 = jnp.zeros((B, GLA_HEADS, GLA_DK, GLA_DV), f32)
    _, S_prev = lax.scan(step, S0, (jnp.moveaxis(decay, 1, 0), jnp.moveaxis(U, 1, 0)))
    S_prev = jnp.moveaxis(S_prev, 0, 1)
    o_inter = jnp.einsum('bnihd,bnhdv->bnihv', qc * jnp.exp(b), S_prev)
    o = (o_intra + o_inter).reshape(B, T, GLA_HEADS, GLA_DV)[:, PAD_FRONT:]
    o = _rmsnorm(o, norm_g) * jax.nn.silu(r.astype(f32).reshape(B, L, GLA_HEADS, GLA_DV))
    return o.reshape(B, L, GLA_DV_TOT)


def fox_mixer(q, k, v, f_logit, b_f, norm_g):
    f32 = jnp.float32
    B, L, _ = q.shape
    T = L + PAD_FRONT

    def heads(a):
        return _pad_front(a.astype(f32).reshape(B, L, FOX_HEADS, FOX_HD))

    qh = heads(q) * FOX_HD ** -0.5
    kh = heads(k)
    vh = heads(v)
    log_f = _pad_front(jax.nn.log_sigmoid(f_logit.astype(f32) + b_f.astype(f32)))
    c = jnp.cumsum(log_f, axis=1).transpose(0, 2, 1)
    neg = jnp.finfo(f32).min
    outs = []
    for i in range(T // FOX_BLOCK):
        q0, q1 = i * FOX_BLOCK, (i + 1) * FOX_BLOCK
        s = jnp.einsum('bqhd,bkhd->bhqk', qh[:, q0:q1], kh[:, :q1])
        s = s + c[:, :, q0:q1, None] - c[:, :, None, :q1]
        qpos = jnp.arange(q0, q1)[:, None]
        kpos = jnp.arange(q1)[None, :]
        mask = (kpos <= qpos) & (kpos >= PAD_FRONT)
        p = jax.nn.softmax(jnp.where(mask, s, neg), axis=-1)
        outs.append(jnp.einsum('bhqk,bkhd->bqhd', p, vh[:, :q1]))
    o = jnp.concatenate(outs, axis=1)[:, PAD_FRONT:]
    o = _rmsnorm(o, norm_g)
    return o.reshape(B, L, FOX_W)


def hier_moe(u, w_rg, b_rg, w_re, b_re, w_g, w_u, w_d):
    f32 = jnp.float32
    B, L, D = u.shape
    xt = u.reshape(B * L, D)
    p_group = jax.nn.softmax((xt @ w_rg).astype(f32) + b_rg.astype(f32), axis=-1)
    g_idx = jnp.argmax(p_group, axis=-1)
    p_g = jnp.max(p_group, axis=-1)
    e_logits = jnp.einsum('nd,gde->nge', xt, w_re).astype(f32) + b_re.astype(f32)
    e_logits = jnp.take_along_axis(e_logits, g_idx[:, None, None], axis=1)[:, 0]
    p_e = jax.nn.softmax(e_logits, axis=-1)
    top_v, top_i = lax.top_k(p_e, TOP_K)
    top_v = top_v / jnp.sum(top_v, axis=-1, keepdims=True)
    w_sel = jnp.sum(jax.nn.one_hot(top_i, EXPERTS_PER_GROUP, dtype=f32) * top_v[..., None], axis=1)
    combine = (jax.nn.one_hot(g_idx, N_GROUPS, dtype=f32) * p_g[:, None])[:, :, None] * w_sel[:, None, :]
    y = jnp.zeros((B * L, D), f32)
    for g in range(N_GROUPS):
        hg = jax.nn.silu(jnp.einsum('nd,edf->nef', xt, w_g[g])) * jnp.einsum('nd,edf->nef', xt, w_u[g])
        y = y + jnp.einsum('nef,efd->nd', hg * combine[:, g, :, None].astype(hg.dtype), w_d[g])
    return y.astype(u.dtype).reshape(B, L, D)


def setup_inputs(seed: int = 0) -> dict:
    key = jax.random.key(seed)
    ks = jax.random.split(key, 20)
    nrm = jax.random.normal
    D = D_MODEL
    return {
        "x": nrm(ks[0], (BATCH, SEQ, D), jnp.float32),
        "meta_tokens": nrm(ks[1], (N_META, D), jnp.float32),
        "norm1_g": 1.0 + 0.02 * nrm(ks[2], (DEPTH, D), jnp.float32),
        "w_in": nrm(ks[3], (DEPTH, D, D_IN_PROJ), jnp.float32) * D ** -0.5,
        "b_fox_f": 3.0 + 0.1 * nrm(ks[4], (DEPTH, FOX_HEADS), jnp.float32),
        "gla_w_gate2": nrm(ks[5], (DEPTH, GLA_RANK, GLA_DK_TOT), jnp.float32) * GLA_RANK ** -0.5,
        "gla_b_gate": 0.1 * nrm(ks[6], (DEPTH, GLA_DK_TOT), jnp.float32),
        "gla_norm_g": 1.0 + 0.02 * nrm(ks[7], (DEPTH, GLA_DV), jnp.float32),
        "fox_norm_g": 1.0 + 0.02 * nrm(ks[8], (DEPTH, FOX_HD), jnp.float32),
        "w_out": nrm(ks[9], (DEPTH, D_MIX, D), jnp.float32) * D_MIX ** -0.5,
        "norm2_g": 1.0 + 0.02 * nrm(ks[10], (DEPTH, D), jnp.float32),
        "w_router_group": nrm(ks[11], (DEPTH, D, N_GROUPS), jnp.float32) * D ** -0.5,
        "b_router_group": 0.01 * nrm(ks[12], (DEPTH, N_GROUPS), jnp.float32),
        "w_router_expert": nrm(ks[13], (DEPTH, N_GROUPS, D, EXPERTS_PER_GROUP), jnp.float32) * D ** -0.5,
        "b_router_expert": 0.01 * nrm(ks[14], (DEPTH, N_GROUPS, EXPERTS_PER_GROUP), jnp.float32),
        "w_exp_gate": nrm(ks[15], (DEPTH, N_GROUPS, EXPERTS_PER_GROUP, D, D_EXPERT), jnp.float32) * D ** -0.5,
        "w_exp_up": nrm(ks[16], (DEPTH, N_GROUPS, EXPERTS_PER_GROUP, D, D_EXPERT), jnp.float32) * D ** -0.5,
        "w_exp_down": nrm(ks[17], (DEPTH, N_GROUPS, EXPERTS_PER_GROUP, D_EXPERT, D), jnp.float32) * D_EXPERT ** -0.5,
        "norm_f_g": 1.0 + 0.02 * nrm(ks[18], (D,), jnp.float32),
    }


def reference(x, meta_tokens, norm1_g, w_in, b_fox_f, gla_w_gate2, gla_b_gate, gla_norm_g, fox_norm_g,
              w_out, norm2_g, w_router_group, b_router_group, w_router_expert, b_router_expert,
              w_exp_gate, w_exp_up, w_exp_down, norm_f_g):
    B = x.shape[0]
    meta = jnp.broadcast_to(meta_tokens[None].astype(x.dtype), (B, N_META, D_MODEL))
    h = jnp.concatenate([meta, x], axis=1)
    for l in range(DEPTH):
        u = _rmsnorm(h, norm1_g[l])
        proj = u @ w_in[l]
        fq, fk, fv, ff, gq, gk, gv, gr, gz = _split_cols(proj)
        o_gla = gla_mixer(gq, gk, gv, gr, gz, gla_w_gate2[l], gla_b_gate[l], gla_norm_g[l])
        o_fox = fox_mixer(fq, fk, fv, ff, b_fox_f[l], fox_norm_g[l])
        mix = jnp.concatenate([o_gla, o_fox], axis=-1).astype(h.dtype)
        h = h + mix @ w_out[l]
        u2 = _rmsnorm(h, norm2_g[l])
        h = h + hier_moe(u2, w_router_group[l], b_router_group[l], w_router_expert[l], b_router_expert[l],
                         w_exp_gate[l], w_exp_up[l], w_exp_down[l])
    out = _rmsnorm(h, norm_f_g)[:, N_META:]
    return out
```

```python
import functools

import jax
import jax.numpy as jnp
from jax import lax
from jax.experimental import pallas as pl
from jax.experimental.pallas import tpu as pltpu

D_MODEL = 2048
N_META = 16
GLA_HEADS = 4
GLA_DK = 128
GLA_DV = 256
GLA_DK_TOT = GLA_HEADS * GLA_DK
GLA_DV_TOT = GLA_HEADS * GLA_DV
GLA_RANK = 16
GLA_TAU = 16.0
GLA_CHUNK = 64
FOX_HEADS = 8
FOX_HD = 128
FOX_W = FOX_HEADS * FOX_HD
FOX_BLOCK = 128
PAD_FRONT = FOX_BLOCK - N_META
N_GROUPS = 4
EXPERTS_PER_GROUP = 8
N_EXPERTS = N_GROUPS * EXPERTS_PER_GROUP
D_EXPERT = 512
EPS = 1e-6

LANES = 128
ROW_TILE = 640
PROJ_COL_TILE = 512
GLA_ROWS = 2 * GLA_CHUNK
FOX_TILE = 640
EXPERT_TILE = 256
OUT_TILE = 128
MASK_VALUE = -1e30
PROJ_BIG = 3 * FOX_W + 2 * GLA_DK_TOT + 2 * GLA_DV_TOT
HALF = D_MODEL // 2

F32 = jnp.float32
BF16 = jnp.bfloat16
NT_DIMS = (((1,), (1,)), ((), ()))
TN_DIMS = (((0,), (0,)), ((), ()))
HIGHEST = lax.Precision.HIGHEST


def _log_sigmoid(x):
    return jnp.minimum(x, 0.0) - jnp.log(1.0 + jnp.exp(-jnp.abs(x)))


def _sigmoid(x):
    return 1.0 / (1.0 + jnp.exp(-x))


def _params(semantics, vmem_mb):
    return pltpu.CompilerParams(dimension_semantics=semantics, vmem_limit_bytes=vmem_mb * 1024 * 1024)


def _in_proj_kernel(h_ref, g_ref, wbig_ref, wsmall_ref, wfft_ref, proj_ref, small_ref, fft_ref, xn_ref):
    @pl.when(pl.program_id(1) == 0)
    def _():
        x = h_ref[...]
        ms = jnp.mean(x * x, axis=-1, keepdims=True)
        xn = (x * lax.rsqrt(ms + EPS) * g_ref[...]).astype(BF16)
        xn_ref[...] = xn
        small_ref[...] = jnp.dot(xn, wsmall_ref[...], preferred_element_type=F32)
        fft_ref[...] = lax.dot_general(wfft_ref[...], xn, NT_DIMS, preferred_element_type=F32)

    proj_ref[...] = jnp.dot(xn_ref[...], wbig_ref[...], preferred_element_type=F32).astype(BF16)


def _in_proj(h0, g1, w_big, w_small, w_fft):
    t = h0.shape[0]
    grid = (t // ROW_TILE, PROJ_BIG // PROJ_COL_TILE)
    return pl.pallas_call(
        _in_proj_kernel,
        grid=grid,
        in_specs=[
            pl.BlockSpec((ROW_TILE, D_MODEL), lambda i, j: (i, 0)),
            pl.BlockSpec((1, D_MODEL), lambda i, j: (0, 0)),
            pl.BlockSpec((D_MODEL, PROJ_COL_TILE), lambda i, j: (0, j)),
            pl.BlockSpec((D_MODEL, LANES), lambda i, j: (0, 0)),
            pl.BlockSpec((FOX_HEADS, D_MODEL), lambda i, j: (0, 0)),
        ],
        out_specs=[
            pl.BlockSpec((ROW_TILE, PROJ_COL_TILE), lambda i, j: (i, j)),
            pl.BlockSpec((ROW_TILE, LANES), lambda i, j: (i, 0)),
            pl.BlockSpec((FOX_HEADS, ROW_TILE), lambda i, j: (0, i)),
        ],
        out_shape=[
            jax.ShapeDtypeStruct((t, PROJ_BIG), BF16),
            jax.ShapeDtypeStruct((t, LANES), F32),
            jax.ShapeDtypeStruct((FOX_HEADS, t), F32),
        ],
        scratch_shapes=[pltpu.VMEM((ROW_TILE, D_MODEL), BF16)],
        compiler_params=_params(("arbitrary", "arbitrary"), 40),
    )(h0, g1, w_big, w_small, w_fft)


def _fox_bias_kernel(fft_ref, bf_ref, negc_ref):
    t = fft_ref.shape[1]
    r = lax.broadcasted_iota(jnp.int32, (LANES, LANES), 0)
    c = lax.broadcasted_iota(jnp.int32, (LANES, LANES), 1)
    upper = (r <= c).astype(F32)
    lane = lax.broadcasted_iota(jnp.int32, (FOX_HEADS, LANES), 1)

    def body(b, carry):
        off = pl.multiple_of(b * LANES, LANES)
        valid = (off + lane) >= PAD_FRONT
        lf = _log_sigmoid(fft_ref[:, pl.ds(off, LANES)] + bf_ref[...])
        lf = jnp.where(valid, lf, 0.0)
        cum = jnp.dot(lf, upper, precision=HIGHEST, preferred_element_type=F32) + carry
        negc_ref[:, pl.ds(off, LANES)] = jnp.where(valid, -cum, MASK_VALUE)
        return cum[:, LANES - 1:LANES]

    lax.fori_loop(0, t // LANES, body, jnp.zeros((FOX_HEADS, 1), F32))


def _fox_bias(fft, b_f):
    return pl.pallas_call(
        _fox_bias_kernel,
        out_shape=jax.ShapeDtypeStruct(fft.shape, F32),
    )(fft, b_f)


def _gla_kernel(q_ref, k_ref, v_ref, r_ref, small_ref, w2_ref, bg_ref, ng_ref, o_ref, st_ref):
    i = pl.program_id(0)

    @pl.when(i == 0)
    def _():
        st_ref[...] = jnp.zeros_like(st_ref)

    gate_logit = jnp.dot(small_ref[...], w2_ref[...], precision=HIGHEST, preferred_element_type=F32) + bg_ref[...]
    g = _log_sigmoid(gate_logit) * (1.0 / GLA_TAU)
    rowid = i * GLA_ROWS + lax.broadcasted_iota(jnp.int32, (GLA_ROWS, 1), 0)
    g = jnp.where(rowid >= PAD_FRONT, g, 0.0)

    ci = lax.broadcasted_iota(jnp.int32, (GLA_CHUNK, GLA_CHUNK), 0)
    cj = lax.broadcasted_iota(jnp.int32, (GLA_CHUNK, GLA_CHUNK), 1)
    causal = cj <= ci
    lower = causal.astype(F32)
    scale = GLA_DK ** -0.5
    mid = GLA_CHUNK // 2

    for c in range(GLA_ROWS // GLA_CHUNK):
        rows = slice(c * GLA_CHUNK, (c + 1) * GLA_CHUNK)
        b = jnp.dot(lower, g[rows], precision=HIGHEST, preferred_element_type=F32)
        b_mid = b[mid:mid + 1]
        b_last = b[GLA_CHUNK - 1:GLA_CHUNK]
        q = q_ref[rows, :].astype(F32) * scale
        k = k_ref[rows, :].astype(F32)
        q_intra = (q * jnp.exp(b - b_mid)).astype(BF16)
        k_intra = (k * jnp.exp(b_mid - b)).astype(BF16)
        q_inter = (q * jnp.exp(b)).astype(BF16)
        k_state = (k * jnp.exp(b_last - b)).astype(BF16)
        decay = jnp.exp(b_last)
        for h in range(GLA_HEADS):
            ks = slice(h * GLA_DK, (h + 1) * GLA_DK)
            vs = slice(h * GLA_DV, (h + 1) * GLA_DV)
            v = v_ref[rows, vs]
            a = lax.dot_general(q_intra[:, ks], k_intra[:, ks], NT_DIMS, preferred_element_type=F32)
            a = jnp.where(causal, a, 0.0).astype(BF16)
            o = jnp.dot(a, v, preferred_element_type=F32)
            st = st_ref[h]
            o = o + lax.dot_general(q_inter[:, ks], st.astype(BF16), NT_DIMS, preferred_element_type=F32)
            u_t = lax.dot_general(v, k_state[:, ks], TN_DIMS, preferred_element_type=F32)
            st_ref[h] = decay[:, ks] * st + u_t
            ms = jnp.mean(o * o, axis=-1, keepdims=True)
            y = o * lax.rsqrt(ms + EPS) * ng_ref[...]
            r = r_ref[rows, vs].astype(F32)
            o_ref[rows, vs] = (y * (r * _sigmoid(r))).astype(BF16)


def _gla(proj, small, w2_pad, b_gate, norm_g):
    t = proj.shape[0]
    q_blk = (3 * FOX_W) // GLA_DK_TOT
    v_blk = (3 * FOX_W + 2 * GLA_DK_TOT) // GLA_DV_TOT
    return pl.pallas_call(
        _gla_kernel,
        grid=(t // GLA_ROWS,),
        in_specs=[
            pl.BlockSpec((GLA_ROWS, GLA_DK_TOT), lambda i: (i, q_blk)),
            pl.BlockSpec((GLA_ROWS, GLA_DK_TOT), lambda i: (i, q_blk + 1)),
            pl.BlockSpec((GLA_ROWS, GLA_DV_TOT), lambda i: (i, v_blk)),
            pl.BlockSpec((GLA_ROWS, GLA_DV_TOT), lambda i: (i, v_blk + 1)),
            pl.BlockSpec((GLA_ROWS, LANES), lambda i: (i, 0)),
            pl.BlockSpec((LANES, GLA_DK_TOT), lambda i: (0, 0)),
            pl.BlockSpec((1, GLA_DK_TOT), lambda i: (0, 0)),
            pl.BlockSpec((1, GLA_DV), lambda i: (0, 0)),
        ],
        out_specs=pl.BlockSpec((GLA_ROWS, GLA_DV_TOT), lambda i: (i, 0)),
        out_shape=jax.ShapeDtypeStruct((t, GLA_DV_TOT), BF16),
        scratch_shapes=[pltpu.VMEM((GLA_HEADS, GLA_DV, GLA_DK), F32)],
        compiler_params=_params(("arbitrary",), 32),
    )(proj, proj, proj, proj, small, w2_pad, b_gate, norm_g)


def _fox_kernel(q_ref, k_ref, v_ref, negc_ref, ng_ref, o_ref):
    qi = pl.program_id(1)
    q = (q_ref[...].astype(F32) * (FOX_HD ** -0.5)).astype(BF16)
    row = lax.broadcasted_iota(jnp.int32, (FOX_TILE, FOX_TILE), 0)
    col = lax.broadcasted_iota(jnp.int32, (FOX_TILE, FOX_TILE), 1)

    def step(kt, carry, diagonal):
        m, l, acc = carry
        off = pl.multiple_of(kt * FOX_TILE, FOX_TILE)
        s = lax.dot_general(q, k_ref[pl.ds(off, FOX_TILE), :], NT_DIMS, preferred_element_type=F32)
        s = s + negc_ref[:, pl.ds(off, FOX_TILE)]
        if diagonal:
            s = jnp.where(col <= row, s, MASK_VALUE)
        m_new = jnp.maximum(m, jnp.max(s, axis=-1, keepdims=True))
        alpha = jnp.exp(m - m_new)
        p = jnp.exp(s - m_new)
        l = alpha * l + jnp.sum(p, axis=-1, keepdims=True)
        acc = alpha * acc + jnp.dot(p.astype(BF16), v_ref[pl.ds(off, FOX_TILE), :], preferred_element_type=F32)
        return m_new, l, acc

    init = (jnp.full((FOX_TILE, 1), MASK_VALUE, F32), jnp.zeros((FOX_TILE, 1), F32),
            jnp.zeros((FOX_TILE, FOX_HD), F32))
    carry = lax.fori_loop(0, qi, lambda kt, cr: step(kt, cr, False), init)
    _, l, acc = step(qi, carry, True)
    o = acc / l
    ms = jnp.mean(o * o, axis=-1, keepdims=True)
    o_ref[...] = (o * lax.rsqrt(ms + EPS) * ng_ref[...]).astype(BF16)


def _fox(proj, negc3, norm_g):
    t = proj.shape[0]
    return pl.pallas_call(
        _fox_kernel,
        grid=(FOX_HEADS, t // FOX_TILE),
        in_specs=[
            pl.BlockSpec((FOX_TILE, FOX_HD), lambda h, i: (i, h)),
            pl.BlockSpec((t, FOX_HD), lambda h, i: (0, FOX_HEADS + h)),
            pl.BlockSpec((t, FOX_HD), lambda h, i: (0, 2 * FOX_HEADS + h)),
            pl.BlockSpec((None, 1, t), lambda h, i: (h, 0, 0)),
            pl.BlockSpec((1, FOX_HD), lambda h, i: (0, 0)),
        ],
        out_specs=pl.BlockSpec((FOX_TILE, FOX_HD), lambda h, i: (i, h)),
        out_shape=jax.ShapeDtypeStruct((t, FOX_W), BF16),
        compiler_params=_params(("arbitrary", "arbitrary"), 40),
    )(proj, proj, proj, negc3, norm_g)


ROUTE_E1, ROUTE_E2, ROUTE_R1, ROUTE_R2, ROUTE_W1, ROUTE_W2 = range(6)


def _out_proj_kernel(og_ref, of_ref, h_ref, wo_ref, g2_ref, wr_ref, br_ref,
                     h1_ref, u2p_ref, route_ref, cnt_ref, tri_ref, run_ref):
    i = pl.program_id(0)

    @pl.when(i == 0)
    def _():
        r = lax.broadcasted_iota(jnp.int32, (ROW_TILE, ROW_TILE), 0)
        c = lax.broadcasted_iota(jnp.int32, (ROW_TILE, ROW_TILE), 1)
        tri_ref[...] = jnp.where(c < r, 1.0, 0.0).astype(BF16)
        run_ref[...] = jnp.zeros_like(run_ref)

    h1 = (h_ref[...]
          + jnp.dot(og_ref[...], wo_ref[0:GLA_DV_TOT, :], preferred_element_type=F32)
          + jnp.dot(of_ref[...], wo_ref[GLA_DV_TOT:, :], preferred_element_type=F32))
    h1_ref[...] = h1
    ms = jnp.mean(h1 * h1, axis=-1, keepdims=True)
    u2 = (h1 * lax.rsqrt(ms + EPS) * g2_ref[...]).astype(BF16)
    hi = pltpu.bitcast(u2[:, :HALF].astype(F32), jnp.uint32)
    lo = pltpu.bitcast(u2[:, HALF:].astype(F32), jnp.uint32)
    u2p_ref[...] = hi | (lo >> 16)

    logits = jnp.dot(u2, wr_ref[...], preferred_element_type=F32) + br_ref[...]
    lane = lax.broadcasted_iota(jnp.int32, logits.shape, 1).astype(F32)
    ninf = -jnp.inf

    def first_max(vals):
        top = jnp.max(vals, axis=-1, keepdims=True)
        idx = jnp.min(jnp.where(vals == top, lane, float(LANES)), axis=-1, keepdims=True)
        return top, idx

    gl = jnp.where(lane < N_GROUPS, logits, ninf)
    g_top, g_idx = first_max(gl)
    p_g = 1.0 / jnp.sum(jnp.exp(gl - g_top), axis=-1, keepdims=True)
    e_lo = N_GROUPS + EXPERTS_PER_GROUP * g_idx
    el = jnp.where((lane >= e_lo) & (lane < e_lo + EXPERTS_PER_GROUP), logits, ninf)
    top1, i1 = first_max(el)
    top2, i2 = first_max(jnp.where(lane == i1, ninf, el))
    ratio = jnp.exp(top2 - top1)
    w1 = 1.0 / (1.0 + ratio)
    w2 = ratio * w1

    is1 = lane == i1
    is2 = lane == i2
    onehot = jnp.where(is1 | is2, 1.0, 0.0)
    before = jnp.dot(tri_ref[...], onehot.astype(BF16), preferred_element_type=F32) + run_ref[...]
    r1 = jnp.sum(jnp.where(is1, before, 0.0), axis=-1, keepdims=True)
    r2 = jnp.sum(jnp.where(is2, before, 0.0), axis=-1, keepdims=True)
    run_ref[...] = run_ref[...] + jnp.sum(onehot, axis=0, keepdims=True)
    cnt_ref[...] = run_ref[...]

    rec = jnp.zeros_like(logits)
    for slot, val in ((ROUTE_E1, i1 - N_GROUPS), (ROUTE_E2, i2 - N_GROUPS), (ROUTE_R1, r1), (ROUTE_R2, r2),
                      (ROUTE_W1, p_g * w1), (ROUTE_W2, p_g * w2)):
        rec = jnp.where(lane == slot, val, rec)
    route_ref[...] = rec


def _out_proj(o_gla, o_fox, h0, w_out, g2, w_router, b_router):
    t = h0.shape[0]
    row = lambda i: (i, 0)
    fixed = lambda i: (0, 0)
    return pl.pallas_call(
        _out_proj_kernel,
        grid=(t // ROW_TILE,),
        in_specs=[
            pl.BlockSpec((ROW_TILE, GLA_DV_TOT), row),
            pl.BlockSpec((ROW_TILE, FOX_W), row),
            pl.BlockSpec((ROW_TILE, D_MODEL), row),
            pl.BlockSpec((D_MODEL, D_MODEL), fixed, pipeline_mode=pl.Buffered(1)),
            pl.BlockSpec((1, D_MODEL), fixed),
            pl.BlockSpec((D_MODEL, LANES), fixed),
            pl.BlockSpec((1, LANES), fixed),
        ],
        out_specs=[
            pl.BlockSpec((ROW_TILE, D_MODEL), row),
            pl.BlockSpec((ROW_TILE, HALF), row),
            pl.BlockSpec((ROW_TILE, LANES), row),
            pl.BlockSpec((1, LANES), fixed),
        ],
        out_shape=[
            jax.ShapeDtypeStruct((t, D_MODEL), F32),
            jax.ShapeDtypeStruct((t, HALF), jnp.uint32),
            jax.ShapeDtypeStruct((t, LANES), F32),
            jax.ShapeDtypeStruct((1, LANES), F32),
        ],
        scratch_shapes=[pltpu.VMEM((ROW_TILE, ROW_TILE), BF16), pltpu.VMEM((1, LANES), F32)],
        compiler_params=_params(("arbitrary",), 48),
    )(o_gla, o_fox, h0, w_out, g2, w_router, b_router)


def _scatter_kernel(pos_ref, last_ref, has_ref, nv_ref, u2p_ref, xs_ref, zero_ref, sem, zsem):
    i = pl.program_id(0)
    n_tiles = xs_ref.shape[0] // EXPERT_TILE

    def zero_copy(start):
        start = pl.multiple_of(start, EXPERT_TILE)
        return pltpu.make_async_copy(zero_ref, xs_ref.at[pl.ds(start, EXPERT_TILE), :], zsem)

    def for_each_zero_tile(action):
        for e in range(N_EXPERTS):
            @pl.when(has_ref[e] > 0)
            def _():
                action(zero_copy(last_ref[e]))

        def unused_tile(j, carry):
            action(zero_copy(j * EXPERT_TILE))
            return carry

        lax.fori_loop(nv_ref[0], n_tiles, unused_tile, 0)

    @pl.when(i == 0)
    def _():
        zero_ref[...] = jnp.zeros_like(zero_ref)
        for_each_zero_tile(lambda cp: cp.start())
        for_each_zero_tile(lambda cp: cp.wait())

    base = i * ROW_TILE

    def row_copy(r, slot):
        dst = pos_ref[2 * (base + r) + slot]
        return pltpu.make_async_copy(u2p_ref.at[pl.ds(r, 1), :], xs_ref.at[pl.ds(dst, 1), :], sem)

    def start(r, carry):
        row_copy(r, 0).start()
        row_copy(r, 1).start()
        return carry

    def wait(r, carry):
        row_copy(r, 0).wait()
        row_copy(r, 1).wait()
        return carry

    lax.fori_loop(0, ROW_TILE, start, 0)
    lax.fori_loop(0, ROW_TILE, wait, 0)


def _scatter(pos, last_row, has, n_valid, u2p, n_rows):
    t = u2p.shape[0]
    return pl.pallas_call(
        _scatter_kernel,
        grid_spec=pltpu.PrefetchScalarGridSpec(
            num_scalar_prefetch=4,
            grid=(t // ROW_TILE,),
            in_specs=[pl.BlockSpec((ROW_TILE, HALF), lambda i, *_: (i, 0))],
            out_specs=pl.BlockSpec(memory_space=pl.ANY),
            scratch_shapes=[pltpu.VMEM((EXPERT_TILE, HALF), jnp.uint32),
                            pltpu.SemaphoreType.DMA, pltpu.SemaphoreType.DMA],
        ),
        out_shape=jax.ShapeDtypeStruct((n_rows, HALF), jnp.uint32),
        compiler_params=_params(("arbitrary",), 32),
    )(pos, last_row, has, n_valid, u2p)


def _moe_kernel(te_ref, nv_ref, xs_ref, wg_ref, wu_ref, wd_ref, y_ref, wgb_ref, wub_ref, wdb_ref):
    i = pl.program_id(0)

    @pl.when(i >= nv_ref[0])
    def _():
        y_ref[...] = jnp.zeros_like(y_ref)

    @pl.when(i < nv_ref[0])
    def _():
        prev = te_ref[jnp.maximum(i - 1, 0)]

        @pl.when((i == 0) | (te_ref[i] != prev))
        def _():
            wgb_ref[...] = wg_ref[...].astype(BF16)
            wub_ref[...] = wu_ref[...].astype(BF16)
            wdb_ref[...] = wd_ref[...].astype(BF16)

        xw = xs_ref[...]
        xa = pltpu.bitcast(xw & jnp.uint32(0xFFFF0000), F32).astype(BF16)
        xb = pltpu.bitcast(xw << 16, F32).astype(BF16)
        hg = (jnp.dot(xa, wgb_ref[0:HALF, :], preferred_element_type=F32)
              + jnp.dot(xb, wgb_ref[HALF:, :], preferred_element_type=F32))
        hu = (jnp.dot(xa, wub_ref[0:HALF, :], preferred_element_type=F32)
              + jnp.dot(xb, wub_ref[HALF:, :], preferred_element_type=F32))
        hm = (hg * _sigmoid(hg) * hu).astype(BF16)
        y_ref[...] = jnp.dot(hm, wdb_ref[...], preferred_element_type=F32)


def _moe(tile_expert, n_valid, xs, w_g, w_u, w_d):
    n_rows = xs.shape[0]
    w_in_spec = pl.BlockSpec((None, D_MODEL, D_EXPERT), lambda i, te, nv: (te[i], 0, 0))
    return pl.pallas_call(
        _moe_kernel,
        grid_spec=pltpu.PrefetchScalarGridSpec(
            num_scalar_prefetch=2,
            grid=(n_rows // EXPERT_TILE,),
            in_specs=[
                pl.BlockSpec((EXPERT_TILE, HALF), lambda i, te, nv: (i, 0)),
                w_in_spec,
                w_in_spec,
                pl.BlockSpec((None, D_EXPERT, D_MODEL), lambda i, te, nv: (te[i], 0, 0)),
            ],
            out_specs=pl.BlockSpec((EXPERT_TILE, D_MODEL), lambda i, te, nv: (i, 0)),
            scratch_shapes=[pltpu.VMEM((D_MODEL, D_EXPERT), BF16), pltpu.VMEM((D_MODEL, D_EXPERT), BF16),
                            pltpu.VMEM((D_EXPERT, D_MODEL), BF16)],
        ),
        out_shape=jax.ShapeDtypeStruct((n_rows, D_MODEL), F32),
        compiler_params=_params(("arbitrary",), 52),
    )(tile_expert, n_valid, xs, w_g, w_u, w_d)


def _combine_kernel(pos_ref, h1_ref, route_ref, gf_ref, y_ref, out_ref, ya_ref, yb_ref, sem):
    base = (pl.program_id(0) + 1) * OUT_TILE

    def row_copy(r, slot, dst_ref):
        src = pos_ref[2 * (base + r) + slot]
        return pltpu.make_async_copy(y_ref.at[pl.ds(src, 1), :], dst_ref.at[pl.ds(r, 1), :], sem)

    def start(r, carry):
        row_copy(r, 0, ya_ref).start()
        row_copy(r, 1, yb_ref).start()
        return carry

    def wait(r, carry):
        row_copy(r, 0, ya_ref).wait()
        row_copy(r, 1, yb_ref).wait()
        return carry

    lax.fori_loop(0, OUT_TILE, start, 0)
    lax.fori_loop(0, OUT_TILE, wait, 0)
    rec = route_ref[...]
    hh = (h1_ref[...] + rec[:, ROUTE_W1:ROUTE_W1 + 1] * ya_ref[...]
          + rec[:, ROUTE_W2:ROUTE_W2 + 1] * yb_ref[...])
    ms = jnp.mean(hh * hh, axis=-1, keepdims=True)
    out_ref[...] = hh * lax.rsqrt(ms + EPS) * gf_ref[...]


def _combine(pos, h1, route, g_f, y):
    t = h1.shape[0]
    n_out = t - OUT_TILE
    return pl.pallas_call(
        _combine_kernel,
        grid_spec=pltpu.PrefetchScalarGridSpec(
            num_scalar_prefetch=1,
            grid=(n_out // OUT_TILE,),
            in_specs=[
                pl.BlockSpec((OUT_TILE, D_MODEL), lambda i, *_: (i + 1, 0)),
                pl.BlockSpec((OUT_TILE, LANES), lambda i, *_: (i + 1, 0)),
                pl.BlockSpec((1, D_MODEL), lambda i, *_: (0, 0)),
                pl.BlockSpec(memory_space=pl.ANY),
            ],
            out_specs=pl.BlockSpec((OUT_TILE, D_MODEL), lambda i, *_: (i, 0)),
            scratch_shapes=[pltpu.VMEM((OUT_TILE, D_MODEL), F32), pltpu.VMEM((OUT_TILE, D_MODEL), F32),
                            pltpu.SemaphoreType.DMA],
        ),
        out_shape=jax.ShapeDtypeStruct((n_out, D_MODEL), F32),
        compiler_params=_params(("arbitrary",), 32),
    )(pos, h1, route, g_f, y)


def _routing_tables(route, cnt, n_tiles):
    experts = route[:, ROUTE_E1:ROUTE_E2 + 1].astype(jnp.int32)
    ranks = route[:, ROUTE_R1:ROUTE_R2 + 1].astype(jnp.int32)
    counts = cnt[0, N_GROUPS:N_GROUPS + N_EXPERTS].astype(jnp.int32)
    tiles = (counts + EXPERT_TILE - 1) // EXPERT_TILE
    tile_end = jnp.cumsum(tiles)
    tile_start = tile_end - tiles
    n_valid = tile_end[-1]
    pos = (jnp.take(tile_start, experts) * EXPERT_TILE + ranks).reshape(-1)
    tile = jnp.minimum(jnp.arange(n_tiles, dtype=jnp.int32), n_valid - 1)
    tile_expert = jnp.sum(tile[:, None] >= tile_end[None, :], axis=-1).astype(jnp.int32)
    last_row = (tile_end - 1) * EXPERT_TILE
    return pos, tile_expert, n_valid.reshape(1), last_row, tiles


def kernel(x, meta_tokens, norm1_g, w_in, b_fox_f, gla_w_gate2, gla_b_gate, gla_norm_g, fox_norm_g, w_out,
           norm2_g, w_router_group, b_router_group, w_router_expert, b_router_expert, w_exp_gate, w_exp_up,
           w_exp_down, norm_f_g):
    batch, seq, _ = x.shape
    assert batch == 1 and norm1_g.shape[0] == 1
    t = PAD_FRONT + N_META + seq
    assert t % ROW_TILE == 0 and t % FOX_TILE == 0

    h0 = jnp.concatenate([jnp.zeros((PAD_FRONT, D_MODEL), F32), meta_tokens.astype(F32), x[0]], axis=0)

    w = w_in[0]
    o_ff = 3 * FOX_W
    o_gq = o_ff + FOX_HEADS
    o_gz = o_gq + 2 * GLA_DK_TOT + 2 * GLA_DV_TOT
    w_big = jnp.concatenate([w[:, :o_ff], w[:, o_gq:o_gz]], axis=1).astype(BF16)
    w_small = jnp.zeros((D_MODEL, LANES), F32)
    w_small = w_small.at[:, :FOX_HEADS].set(w[:, o_ff:o_gq])
    w_small = w_small.at[:, FOX_HEADS:FOX_HEADS + GLA_RANK].set(w[:, o_gz:]).astype(BF16)
    w_fft = w[:, o_ff:o_gq].T.astype(BF16)
    proj, small, fft = _in_proj(h0, norm1_g, w_big, w_small, w_fft)

    negc = _fox_bias(fft, b_fox_f[0].reshape(FOX_HEADS, 1))
    w2_pad = jnp.zeros((LANES, GLA_DK_TOT), F32).at[FOX_HEADS:FOX_HEADS + GLA_RANK].set(gla_w_gate2[0])
    o_gla = _gla(proj, small, w2_pad, gla_b_gate, gla_norm_g)
    o_fox = _fox(proj, negc.reshape(FOX_HEADS, 1, t), fox_norm_g)

    w_router = jnp.zeros((D_MODEL, LANES), F32)
    w_router = w_router.at[:, :N_GROUPS].set(w_router_group[0])
    w_router = w_router.at[:, N_GROUPS:N_GROUPS + N_EXPERTS].set(
        jnp.transpose(w_router_expert[0], (1, 0, 2)).reshape(D_MODEL, N_EXPERTS)).astype(BF16)
    b_router = jnp.zeros((1, LANES), F32)
    b_router = b_router.at[0, :N_GROUPS].set(b_router_group[0])
    b_router = b_router.at[0, N_GROUPS:N_GROUPS + N_EXPERTS].set(b_router_expert[0].reshape(-1))
    h1, u2p, route, cnt = _out_proj(o_gla, o_fox, h0, w_out[0].astype(BF16), norm2_g, w_router, b_router)

    n_tiles = (2 * t) // EXPERT_TILE + N_EXPERTS
    pos, tile_expert, n_valid, last_row, tiles = _routing_tables(route, cnt, n_tiles)
    xs = _scatter(pos, last_row, tiles, n_valid, u2p, n_tiles * EXPERT_TILE)
    y = _moe(tile_expert, n_valid, xs,
             w_exp_gate[0].reshape(N_EXPERTS, D_MODEL, D_EXPERT),
             w_exp_up[0].reshape(N_EXPERTS, D_MODEL, D_EXPERT),
             w_exp_down[0].reshape(N_EXPERTS, D_EXPERT, D_MODEL))
    out = _combine(pos, h1, route, norm_f_g.reshape(1, D_MODEL), y)
    return out.reshape(1, seq, D_MODEL)
```

```python
import functools

import jax
import jax.numpy as jnp
from jax import lax
from jax.experimental import pallas as pl
from jax.experimental.pallas import tpu as pltpu

D_MODEL = 2048
N_META = 16
GLA_HEADS = 4
GLA_DK = 128
GLA_DV = 256
GLA_DK_TOT = GLA_HEADS * GLA_DK
GLA_DV_TOT = GLA_HEADS * GLA_DV
GLA_RANK = 16
GLA_TAU = 16.0
GLA_CHUNK = 64
FOX_HEADS = 8
FOX_HD = 128
FOX_W = FOX_HEADS * FOX_HD
FOX_BLOCK = 128
PAD_FRONT = FOX_BLOCK - N_META
N_GROUPS = 4
EXPERTS_PER_GROUP = 8
N_EXPERTS = N_GROUPS * EXPERTS_PER_GROUP
D_EXPERT = 512
EPS = 1e-6

LANES = 128
ROW_TILE = 640
PROJ_COL_TILE = 512
GLA_ROWS = 2 * GLA_CHUNK
FOX_TILE = 640
FOX_ROWS = 128
FOX_KEYS = 640
FOX_GROUP = 2
FOX_SKEW = 3
LOG2E = 1.4426950408889634
EXPERT_TILE = 256
OUT_TILE = 128
MASK_VALUE = -1e30
PROJ_BIG = 3 * FOX_W + 2 * GLA_DK_TOT + 2 * GLA_DV_TOT
HALF = D_MODEL // 2
ROW_SUBLANES = HALF // LANES
OUT_SUBLANES = D_MODEL // LANES
DMA_UNROLL = 8

F32 = jnp.float32
BF16 = jnp.bfloat16
NT_DIMS = (((1,), (1,)), ((), ()))
TN_DIMS = (((0,), (0,)), ((), ()))
HIGHEST = lax.Precision.HIGHEST


def _log_sigmoid(x):
    return jnp.minimum(x, 0.0) - jnp.log(1.0 + jnp.exp(-jnp.abs(x)))


def _sigmoid(x):
    return 1.0 / (1.0 + jnp.exp(-x))


def _rows_load(ref, n_rows, n_chunks):
    return jnp.concatenate([ref[pl.ds(s, n_rows, stride=n_chunks), :] for s in range(n_chunks)], axis=1)


def _rows_store(ref, value, n_chunks):
    n_rows = value.shape[0]
    for s in range(n_chunks):
        ref[pl.ds(s, n_rows, stride=n_chunks), :] = value[:, s * LANES:(s + 1) * LANES]


def _params(semantics, vmem_mb):
    return pltpu.CompilerParams(dimension_semantics=semantics, vmem_limit_bytes=vmem_mb * 1024 * 1024)


def _in_proj_kernel(h_ref, g_ref, wbig_ref, wsmall_ref, wfft_ref, proj_ref, small_ref, fft_ref, xn_ref):
    @pl.when(pl.program_id(1) == 0)
    def _():
        x = h_ref[...]
        ms = jnp.mean(x * x, axis=-1, keepdims=True)
        xn = (x * lax.rsqrt(ms + EPS) * g_ref[...]).astype(BF16)
        xn_ref[...] = xn
        small_ref[...] = jnp.dot(xn, wsmall_ref[...], preferred_element_type=F32)
        fft_ref[...] = lax.dot_general(wfft_ref[...], xn, NT_DIMS, preferred_element_type=F32)

    proj_ref[...] = jnp.dot(xn_ref[...], wbig_ref[...], preferred_element_type=F32).astype(BF16)


def _in_proj(h0, g1, w_big, w_small, w_fft):
    t = h0.shape[0]
    grid = (t // ROW_TILE, PROJ_BIG // PROJ_COL_TILE)
    return pl.pallas_call(
        _in_proj_kernel,
        grid=grid,
        in_specs=[
            pl.BlockSpec((ROW_TILE, D_MODEL), lambda i, j: (i, 0)),
            pl.BlockSpec((1, D_MODEL), lambda i, j: (0, 0)),
            pl.BlockSpec((D_MODEL, PROJ_COL_TILE), lambda i, j: (0, j)),
            pl.BlockSpec((D_MODEL, LANES), lambda i, j: (0, 0)),
            pl.BlockSpec((FOX_HEADS, D_MODEL), lambda i, j: (0, 0)),
        ],
        out_specs=[
            pl.BlockSpec((ROW_TILE, PROJ_COL_TILE), lambda i, j: (i, j)),
            pl.BlockSpec((ROW_TILE, LANES), lambda i, j: (i, 0)),
            pl.BlockSpec((FOX_HEADS, ROW_TILE), lambda i, j: (0, i)),
        ],
        out_shape=[
            jax.ShapeDtypeStruct((t, PROJ_BIG), BF16),
            jax.ShapeDtypeStruct((t, LANES), F32),
            jax.ShapeDtypeStruct((FOX_HEADS, t), F32),
        ],
        scratch_shapes=[pltpu.VMEM((ROW_TILE, D_MODEL), BF16)],
        compiler_params=_params(("arbitrary", "arbitrary"), 40),
    )(h0, g1, w_big, w_small, w_fft)


def _fox_bias_kernel(fft_ref, bf_ref, negc_ref):
    t = fft_ref.shape[1]
    r = lax.broadcasted_iota(jnp.int32, (LANES, LANES), 0)
    c = lax.broadcasted_iota(jnp.int32, (LANES, LANES), 1)
    upper = (r <= c).astype(F32)
    lane = lax.broadcasted_iota(jnp.int32, (FOX_HEADS, LANES), 1)

    def body(b, carry):
        off = pl.multiple_of(b * LANES, LANES)
        valid = (off + lane) >= PAD_FRONT
        lf = _log_sigmoid(fft_ref[:, pl.ds(off, LANES)] + bf_ref[...])
        lf = jnp.where(valid, lf, 0.0)
        cum = jnp.dot(lf, upper, precision=HIGHEST, preferred_element_type=F32) + carry
        negc_ref[:, pl.ds(off, LANES)] = jnp.where(valid, -LOG2E * cum, MASK_VALUE)
        return cum[:, LANES - 1:LANES]

    lax.fori_loop(0, t // LANES, body, jnp.zeros((FOX_HEADS, 1), F32))


def _fox_bias(fft, b_f):
    return pl.pallas_call(
        _fox_bias_kernel,
        out_shape=jax.ShapeDtypeStruct(fft.shape, F32),
    )(fft, b_f)


def _gla_kernel(q_ref, k_ref, v_ref, r_ref, small_ref, w2_ref, bg_ref, ng_ref, o_ref, st_ref):
    i = pl.program_id(0)

    @pl.when(i == 0)
    def _():
        st_ref[...] = jnp.zeros_like(st_ref)

    gate_logit = jnp.dot(small_ref[...], w2_ref[...], precision=HIGHEST, preferred_element_type=F32) + bg_ref[...]
    g = _log_sigmoid(gate_logit) * (1.0 / GLA_TAU)
    rowid = i * GLA_ROWS + lax.broadcasted_iota(jnp.int32, (GLA_ROWS, 1), 0)
    g = jnp.where(rowid >= PAD_FRONT, g, 0.0)

    ci = lax.broadcasted_iota(jnp.int32, (GLA_CHUNK, GLA_CHUNK), 0)
    cj = lax.broadcasted_iota(jnp.int32, (GLA_CHUNK, GLA_CHUNK), 1)
    causal = cj <= ci
    lower = causal.astype(F32)
    scale = GLA_DK ** -0.5
    mid = GLA_CHUNK // 2

    for c in range(GLA_ROWS // GLA_CHUNK):
        rows = slice(c * GLA_CHUNK, (c + 1) * GLA_CHUNK)
        b = jnp.dot(lower, g[rows], precision=HIGHEST, preferred_element_type=F32)
        b_mid = b[mid:mid + 1]
        b_last = b[GLA_CHUNK - 1:GLA_CHUNK]
        q = q_ref[rows, :].astype(F32) * scale
        k = k_ref[rows, :].astype(F32)
        q_intra = (q * jnp.exp(b - b_mid)).astype(BF16)
        k_intra = (k * jnp.exp(b_mid - b)).astype(BF16)
        q_inter = (q * jnp.exp(b)).astype(BF16)
        k_state = (k * jnp.exp(b_last - b)).astype(BF16)
        decay = jnp.exp(b_last)
        for h in range(GLA_HEADS):
            ks = slice(h * GLA_DK, (h + 1) * GLA_DK)
            vs = slice(h * GLA_DV, (h + 1) * GLA_DV)
            v = v_ref[rows, vs]
            a = lax.dot_general(q_intra[:, ks], k_intra[:, ks], NT_DIMS, preferred_element_type=F32)
            a = jnp.where(causal, a, 0.0).astype(BF16)
            o = jnp.dot(a, v, preferred_element_type=F32)
            st = st_ref[h]
            o = o + lax.dot_general(q_inter[:, ks], st.astype(BF16), NT_DIMS, preferred_element_type=F32)
            u_t = lax.dot_general(v, k_state[:, ks], TN_DIMS, preferred_element_type=F32)
            st_ref[h] = decay[:, ks] * st + u_t
            ms = jnp.mean(o * o, axis=-1, keepdims=True)
            y = o * lax.rsqrt(ms + EPS) * ng_ref[...]
            r = r_ref[rows, vs].astype(F32)
            o_ref[rows, vs] = (y * (r * _sigmoid(r))).astype(BF16)


def _gla(proj, small, w2_pad, b_gate, norm_g):
    t = proj.shape[0]
    q_blk = (3 * FOX_W) // GLA_DK_TOT
    v_blk = (3 * FOX_W + 2 * GLA_DK_TOT) // GLA_DV_TOT
    return pl.pallas_call(
        _gla_kernel,
        grid=(t // GLA_ROWS,),
        in_specs=[
            pl.BlockSpec((GLA_ROWS, GLA_DK_TOT), lambda i: (i, q_blk)),
            pl.BlockSpec((GLA_ROWS, GLA_DK_TOT), lambda i: (i, q_blk + 1)),
            pl.BlockSpec((GLA_ROWS, GLA_DV_TOT), lambda i: (i, v_blk)),
            pl.BlockSpec((GLA_ROWS, GLA_DV_TOT), lambda i: (i, v_blk + 1)),
            pl.BlockSpec((GLA_ROWS, LANES), lambda i: (i, 0)),
            pl.BlockSpec((LANES, GLA_DK_TOT), lambda i: (0, 0)),
            pl.BlockSpec((1, GLA_DK_TOT), lambda i: (0, 0)),
            pl.BlockSpec((1, GLA_DV), lambda i: (0, 0)),
        ],
        out_specs=pl.BlockSpec((GLA_ROWS, GLA_DV_TOT), lambda i: (i, 0)),
        out_shape=jax.ShapeDtypeStruct((t, GLA_DV_TOT), BF16),
        scratch_shapes=[pltpu.VMEM((GLA_HEADS, GLA_DV, GLA_DK), F32)],
        compiler_params=_params(("arbitrary",), 32),
    )(proj, proj, proj, proj, small, w2_pad, b_gate, norm_g)


def _fox_kernel(q_ref, k_ref, v_ref, negc_ref, ng_ref, o_ref, qs_ref):
    qi = pl.program_id(1)
    n_blocks = FOX_TILE // FOX_ROWS
    units = [(hh, rb) for hh in range(FOX_GROUP) for rb in range(n_blocks)]
    qs_ref[...] = (q_ref[...].astype(F32) * (FOX_HD ** -0.5 * LOG2E)).astype(BF16)
    row = lax.broadcasted_iota(jnp.int32, (FOX_ROWS, FOX_ROWS), 0)
    col = lax.broadcasted_iota(jnp.int32, (FOX_ROWS, FOX_ROWS), 1)

    def tile(kt, state, diagonal):
        off = pl.multiple_of(kt * FOX_TILE, FOX_TILE)
        steps = []
        for k0 in range(0, FOX_TILE, FOX_KEYS):
            for u, (hh, rb) in enumerate(units):
                k1 = min(k0 + FOX_KEYS, (rb + 1) * FOX_ROWS if diagonal else FOX_TILE)
                if k1 > k0:
                    steps.append((u, k0, k1))

        def scores(step):
            u, k0, k1 = step
            hh, rb = units[u]
            rows = slice(rb * FOX_ROWS, (rb + 1) * FOX_ROWS)
            cols = slice(hh * FOX_HD, (hh + 1) * FOX_HD)
            s = lax.dot_general(qs_ref[rows, cols], k_ref[pl.ds(off + k0, k1 - k0), cols], NT_DIMS,
                                preferred_element_type=F32)
            s = s + negc_ref[hh, :, pl.ds(off + k0, k1 - k0)]
            if diagonal and k1 == (rb + 1) * FOX_ROWS:
                tail = jnp.where(col <= row, s[:, k1 - k0 - FOX_ROWS:], MASK_VALUE)
                s = tail if k1 - k0 == FOX_ROWS else jnp.concatenate([s[:, :k1 - k0 - FOX_ROWS], tail], axis=1)
            return s

        def update(step, s, state):
            u, k0, k1 = step
            hh, _ = units[u]
            m_prev, l_prev, acc_prev = state[u]
            m_new = jnp.maximum(m_prev, jnp.max(s, axis=-1, keepdims=True))
            alpha = jnp.exp2(m_prev - m_new)
            p = jnp.exp2(s - m_new)
            l_new = alpha * l_prev + jnp.sum(p, axis=-1, keepdims=True)
            acc_new = alpha * acc_prev + jnp.dot(
                p.astype(BF16), v_ref[pl.ds(off + k0, k1 - k0), hh * FOX_HD:(hh + 1) * FOX_HD],
                preferred_element_type=F32)
            state[u] = (m_new, l_new, acc_new)

        state = list(state)
        pending = [scores(st) for st in steps[:FOX_SKEW]]
        for j, st in enumerate(steps):
            if j + FOX_SKEW < len(steps):
                pending.append(scores(steps[j + FOX_SKEW]))
            update(st, pending[j], state)
            pending[j] = None
        return tuple(state)

    init = tuple((jnp.full((FOX_ROWS, 1), MASK_VALUE, F32), jnp.zeros((FOX_ROWS, 1), F32),
                  jnp.zeros((FOX_ROWS, FOX_HD), F32)) for _ in units)
    state = lax.fori_loop(0, qi, lambda kt, st: tile(kt, st, False), init)
    state = tile(qi, state, True)
    for u, (hh, rb) in enumerate(units):
        _, l, acc = state[u]
        o = acc / l
        ms = jnp.mean(o * o, axis=-1, keepdims=True)
        o_ref[rb * FOX_ROWS:(rb + 1) * FOX_ROWS, hh * FOX_HD:(hh + 1) * FOX_HD] = (
            o * lax.rsqrt(ms + EPS) * ng_ref[...]).astype(BF16)


def _fox(proj, negc3, norm_g):
    t = proj.shape[0]
    width = FOX_GROUP * FOX_HD
    k_blk = FOX_W // width
    return pl.pallas_call(
        _fox_kernel,
        grid=(FOX_HEADS // FOX_GROUP, t // FOX_TILE),
        in_specs=[
            pl.BlockSpec((FOX_TILE, width), lambda g, i: (i, g)),
            pl.BlockSpec((t, width), lambda g, i: (0, k_blk + g)),
            pl.BlockSpec((t, width), lambda g, i: (0, 2 * k_blk + g)),
            pl.BlockSpec((FOX_GROUP, 1, t), lambda g, i: (g, 0, 0)),
            pl.BlockSpec((1, FOX_HD), lambda g, i: (0, 0)),
        ],
        out_specs=pl.BlockSpec((FOX_TILE, width), lambda g, i: (i, g)),
        out_shape=jax.ShapeDtypeStruct((t, FOX_W), BF16),
        scratch_shapes=[pltpu.VMEM((FOX_TILE, width), BF16)],
        compiler_params=_params(("arbitrary", "arbitrary"), 40),
    )(proj, proj, proj, negc3, norm_g)


ROUTE_E1, ROUTE_E2, ROUTE_R1, ROUTE_R2, ROUTE_W1, ROUTE_W2 = range(6)


def _out_proj_kernel(og_ref, of_ref, h_ref, wo_ref, g2_ref, wr_ref, br_ref,
                     h1_ref, u2p_ref, route_ref, cnt_ref, tri_ref, run_ref):
    i = pl.program_id(0)

    @pl.when(i == 0)
    def _():
        r = lax.broadcasted_iota(jnp.int32, (ROW_TILE, ROW_TILE), 0)
        c = lax.broadcasted_iota(jnp.int32, (ROW_TILE, ROW_TILE), 1)
        tri_ref[...] = jnp.where(c < r, 1.0, 0.0).astype(BF16)
        run_ref[...] = jnp.zeros_like(run_ref)

    h1 = (h_ref[...]
          + jnp.dot(og_ref[...], wo_ref[0:GLA_DV_TOT, :], preferred_element_type=F32)
          + jnp.dot(of_ref[...], wo_ref[GLA_DV_TOT:, :], preferred_element_type=F32))
    h1_ref[...] = h1
    ms = jnp.mean(h1 * h1, axis=-1, keepdims=True)
    u2 = (h1 * lax.rsqrt(ms + EPS) * g2_ref[...]).astype(BF16)
    hi = pltpu.bitcast(u2[:, :HALF].astype(F32), jnp.uint32)
    lo = pltpu.bitcast(u2[:, HALF:].astype(F32), jnp.uint32)
    packed = hi | (lo >> 16)
    _rows_store(u2p_ref, packed, ROW_SUBLANES)

    logits = jnp.dot(u2, wr_ref[...], preferred_element_type=F32) + br_ref[...]
    lane = lax.broadcasted_iota(jnp.int32, logits.shape, 1).astype(F32)
    ninf = -jnp.inf

    def first_max(vals):
        top = jnp.max(vals, axis=-1, keepdims=True)
        idx = jnp.min(jnp.where(vals == top, lane, float(LANES)), axis=-1, keepdims=True)
        return top, idx

    gl = jnp.where(lane < N_GROUPS, logits, ninf)
    g_top, g_idx = first_max(gl)
    p_g = 1.0 / jnp.sum(jnp.exp(gl - g_top), axis=-1, keepdims=True)
    e_lo = N_GROUPS + EXPERTS_PER_GROUP * g_idx
    el = jnp.where((lane >= e_lo) & (lane < e_lo + EXPERTS_PER_GROUP), logits, ninf)
    top1, i1 = first_max(el)
    top2, i2 = first_max(jnp.where(lane == i1, ninf, el))
    ratio = jnp.exp(top2 - top1)
    w1 = 1.0 / (1.0 + ratio)
    w2 = ratio * w1

    is1 = lane == i1
    is2 = lane == i2
    onehot = jnp.where(is1 | is2, 1.0, 0.0)
    before = jnp.dot(tri_ref[...], onehot.astype(BF16), preferred_element_type=F32) + run_ref[...]
    r1 = jnp.sum(jnp.where(is1, before, 0.0), axis=-1, keepdims=True)
    r2 = jnp.sum(jnp.where(is2, before, 0.0), axis=-1, keepdims=True)
    run_ref[...] = run_ref[...] + jnp.sum(onehot, axis=0, keepdims=True)
    cnt_ref[...] = run_ref[...]

    rec = jnp.zeros_like(logits)
    for slot, val in ((ROUTE_E1, i1 - N_GROUPS), (ROUTE_E2, i2 - N_GROUPS), (ROUTE_R1, r1), (ROUTE_R2, r2),
                      (ROUTE_W1, p_g * w1), (ROUTE_W2, p_g * w2)):
        rec = jnp.where(lane == slot, val, rec)
    route_ref[...] = rec


def _out_proj(o_gla, o_fox, h0, w_out, g2, w_router, b_router):
    t = h0.shape[0]
    row = lambda i: (i, 0)
    fixed = lambda i: (0, 0)
    return pl.pallas_call(
        _out_proj_kernel,
        grid=(t // ROW_TILE,),
        in_specs=[
            pl.BlockSpec((ROW_TILE, GLA_DV_TOT), row),
            pl.BlockSpec((ROW_TILE, FOX_W), row),
            pl.BlockSpec((ROW_TILE, D_MODEL), row),
            pl.BlockSpec((D_MODEL, D_MODEL), fixed, pipeline_mode=pl.Buffered(1)),
            pl.BlockSpec((1, D_MODEL), fixed),
            pl.BlockSpec((D_MODEL, LANES), fixed),
            pl.BlockSpec((1, LANES), fixed),
        ],
        out_specs=[
            pl.BlockSpec((ROW_TILE, D_MODEL), row),
            pl.BlockSpec((ROW_TILE * ROW_SUBLANES, LANES), row),
            pl.BlockSpec((ROW_TILE, LANES), row),
            pl.BlockSpec((1, LANES), fixed),
        ],
        out_shape=[
            jax.ShapeDtypeStruct((t, D_MODEL), F32),
            jax.ShapeDtypeStruct((t * ROW_SUBLANES, LANES), jnp.uint32),
            jax.ShapeDtypeStruct((t, LANES), F32),
            jax.ShapeDtypeStruct((1, LANES), F32),
        ],
        scratch_shapes=[pltpu.VMEM((ROW_TILE, ROW_TILE), BF16), pltpu.VMEM((1, LANES), F32)],
        compiler_params=_params(("arbitrary",), 48),
    )(o_gla, o_fox, h0, w_out, g2, w_router, b_router)


def _scatter_kernel(pos_ref, last_ref, has_ref, nv_ref, u2p_ref, xs_ref, zero_ref, sem, zsem):
    i = pl.program_id(0)
    tile_rows = EXPERT_TILE * ROW_SUBLANES
    n_tiles = xs_ref.shape[0] // tile_rows

    def zero_copy(start):
        start = pl.multiple_of(start * ROW_SUBLANES, tile_rows)
        return pltpu.make_async_copy(zero_ref, xs_ref.at[pl.ds(start, tile_rows), :], zsem)

    def for_each_zero_tile(action):
        for e in range(N_EXPERTS):
            @pl.when(has_ref[e] > 0)
            def _():
                action(zero_copy(last_ref[e]))

        def unused_tile(j, carry):
            action(zero_copy(j * EXPERT_TILE))
            return carry

        lax.fori_loop(nv_ref[0], n_tiles, unused_tile, 0)

    @pl.when(i == 0)
    def _():
        zero_ref[...] = jnp.zeros_like(zero_ref)
        for_each_zero_tile(lambda cp: cp.start())
        for_each_zero_tile(lambda cp: cp.wait())

    base = i * ROW_TILE

    def row_copy(r, slot):
        dst = pos_ref[2 * (base + r) + slot]
        src = pl.multiple_of(r * ROW_SUBLANES, ROW_SUBLANES)
        dst = pl.multiple_of(dst * ROW_SUBLANES, ROW_SUBLANES)
        return pltpu.make_async_copy(u2p_ref.at[pl.ds(src, ROW_SUBLANES), :],
                                     xs_ref.at[pl.ds(dst, ROW_SUBLANES), :], sem)

    def for_each_row(action):
        def group(j, carry):
            for k in range(DMA_UNROLL):
                action(row_copy(j * DMA_UNROLL + k, 0))
                action(row_copy(j * DMA_UNROLL + k, 1))
            return carry

        lax.fori_loop(0, ROW_TILE // DMA_UNROLL, group, 0)

    for_each_row(lambda cp: cp.start())
    for_each_row(lambda cp: cp.wait())


def _scatter(pos, last_row, has, n_valid, u2p, n_rows):
    t = u2p.shape[0] // ROW_SUBLANES
    return pl.pallas_call(
        _scatter_kernel,
        grid_spec=pltpu.PrefetchScalarGridSpec(
            num_scalar_prefetch=4,
            grid=(t // ROW_TILE,),
            in_specs=[pl.BlockSpec((ROW_TILE * ROW_SUBLANES, LANES), lambda i, *_: (i, 0))],
            out_specs=pl.BlockSpec(memory_space=pl.ANY),
            scratch_shapes=[pltpu.VMEM((EXPERT_TILE * ROW_SUBLANES, LANES), jnp.uint32),
                            pltpu.SemaphoreType.DMA, pltpu.SemaphoreType.DMA],
        ),
        out_shape=jax.ShapeDtypeStruct((n_rows * ROW_SUBLANES, LANES), jnp.uint32),
        compiler_params=_params(("arbitrary",), 32),
    )(pos, last_row, has, n_valid, u2p)


def _moe_kernel(te_ref, nv_ref, xs_ref, wg_ref, wu_ref, wd_ref, y_ref, wgb_ref, wub_ref, wdb_ref):
    i = pl.program_id(0)

    @pl.when(i >= nv_ref[0])
    def _():
        y_ref[...] = jnp.zeros_like(y_ref)

    @pl.when(i < nv_ref[0])
    def _():
        prev = te_ref[jnp.maximum(i - 1, 0)]

        @pl.when((i == 0) | (te_ref[i] != prev))
        def _():
            wgb_ref[...] = wg_ref[...].astype(BF16)
            wub_ref[...] = wu_ref[...].astype(BF16)
            wdb_ref[...] = wd_ref[...].astype(BF16)

        xw = _rows_load(xs_ref, EXPERT_TILE, ROW_SUBLANES)
        xa =pltpu.bitcast(xw & jnp.uint32(0xFFFF0000), F32).astype(BF16)
        xb = pltpu.bitcast(xw << 16, F32).astype(BF16)
        hg = (jnp.dot(xa, wgb_ref[0:HALF, :], preferred_element_type=F32)
              + jnp.dot(xb, wgb_ref[HALF:, :], preferred_element_type=F32))
        hu = (jnp.dot(xa, wub_ref[0:HALF, :], preferred_element_type=F32)
              + jnp.dot(xb, wub_ref[HALF:, :], preferred_element_type=F32))
        hm = (hg * _sigmoid(hg) * hu).astype(BF16)
        y = jnp.dot(hm, wdb_ref[...], preferred_element_type=F32)
        _rows_store(y_ref, y, OUT_SUBLANES)


def _moe(tile_expert, n_valid, xs, w_g, w_u, w_d):
    n_rows = xs.shape[0] // ROW_SUBLANES
    w_in_spec = pl.BlockSpec((None, D_MODEL, D_EXPERT), lambda i, te, nv: (te[i], 0, 0))
    return pl.pallas_call(
        _moe_kernel,
        grid_spec=pltpu.PrefetchScalarGridSpec(
            num_scalar_prefetch=2,
            grid=(n_rows // EXPERT_TILE,),
            in_specs=[
                pl.BlockSpec((EXPERT_TILE * ROW_SUBLANES, LANES), lambda i, te, nv: (i, 0)),
                w_in_spec,
                w_in_spec,
                pl.BlockSpec((None, D_EXPERT, D_MODEL), lambda i, te, nv: (te[i], 0, 0)),
            ],
            out_specs=pl.BlockSpec((EXPERT_TILE * OUT_SUBLANES, LANES), lambda i, te, nv: (i, 0)),
            scratch_shapes=[pltpu.VMEM((D_MODEL, D_EXPERT), BF16), pltpu.VMEM((D_MODEL, D_EXPERT), BF16),
                            pltpu.VMEM((D_EXPERT, D_MODEL), BF16)],
        ),
        out_shape=jax.ShapeDtypeStruct((n_rows * OUT_SUBLANES, LANES), F32),
        compiler_params=_params(("arbitrary",), 52),
    )(tile_expert, n_valid, xs, w_g, w_u, w_d)


def _combine_kernel(pos_ref, h1_ref, route_ref, gf_ref, y_ref, out_ref, ya_ref, yb_ref, sem):
    base = (pl.program_id(0) + 1) * OUT_TILE

    def row_copy(r, slot, dst_ref):
        src = pos_ref[2 * (base + r) + slot]
        src = pl.multiple_of(src * OUT_SUBLANES, OUT_SUBLANES)
        dst = pl.multiple_of(r * OUT_SUBLANES, OUT_SUBLANES)
        return pltpu.make_async_copy(y_ref.at[pl.ds(src, OUT_SUBLANES), :],
                                     dst_ref.at[pl.ds(dst, OUT_SUBLANES), :], sem)

    def for_each_row(action):
        def group(j, carry):
            for k in range(DMA_UNROLL):
                action(row_copy(j * DMA_UNROLL + k, 0, ya_ref))
                action(row_copy(j * DMA_UNROLL + k, 1, yb_ref))
            return carry

        lax.fori_loop(0, OUT_TILE // DMA_UNROLL, group, 0)

    for_each_row(lambda cp: cp.start())
    for_each_row(lambda cp: cp.wait())
    rec = route_ref[...]
    ya = _rows_load(ya_ref, OUT_TILE, OUT_SUBLANES)
    yb = _rows_load(yb_ref, OUT_TILE, OUT_SUBLANES)
    hh = h1_ref[...] + rec[:, ROUTE_W1:ROUTE_W1 + 1] * ya + rec[:, ROUTE_W2:ROUTE_W2 + 1] * yb
    ms = jnp.mean(hh * hh, axis=-1, keepdims=True)
    out_ref[...] = hh * lax.rsqrt(ms + EPS) * gf_ref[...]


def _combine(pos, h1, route, g_f, y):
    t = h1.shape[0]
    n_out = t - OUT_TILE
    return pl.pallas_call(
        _combine_kernel,
        grid_spec=pltpu.PrefetchScalarGridSpec(
            num_scalar_prefetch=1,
            grid=(n_out // OUT_TILE,),
            in_specs=[
                pl.BlockSpec((OUT_TILE, D_MODEL), lambda i, *_: (i + 1, 0)),
                pl.BlockSpec((OUT_TILE, LANES), lambda i, *_: (i + 1, 0)),
                pl.BlockSpec((1, D_MODEL), lambda i, *_: (0, 0)),
                pl.BlockSpec(memory_space=pl.ANY),
            ],
            out_specs=pl.BlockSpec((OUT_TILE, D_MODEL), lambda i, *_: (i, 0)),
            scratch_shapes=[pltpu.VMEM((OUT_TILE * OUT_SUBLANES, LANES), F32),
                            pltpu.VMEM((OUT_TILE * OUT_SUBLANES, LANES), F32),
                            pltpu.SemaphoreType.DMA],
        ),
        out_shape=jax.ShapeDtypeStruct((n_out, D_MODEL), F32),
        compiler_params=_params(("arbitrary",), 32),
    )(pos, h1, route, g_f, y)


def _routing_tables(route, cnt, n_tiles):
    experts = route[:, ROUTE_E1:ROUTE_E2 + 1].astype(jnp.int32)
    ranks = route[:, ROUTE_R1:ROUTE_R2 + 1].astype(jnp.int32)
    counts = cnt[0, N_GROUPS:N_GROUPS + N_EXPERTS].astype(jnp.int32)
    tiles = (counts + EXPERT_TILE - 1) // EXPERT_TILE
    tile_end = jnp.cumsum(tiles)
    tile_start = tile_end - tiles
    n_valid = tile_end[-1]
    pos = (jnp.take(tile_start, experts) * EXPERT_TILE + ranks).reshape(-1)
    tile = jnp.minimum(jnp.arange(n_tiles, dtype=jnp.int32), n_valid - 1)
    tile_expert = jnp.sum(tile[:, None] >= tile_end[None, :], axis=-1).astype(jnp.int32)
    last_row = (tile_end - 1) * EXPERT_TILE
    return pos, tile_expert, n_valid.reshape(1), last_row, tiles


def kernel(x, meta_tokens, norm1_g, w_in, b_fox_f, gla_w_gate2, gla_b_gate, gla_norm_g, fox_norm_g, w_out,
           norm2_g, w_router_group, b_router_group, w_router_expert, b_router_expert, w_exp_gate, w_exp_up,
           w_exp_down, norm_f_g):
    batch, seq, _ = x.shape
    assert batch == 1 and norm1_g.shape[0] == 1
    t = PAD_FRONT + N_META + seq
    assert t % ROW_TILE == 0 and t % FOX_TILE == 0

    h0 = jnp.concatenate([jnp.zeros((PAD_FRONT, D_MODEL), F32), meta_tokens.astype(F32), x[0]], axis=0)

    w = w_in[0]
    o_ff = 3 * FOX_W
    o_gq = o_ff + FOX_HEADS
    o_gz = o_gq + 2 * GLA_DK_TOT + 2 * GLA_DV_TOT
    w_big = jnp.concatenate([w[:, :o_ff], w[:, o_gq:o_gz]], axis=1).astype(BF16)
    w_small = jnp.zeros((D_MODEL, LANES), F32)
    w_small = w_small.at[:, :FOX_HEADS].set(w[:, o_ff:o_gq])
    w_small = w_small.at[:, FOX_HEADS:FOX_HEADS + GLA_RANK].set(w[:, o_gz:]).astype(BF16)
    w_fft = w[:, o_ff:o_gq].T.astype(BF16)
    proj, small, fft = _in_proj(h0, norm1_g, w_big, w_small, w_fft)

    negc = _fox_bias(fft, b_fox_f[0].reshape(FOX_HEADS, 1))
    w2_pad = jnp.zeros((LANES, GLA_DK_TOT), F32).at[FOX_HEADS:FOX_HEADS + GLA_RANK].set(gla_w_gate2[0])
    o_gla = _gla(proj, small, w2_pad, gla_b_gate, gla_norm_g)
    o_fox = _fox(proj, negc.reshape(FOX_HEADS, 1, t), fox_norm_g)

    w_router = jnp.zeros((D_MODEL, LANES), F32)
    w_router = w_router.at[:, :N_GROUPS].set(w_router_group[0])
    w_router = w_router.at[:, N_GROUPS:N_GROUPS + N_EXPERTS].set(
        jnp.transpose(w_router_expert[0], (1, 0, 2)).reshape(D_MODEL, N_EXPERTS)).astype(BF16)
    b_router = jnp.zeros((1, LANES), F32)
    b_router = b_router.at[0, :N_GROUPS].set(b_router_group[0])
    b_router = b_router.at[0, N_GROUPS:N_GROUPS + N_EXPERTS].set(b_router_expert[0].reshape(-1))
    h1, u2p, route, cnt = _out_proj(o_gla, o_fox, h0, w_out[0].astype(BF16), norm2_g, w_router, b_router)

    n_tiles = (2 * t) // EXPERT_TILE + N_EXPERTS
    pos, tile_expert, n_valid, last_row, tiles = _routing_tables(route, cnt, n_tiles)
    xs = _scatter(pos, last_row, tiles, n_valid, u2p, n_tiles * EXPERT_TILE)
    y = _moe(tile_expert, n_valid, xs,
             w_exp_gate[0].reshape(N_EXPERTS, D_MODEL, D_EXPERT),
             w_exp_up[0].reshape(N_EXPERTS, D_MODEL, D_EXPERT),
             w_exp_down[0].reshape(N_EXPERTS, D_EXPERT, D_MODEL))
    out = _combine(pos, h1, route, norm_f_g.reshape(1, D_MODEL), y)
    return out.reshape(1, seq, D_MODEL)
```

```python
import jax
import jax.numpy as jnp
from jax import lax
from jax.experimental import pallas as pl
from jax.experimental.pallas import tpu as pltpu

D_MODEL = 2048
N_META = 16
GLA_HEADS = 4
GLA_DK = 128
GLA_DV = 256
GLA_DK_TOT = GLA_HEADS * GLA_DK
GLA_DV_TOT = GLA_HEADS * GLA_DV
GLA_RANK = 16
GLA_TAU = 16.0
GLA_CHUNK = 64
FOX_HEADS = 8
FOX_HD = 128
FOX_W = FOX_HEADS * FOX_HD
FOX_BLOCK = 128
PAD_FRONT = FOX_BLOCK - N_META
HEAD_ROWS = PAD_FRONT + N_META
N_GROUPS = 4
EXPERTS_PER_GROUP = 8
N_EXPERTS = N_GROUPS * EXPERTS_PER_GROUP
D_EXPERT = 512
EPS = 1e-6

LANES = 128
PROJ_ROWS = HEAD_ROWS
GLA_ROWS = 2 * GLA_CHUNK
FOX_TILE = 1024
FOX_ROWS = 128
FOX_KEYS = 1024
FOX_GROUP = 2
FOX_SKEW = 3
LOG2E = 1.4426950408889634
ROW_TILE = 512
EXPERT_TILE = 256
OUT_TILE = 256
MASK_VALUE = -1e30
PROJ_BIG = 3 * FOX_W + 2 * GLA_DK_TOT + 2 * GLA_DV_TOT
HALF = D_MODEL // 2
ROW_SUBLANES = HALF // LANES
OUT_SUBLANES = D_MODEL // LANES
DMA_UNROLL = 8

F32 = jnp.float32
BF16 = jnp.bfloat16
NT_DIMS = (((1,), (1,)), ((), ()))
TN_DIMS = (((0,), (0,)), ((), ()))
HIGHEST = lax.Precision.HIGHEST


def _log_sigmoid(x):
    return jnp.minimum(x, 0.0) - jnp.log(1.0 + jnp.exp(-jnp.abs(x)))


def _sigmoid(x):
    return 1.0 / (1.0 + jnp.exp(-x))


def _rows_load(ref, n_rows, n_chunks):
    return jnp.concatenate([ref[pl.ds(s, n_rows, stride=n_chunks), :] for s in range(n_chunks)], axis=1)


def _rows_store(ref, value, n_chunks):
    n_rows = value.shape[0]
    for s in range(n_chunks):
        ref[pl.ds(s, n_rows, stride=n_chunks), :] = value[:, s * LANES:(s + 1) * LANES]


def _params(semantics, vmem_mb):
    return pltpu.CompilerParams(dimension_semantics=semantics, vmem_limit_bytes=vmem_mb * 1024 * 1024)


def _in_proj_kernel(head_ref, x_ref, g_ref, wbig_ref, wsmall_ref, proj_ref, small_ref):
    x = jnp.where(pl.program_id(0) == 0, head_ref[...], x_ref[...])
    ms = jnp.mean(x * x, axis=-1, keepdims=True)
    xn = (x * lax.rsqrt(ms + EPS) * g_ref[...]).astype(BF16)
    small_ref[...] = jnp.dot(xn, wsmall_ref[...], preferred_element_type=F32)
    proj_ref[...] = jnp.dot(xn, wbig_ref[...], preferred_element_type=F32).astype(BF16)


def _in_proj(head, x, g1, w_big, w_small):
    t = HEAD_ROWS + x.shape[0]
    fixed = lambda i: (0, 0)
    return pl.pallas_call(
        _in_proj_kernel,
        grid=(t // PROJ_ROWS,),
        in_specs=[
            pl.BlockSpec((PROJ_ROWS, D_MODEL), fixed),
            pl.BlockSpec((PROJ_ROWS, D_MODEL), lambda i: (jnp.maximum(i - 1, 0), 0)),
            pl.BlockSpec((1, D_MODEL), fixed),
            pl.BlockSpec((D_MODEL, PROJ_BIG), fixed, pipeline_mode=pl.Buffered(1)),
            pl.BlockSpec((D_MODEL, LANES), fixed),
        ],
        out_specs=[
            pl.BlockSpec((PROJ_ROWS, PROJ_BIG), lambda i: (i, 0)),
            pl.BlockSpec((PROJ_ROWS, LANES), lambda i: (i, 0)),
        ],
        out_shape=[
            jax.ShapeDtypeStruct((t, PROJ_BIG), BF16),
            jax.ShapeDtypeStruct((t, LANES), F32),
        ],
        compiler_params=_params(("arbitrary",), 48),
    )(head, x, g1, w_big, w_small)


def _fox_bias_kernel(small_ref, bf_ref, negc_ref):
    t = small_ref.shape[0]
    r = lax.broadcasted_iota(jnp.int32, (LANES, LANES), 0)
    c = lax.broadcasted_iota(jnp.int32, (LANES, LANES), 1)
    upper = (r <= c).astype(F32)
    lane = lax.broadcasted_iota(jnp.int32, (FOX_HEADS, LANES), 1)

    def body(b, carry):
        off = pl.multiple_of(b * LANES, LANES)
        valid = (off + lane) >= PAD_FRONT
        f_logit = small_ref[pl.ds(off, LANES), :].T[0:FOX_HEADS, :]
        lf = jnp.where(valid, _log_sigmoid(f_logit + bf_ref[...]), 0.0)
        cum = jnp.dot(lf, upper, precision=HIGHEST, preferred_element_type=F32) + carry
        negc_ref[:, pl.ds(off, LANES)] = jnp.where(valid, -LOG2E * cum, MASK_VALUE)
        return cum[:, LANES - 1:LANES]

    lax.fori_loop(0, t // LANES, body, jnp.zeros((FOX_HEADS, 1), F32))


def _fox_bias(small, b_f):
    return pl.pallas_call(
        _fox_bias_kernel,
        out_shape=jax.ShapeDtypeStruct((FOX_HEADS, small.shape[0]), F32),
    )(small, b_f)


def _gla_kernel(q_ref, k_ref, v_ref, r_ref, small_ref, w2_ref, bg_ref, ng_ref, o_ref, st_ref):
    i = pl.program_id(0)

    @pl.when(i == 0)
    def _():
        st_ref[...] = jnp.zeros_like(st_ref)

    gate_logit = jnp.dot(small_ref[...], w2_ref[...], precision=HIGHEST, preferred_element_type=F32) + bg_ref[...]
    g = _log_sigmoid(gate_logit) * (1.0 / GLA_TAU)
    rowid = i * GLA_ROWS + lax.broadcasted_iota(jnp.int32, (GLA_ROWS, 1), 0)
    g = jnp.where(rowid >= PAD_FRONT, g, 0.0)

    ci = lax.broadcasted_iota(jnp.int32, (GLA_CHUNK, GLA_CHUNK), 0)
    cj = lax.broadcasted_iota(jnp.int32, (GLA_CHUNK, GLA_CHUNK), 1)
    causal = cj <= ci
    lower = causal.astype(F32)
    scale = GLA_DK ** -0.5
    mid = GLA_CHUNK // 2

    for c in range(GLA_ROWS // GLA_CHUNK):
        rows = slice(c * GLA_CHUNK, (c + 1) * GLA_CHUNK)
        b = jnp.dot(lower, g[rows], precision=HIGHEST, preferred_element_type=F32)
        b_mid = b[mid:mid + 1]
        b_last = b[GLA_CHUNK - 1:GLA_CHUNK]
        q = q_ref[rows, :].astype(F32) * scale
        k = k_ref[rows, :].astype(F32)
        q_intra = (q * jnp.exp(b - b_mid)).astype(BF16)
        k_intra = (k * jnp.exp(b_mid - b)).astype(BF16)
        q_inter = (q * jnp.exp(b)).astype(BF16)
        k_state = (k * jnp.exp(b_last - b)).astype(BF16)
        decay = jnp.exp(b_last)
        for h in range(GLA_HEADS):
            ks = slice(h * GLA_DK, (h + 1) * GLA_DK)
            vs = slice(h * GLA_DV, (h + 1) * GLA_DV)
            v = v_ref[rows, vs]
            a = lax.dot_general(q_intra[:, ks], k_intra[:, ks], NT_DIMS, preferred_element_type=F32)
            a = jnp.where(causal, a, 0.0).astype(BF16)
            o = jnp.dot(a, v, preferred_element_type=F32)
            st = st_ref[h]
            o = o + lax.dot_general(q_inter[:, ks], st.astype(BF16), NT_DIMS, preferred_element_type=F32)
            u_t = lax.dot_general(v, k_state[:, ks], TN_DIMS, preferred_element_type=F32)
            st_ref[h] = decay[:, ks] * st + u_t
            ms = jnp.mean(o * o, axis=-1, keepdims=True)
            y = o * lax.rsqrt(ms + EPS) * ng_ref[...]
            r = r_ref[rows, vs].astype(F32)
            o_ref[rows, vs] = (y * (r * _sigmoid(r))).astype(BF16)


def _gla(proj, small, w2_pad, b_gate, norm_g):
    t = proj.shape[0]
    q_blk = (3 * FOX_W) // GLA_DK_TOT
    v_blk = (3 * FOX_W + 2 * GLA_DK_TOT) // GLA_DV_TOT
    return pl.pallas_call(
        _gla_kernel,
        grid=(t // GLA_ROWS,),
        in_specs=[
            pl.BlockSpec((GLA_ROWS, GLA_DK_TOT), lambda i: (i, q_blk)),
            pl.BlockSpec((GLA_ROWS, GLA_DK_TOT), lambda i: (i, q_blk + 1)),
            pl.BlockSpec((GLA_ROWS, GLA_DV_TOT), lambda i: (i, v_blk)),
            pl.BlockSpec((GLA_ROWS, GLA_DV_TOT), lambda i: (i, v_blk + 1)),
            pl.BlockSpec((GLA_ROWS, LANES), lambda i: (i, 0)),
            pl.BlockSpec((LANES, GLA_DK_TOT), lambda i: (0, 0)),
            pl.BlockSpec((1, GLA_DK_TOT), lambda i: (0, 0)),
            pl.BlockSpec((1, GLA_DV), lambda i: (0, 0)),
        ],
        out_specs=pl.BlockSpec((GLA_ROWS, GLA_DV_TOT), lambda i: (jnp.maximum(i - 1, 0), 0)),
        out_shape=jax.ShapeDtypeStruct((t - HEAD_ROWS, GLA_DV_TOT), BF16),
        scratch_shapes=[pltpu.VMEM((GLA_HEADS, GLA_DV, GLA_DK), F32)],
        compiler_params=_params(("arbitrary",), 32),
    )(proj, proj, proj, proj, small, w2_pad, b_gate, norm_g)


def _fox_kernel(q_ref, k_ref, v_ref, negc_ref, ng_ref, o_ref, qs_ref, va_ref):
    qi = pl.program_id(1)
    n_blocks = FOX_TILE // FOX_ROWS

    @pl.when(qi == 0)
    def _():
        lane = lax.broadcasted_iota(jnp.int32, (v_ref.shape[0], FOX_HD), 1)
        ones_col = jnp.where(lane == 0, 1.0, 0.0).astype(BF16)
        for hh in range(FOX_GROUP):
            va_ref[:, 2 * hh * FOX_HD:(2 * hh + 1) * FOX_HD] = v_ref[:, hh * FOX_HD:(hh + 1) * FOX_HD]
            va_ref[:, (2 * hh + 1) * FOX_HD:(2 * hh + 2) * FOX_HD] = ones_col

    units = [(hh, rb) for hh in range(FOX_GROUP) for rb in range(n_blocks)]
    q0 = pl.multiple_of(HEAD_ROWS + qi * FOX_TILE, FOX_ROWS)
    qs_ref[...] = (q_ref[pl.ds(q0, FOX_TILE), :].astype(F32) * (FOX_HD ** -0.5 * LOG2E)).astype(BF16)
    row = lax.broadcasted_iota(jnp.int32, (FOX_ROWS, FOX_ROWS), 0)
    col = lax.broadcasted_iota(jnp.int32, (FOX_ROWS, FOX_ROWS), 1)

    def run(state, steps):
        def scores(step):
            u, off, k0, k1, causal_tail = step
            hh, rb = units[u]
            rows = slice(rb * FOX_ROWS, (rb + 1) * FOX_ROWS)
            cols = slice(hh * FOX_HD, (hh + 1) * FOX_HD)
            s = lax.dot_general(qs_ref[rows, cols], k_ref[pl.ds(off + k0, k1 - k0), cols], NT_DIMS,
                                preferred_element_type=F32)
            s = s + negc_ref[hh, :, pl.ds(off + k0, k1 - k0)]
            if causal_tail:
                tail = jnp.where(col <= row, s[:, k1 - k0 - FOX_ROWS:], MASK_VALUE)
                s = tail if k1 - k0 == FOX_ROWS else jnp.concatenate([s[:, :k1 - k0 - FOX_ROWS], tail], axis=1)
            return s

        def update(step, s, state):
            u, off, k0, k1, _ = step
            hh, _ = units[u]
            m_prev, acc_prev = state[u]
            m_new = jnp.maximum(m_prev, jnp.max(s, axis=-1, keepdims=True))
            p = jnp.exp2(s - m_new).astype(BF16)
            acc_new = jnp.exp2(m_prev - m_new) * acc_prev + jnp.dot(
                p, va_ref[pl.ds(off + k0, k1 - k0), 2 * hh * FOX_HD:(2 * hh + 2) * FOX_HD],
                preferred_element_type=F32)
            state[u] = (m_new, acc_new)

        state = list(state)
        pending = [scores(st) for st in steps[:FOX_SKEW]]
        for j, st in enumerate(steps):
            if j + FOX_SKEW < len(steps):
                pending.append(scores(steps[j + FOX_SKEW]))
            update(st, pending[j], state)
            pending[j] = None
        return tuple(state)

    head_steps = [(u, 0, 0, HEAD_ROWS, False) for u in range(len(units))]

    def full_steps(off):
        return [(u, off, k0, k0 + FOX_KEYS, False)
                for k0 in range(0, FOX_TILE, FOX_KEYS) for u in range(len(units))]

    diag_steps = []
    for k0 in range(0, FOX_TILE, FOX_KEYS):
        for u, (_, rb) in enumerate(units):
            last = (rb + 1) * FOX_ROWS
            if last > k0:
                diag_steps.append((u, q0, k0, min(k0 + FOX_KEYS, last), last <= k0 + FOX_KEYS))

    state = tuple((jnp.full((FOX_ROWS, 1), MASK_VALUE, F32), jnp.zeros((FOX_ROWS, 2 * FOX_HD), F32))
                  for _ in units)
    state = run(state, head_steps)
    state = lax.fori_loop(
        0, qi, lambda kt, st: run(st, full_steps(pl.multiple_of(HEAD_ROWS + kt * FOX_TILE, FOX_ROWS))), state)
    state = run(state, diag_steps)
    for u, (hh, rb) in enumerate(units):
        _, acc = state[u]
        o = acc[:, :FOX_HD] / acc[:, FOX_HD:FOX_HD + 1]
        ms = jnp.mean(o * o, axis=-1, keepdims=True)
        o_ref[rb * FOX_ROWS:(rb + 1) * FOX_ROWS, hh * FOX_HD:(hh + 1) * FOX_HD] = (
            o * lax.rsqrt(ms + EPS) * ng_ref[...]).astype(BF16)


def _fox(proj, negc3, norm_g):
    t = proj.shape[0]
    width = FOX_GROUP * FOX_HD
    k_blk = FOX_W // width
    return pl.pallas_call(
        _fox_kernel,
        grid=(FOX_HEADS // FOX_GROUP, (t - HEAD_ROWS) // FOX_TILE),
        in_specs=[
            pl.BlockSpec((t, width), lambda g, i: (0, g)),
            pl.BlockSpec((t, width), lambda g, i: (0, k_blk + g)),
            pl.BlockSpec((t, width), lambda g, i: (0, 2 * k_blk + g)),
            pl.BlockSpec((FOX_GROUP, 1, t), lambda g, i: (g, 0, 0)),
            pl.BlockSpec((1, FOX_HD), lambda g, i: (0, 0)),
        ],
        out_specs=pl.BlockSpec((FOX_TILE, width), lambda g, i: (i, g)),
        out_shape=jax.ShapeDtypeStruct((t - HEAD_ROWS, FOX_W), BF16),
        scratch_shapes=[pltpu.VMEM((FOX_TILE, width), BF16), pltpu.VMEM((t, 2 * width), BF16)],
        compiler_params=_params(("arbitrary", "arbitrary"), 56),
    )(proj, proj, proj, negc3, norm_g)


ROUTE_E1, ROUTE_E2, ROUTE_R1, ROUTE_R2, ROUTE_W1, ROUTE_W2 = range(6)


def _out_proj_kernel(og_ref, of_ref, h_ref, wo_ref, g2_ref, wr_ref, br_ref,
                     h1_ref, u2p_ref, route_ref, cnt_ref, tri_ref, run_ref):
    i = pl.program_id(0)

    @pl.when(i == 0)
    def _():
        r = lax.broadcasted_iota(jnp.int32, (ROW_TILE, ROW_TILE), 0)
        c = lax.broadcasted_iota(jnp.int32, (ROW_TILE, ROW_TILE), 1)
        tri_ref[...] = jnp.where(c < r, 1.0, 0.0).astype(BF16)
        run_ref[...] = jnp.zeros_like(run_ref)

    h1 = (h_ref[...]
          + jnp.dot(og_ref[...], wo_ref[0:GLA_DV_TOT, :], preferred_element_type=F32)
          + jnp.dot(of_ref[...], wo_ref[GLA_DV_TOT:, :], preferred_element_type=F32))
    h1_ref[...] = h1
    ms = jnp.mean(h1 * h1, axis=-1, keepdims=True)
    u2 = (h1 * lax.rsqrt(ms + EPS) * g2_ref[...]).astype(BF16)
    hi = pltpu.bitcast(u2[:, :HALF].astype(F32), jnp.uint32)
    lo = pltpu.bitcast(u2[:, HALF:].astype(F32), jnp.uint32)
    _rows_store(u2p_ref, hi | (lo >> 16), ROW_SUBLANES)

    logits = jnp.dot(u2, wr_ref[...], preferred_element_type=F32) + br_ref[...]
    lane = lax.broadcasted_iota(jnp.int32, logits.shape, 1).astype(F32)
    ninf = -jnp.inf

    def first_max(vals):
        top = jnp.max(vals, axis=-1, keepdims=True)
        idx = jnp.min(jnp.where(vals == top, lane, float(LANES)), axis=-1, keepdims=True)
        return top, idx

    gl = jnp.where(lane < N_GROUPS, logits, ninf)
    g_top, g_idx = first_max(gl)
    p_g = 1.0 / jnp.sum(jnp.exp(gl - g_top), axis=-1, keepdims=True)
    e_lo = N_GROUPS + EXPERTS_PER_GROUP * g_idx
    el = jnp.where((lane >= e_lo) & (lane < e_lo + EXPERTS_PER_GROUP), logits, ninf)
    top1, i1 = first_max(el)
    top2, i2 = first_max(jnp.where(lane == i1, ninf, el))
    ratio = jnp.exp(top2 - top1)
    w1 = 1.0 / (1.0 + ratio)
    w2 = ratio * w1

    is1 = lane == i1
    is2 = lane == i2
    onehot = jnp.where(is1 | is2, 1.0, 0.0)
    before = jnp.dot(tri_ref[...], onehot.astype(BF16), preferred_element_type=F32) + run_ref[...]
    r1 = jnp.sum(jnp.where(is1, before, 0.0), axis=-1, keepdims=True)
    r2 = jnp.sum(jnp.where(is2, before, 0.0), axis=-1, keepdims=True)
    run_ref[...] = run_ref[...] + jnp.sum(onehot, axis=0, keepdims=True)
    cnt_ref[...] = run_ref[...]

    rec = jnp.zeros_like(logits)
    for slot, val in ((ROUTE_E1, i1 - N_GROUPS), (ROUTE_E2, i2 - N_GROUPS), (ROUTE_R1, r1), (ROUTE_R2, r2),
                      (ROUTE_W1, p_g * w1), (ROUTE_W2, p_g * w2)):
        rec = jnp.where(lane == slot, val, rec)
    route_ref[...] = rec


def _out_proj(o_gla, o_fox, h0, w_out, g2, w_router, b_router):
    t = h0.shape[0]
    row = lambda i: (i, 0)
    fixed = lambda i: (0, 0)
    return pl.pallas_call(
        _out_proj_kernel,
        grid=(t // ROW_TILE,),
        in_specs=[
            pl.BlockSpec((ROW_TILE, GLA_DV_TOT), row),
            pl.BlockSpec((ROW_TILE, FOX_W), row),
            pl.BlockSpec((ROW_TILE, D_MODEL), row),
            pl.BlockSpec((D_MODEL, D_MODEL), fixed, pipeline_mode=pl.Buffered(1)),
            pl.BlockSpec((1, D_MODEL), fixed),
            pl.BlockSpec((D_MODEL, LANES), fixed),
            pl.BlockSpec((1, LANES), fixed),
        ],
        out_specs=[
            pl.BlockSpec((ROW_TILE, D_MODEL), row),
            pl.BlockSpec((ROW_TILE * ROW_SUBLANES, LANES), row),
            pl.BlockSpec((ROW_TILE, LANES), row),
            pl.BlockSpec((1, LANES), fixed),
        ],
        out_shape=[
            jax.ShapeDtypeStruct((t, D_MODEL), F32),
            jax.ShapeDtypeStruct((t * ROW_SUBLANES, LANES), jnp.uint32),
            jax.ShapeDtypeStruct((t, LANES), F32),
            jax.ShapeDtypeStruct((1, LANES), F32),
        ],
        scratch_shapes=[pltpu.VMEM((ROW_TILE, ROW_TILE), BF16), pltpu.VMEM((1, LANES), F32)],
        compiler_params=_params(("arbitrary",), 48),
    )(o_gla, o_fox, h0, w_out, g2, w_router, b_router)


def _scatter_kernel(pos1_ref, pos2_ref, last_ref, has_ref, nv_ref, u2p_ref, xs_ref, zero_ref, sem, zsem):
    i = pl.program_id(0)
    tile_rows = EXPERT_TILE * ROW_SUBLANES
    n_tiles = xs_ref.shape[0] // tile_rows

    def zero_copy(start):
        start = pl.multiple_of(start * ROW_SUBLANES, tile_rows)
        return pltpu.make_async_copy(zero_ref, xs_ref.at[pl.ds(start, tile_rows), :], zsem)

    def for_each_zero_tile(action):
        for e in range(N_EXPERTS):
            @pl.when(has_ref[e] > 0)
            def _():
                action(zero_copy(last_ref[e]))

        def unused_tile(j, carry):
            action(zero_copy(j * EXPERT_TILE))
            return carry

        lax.fori_loop(nv_ref[0], n_tiles, unused_tile, 0)

    @pl.when(i == 0)
    def _():
        zero_ref[...] = jnp.zeros_like(zero_ref)
        for_each_zero_tile(lambda cp: cp.start())
        for_each_zero_tile(lambda cp: cp.wait())

    base = i * ROW_TILE

    def row_copy(r, pos_ref):
        src = pl.multiple_of(r * ROW_SUBLANES, ROW_SUBLANES)
        dst = pl.multiple_of(pos_ref[base + r] * ROW_SUBLANES, ROW_SUBLANES)
        return pltpu.make_async_copy(u2p_ref.at[pl.ds(src, ROW_SUBLANES), :],
                                     xs_ref.at[pl.ds(dst, ROW_SUBLANES), :], sem)

    def for_each_row(action):
        def group(j, carry):
            for k in range(DMA_UNROLL):
                action(row_copy(j * DMA_UNROLL + k, pos1_ref))
                action(row_copy(j * DMA_UNROLL + k, pos2_ref))
            return carry

        lax.fori_loop(0, ROW_TILE // DMA_UNROLL, group, 0)

    for_each_row(lambda cp: cp.start())
    for_each_row(lambda cp: cp.wait())


def _scatter(pos1, pos2, last_row, has, n_valid, u2p, n_rows):
    t = u2p.shape[0] // ROW_SUBLANES
    return pl.pallas_call(
        _scatter_kernel,
        grid_spec=pltpu.PrefetchScalarGridSpec(
            num_scalar_prefetch=5,
            grid=(t // ROW_TILE,),
            in_specs=[pl.BlockSpec((ROW_TILE * ROW_SUBLANES, LANES), lambda i, *_: (i, 0))],
            out_specs=pl.BlockSpec(memory_space=pl.ANY),
            scratch_shapes=[pltpu.VMEM((EXPERT_TILE * ROW_SUBLANES, LANES), jnp.uint32),
                            pltpu.SemaphoreType.DMA, pltpu.SemaphoreType.DMA],
        ),
        out_shape=jax.ShapeDtypeStruct((n_rows * ROW_SUBLANES, LANES), jnp.uint32),
        compiler_params=_params(("arbitrary",), 32),
    )(pos1, pos2, last_row, has, n_valid, u2p)


def _moe_kernel(te_ref, nv_ref, xs_ref, wg_ref, wu_ref, wd_ref, y_ref, wgb_ref, wub_ref, wdb_ref):
    i = pl.program_id(0)

    @pl.when(i >= nv_ref[0])
    def _():
        y_ref[...] = jnp.zeros_like(y_ref)

    @pl.when(i < nv_ref[0])
    def _():
        prev = te_ref[jnp.maximum(i - 1, 0)]

        @pl.when((i == 0) | (te_ref[i] != prev))
        def _():
            wgb_ref[...] = wg_ref[...].astype(BF16)
            wub_ref[...] = wu_ref[...].astype(BF16)
            wdb_ref[...] = wd_ref[...].astype(BF16)

        xw = _rows_load(xs_ref, EXPERT_TILE, ROW_SUBLANES)
        xa = pltpu.bitcast(xw & jnp.uint32(0xFFFF0000), F32).astype(BF16)
        xb = pltpu.bitcast(xw << 16, F32).astype(BF16)
        hg = (jnp.dot(xa, wgb_ref[0:HALF, :], preferred_element_type=F32)
              + jnp.dot(xb, wgb_ref[HALF:, :], preferred_element_type=F32))
        hu = (jnp.dot(xa, wub_ref[0:HALF, :], preferred_element_type=F32)
              + jnp.dot(xb, wub_ref[HALF:, :], preferred_element_type=F32))
        hm = (hg * _sigmoid(hg) * hu).astype(BF16)
        y = jnp.dot(hm, wdb_ref[...], preferred_element_type=F32)
        _rows_store(y_ref, y, OUT_SUBLANES)


def _moe(tile_expert, n_valid, xs, w_g, w_u, w_d):
    n_rows = xs.shape[0] // ROW_SUBLANES
    w_in_spec = pl.BlockSpec((None, D_MODEL, D_EXPERT), lambda i, te, nv: (te[i], 0, 0))
    return pl.pallas_call(
        _moe_kernel,
        grid_spec=pltpu.PrefetchScalarGridSpec(
            num_scalar_prefetch=2,
            grid=(n_rows // EXPERT_TILE,),
            in_specs=[
                pl.BlockSpec((EXPERT_TILE * ROW_SUBLANES, LANES), lambda i, te, nv: (i, 0)),
                w_in_spec,
                w_in_spec,
                pl.BlockSpec((None, D_EXPERT, D_MODEL), lambda i, te, nv: (te[i], 0, 0)),
            ],
            out_specs=pl.BlockSpec((EXPERT_TILE * OUT_SUBLANES, LANES), lambda i, te, nv: (i, 0)),
            scratch_shapes=[pltpu.VMEM((D_MODEL, D_EXPERT), BF16), pltpu.VMEM((D_MODEL, D_EXPERT), BF16),
                            pltpu.VMEM((D_EXPERT, D_MODEL), BF16)],
        ),
        out_shape=jax.ShapeDtypeStruct((n_rows * OUT_SUBLANES, LANES), F32),
        compiler_params=_params(("arbitrary",), 52),
    )(tile_expert, n_valid, xs, w_g, w_u, w_d)


def _combine_kernel(pos1_ref, pos2_ref, h1_ref, route_ref, gf_ref, y_ref, out_ref, ya_ref, yb_ref, sems):
    i = pl.program_id(0)
    n_steps = pl.num_programs(0)

    def row_copy(tile, r, pos_ref, buf_ref):
        slot = tile % 2
        src = pl.multiple_of(pos_ref[tile * OUT_TILE + r] * OUT_SUBLANES, OUT_SUBLANES)
        dst = pl.multiple_of(r * OUT_SUBLANES, OUT_SUBLANES)
        return pltpu.make_async_copy(y_ref.at[pl.ds(src, OUT_SUBLANES), :],
                                     buf_ref.at[slot, pl.ds(dst, OUT_SUBLANES), :], sems.at[slot])

    def for_each_row(tile, action):
        def group(j, carry):
            for k in range(DMA_UNROLL):
                action(row_copy(tile, j * DMA_UNROLL + k, pos1_ref, ya_ref))
                action(row_copy(tile, j * DMA_UNROLL + k, pos2_ref, yb_ref))
            return carry

        lax.fori_loop(0, OUT_TILE // DMA_UNROLL, group, 0)

    @pl.when(i == 0)
    def _():
        for_each_row(i, lambda cp: cp.start())

    @pl.when(i + 1 < n_steps)
    def _():
        for_each_row(i + 1, lambda cp: cp.start())

    for_each_row(i, lambda cp: cp.wait())
    slot = i % 2
    rec = route_ref[...]
    ya = _rows_load(ya_ref.at[slot], OUT_TILE, OUT_SUBLANES)
    yb = _rows_load(yb_ref.at[slot], OUT_TILE, OUT_SUBLANES)
    hh = h1_ref[...] + rec[:, ROUTE_W1:ROUTE_W1 + 1] * ya + rec[:, ROUTE_W2:ROUTE_W2 + 1] * yb
    ms = jnp.mean(hh * hh, axis=-1, keepdims=True)
    out_ref[...] = hh * lax.rsqrt(ms + EPS) * gf_ref[...]


def _combine(pos1, pos2, h1, route, g_f, y):
    t = h1.shape[0]
    row = lambda i, *_: (i, 0)
    return pl.pallas_call(
        _combine_kernel,
        grid_spec=pltpu.PrefetchScalarGridSpec(
            num_scalar_prefetch=2,
            grid=(t // OUT_TILE,),
            in_specs=[
                pl.BlockSpec((OUT_TILE, D_MODEL), row),
                pl.BlockSpec((OUT_TILE, LANES), row),
                pl.BlockSpec((1, D_MODEL), lambda i, *_: (0, 0)),
                pl.BlockSpec(memory_space=pl.ANY),
            ],
            out_specs=pl.BlockSpec((OUT_TILE, D_MODEL), row),
            scratch_shapes=[pltpu.VMEM((2, OUT_TILE * OUT_SUBLANES, LANES), F32),
                            pltpu.VMEM((2, OUT_TILE * OUT_SUBLANES, LANES), F32),
                            pltpu.SemaphoreType.DMA((2,))],
        ),
        out_shape=jax.ShapeDtypeStruct((t, D_MODEL), F32),
        compiler_params=_params(("arbitrary",), 40),
    )(pos1, pos2, h1, route, g_f, y)


def _routing_tables(route, cnt, n_tiles):
    counts = cnt[0, N_GROUPS:N_GROUPS + N_EXPERTS].astype(jnp.int32)
    tiles = (counts + EXPERT_TILE - 1) // EXPERT_TILE
    tile_end = jnp.cumsum(tiles)
    row_start = (tile_end - tiles) * EXPERT_TILE
    n_valid = tile_end[-1]
    expert_ids = jnp.arange(N_EXPERTS, dtype=jnp.int32)

    def positions(expert_lane, rank_lane):
        expert = route[:, expert_lane].astype(jnp.int32)
        start = jnp.sum(jnp.where(expert[:, None] == expert_ids[None, :], row_start[None, :], 0), axis=1)
        return start + route[:, rank_lane].astype(jnp.int32)

    pos1 = positions(ROUTE_E1, ROUTE_R1)
    pos2 = positions(ROUTE_E2, ROUTE_R2)
    tile = jnp.minimum(jnp.arange(n_tiles, dtype=jnp.int32), n_valid - 1)
    tile_expert = jnp.sum(tile[:, None] >= tile_end[None, :], axis=-1).astype(jnp.int32)
    last_row = (tile_end - 1) * EXPERT_TILE
    return pos1, pos2, tile_expert, n_valid.reshape(1), last_row, tiles


def kernel(x, meta_tokens, norm1_g, w_in, b_fox_f, gla_w_gate2, gla_b_gate, gla_norm_g, fox_norm_g, w_out,
           norm2_g, w_router_group, b_router_group, w_router_expert, b_router_expert, w_exp_gate, w_exp_up,
           w_exp_down, norm_f_g):
    batch, seq, _ = x.shape
    assert batch == 1 and norm1_g.shape[0] == 1
    assert seq % FOX_TILE == 0 and seq % ROW_TILE == 0
    t = HEAD_ROWS + seq
    x2 = x[0]
    head = jnp.concatenate([jnp.zeros((PAD_FRONT, D_MODEL), F32), meta_tokens.astype(F32)], axis=0)

    w = w_in[0]
    o_ff = 3 * FOX_W
    o_gq = o_ff + FOX_HEADS
    o_gz = o_gq + 2 * GLA_DK_TOT + 2 * GLA_DV_TOT
    w_big = jnp.concatenate([w[:, :o_ff].astype(BF16), w[:, o_gq:o_gz].astype(BF16)], axis=1)
    w_small = jnp.concatenate([w[:, o_ff:o_gq], w[:, o_gz:],
                               jnp.zeros((D_MODEL, LANES - FOX_HEADS - GLA_RANK), F32)], axis=1).astype(BF16)
    proj, small = _in_proj(head, x2, norm1_g, w_big, w_small)

    negc = _fox_bias(small, b_fox_f[0].reshape(FOX_HEADS, 1))
    w2_pad = jnp.zeros((LANES, GLA_DK_TOT), F32).at[FOX_HEADS:FOX_HEADS + GLA_RANK].set(gla_w_gate2[0])
    o_gla = _gla(proj, small, w2_pad, gla_b_gate, gla_norm_g)
    o_fox = _fox(proj, negc.reshape(FOX_HEADS, 1, t), fox_norm_g)

    w_router = jnp.concatenate(
        [w_router_group[0], jnp.transpose(w_router_expert[0], (1, 0, 2)).reshape(D_MODEL, N_EXPERTS),
         jnp.zeros((D_MODEL, LANES - N_GROUPS - N_EXPERTS), F32)], axis=1).astype(BF16)
    b_router = jnp.concatenate([b_router_group[0], b_router_expert[0].reshape(-1),
                                jnp.zeros((LANES - N_GROUPS - N_EXPERTS,), F32)]).reshape(1, LANES)
    h1, u2p, route, cnt = _out_proj(o_gla, o_fox, x2, w_out[0].astype(BF16), norm2_g, w_router, b_router)

    n_tiles = (2 * seq) // EXPERT_TILE + N_EXPERTS
    pos1, pos2, tile_expert, n_valid, last_row, tiles = _routing_tables(route, cnt, n_tiles)
    xs = _scatter(pos1, pos2, last_row, tiles, n_valid, u2p, n_tiles * EXPERT_TILE)
    y = _moe(tile_expert, n_valid, xs,
             w_exp_gate[0].reshape(N_EXPERTS, D_MODEL, D_EXPERT),
             w_exp_up[0].reshape(N_EXPERTS, D_MODEL, D_EXPERT),
             w_exp_down[0].reshape(N_EXPERTS, D_EXPERT, D_MODEL))
    out = _combine(pos1, pos2, h1, route, norm_f_g.reshape(1, D_MODEL), y)
    return out.reshape(1, seq, D_MODEL)
```

```python
import jax
import jax.numpy as jnp
from jax import lax
from jax.experimental import pallas as pl
from jax.experimental.pallas import tpu as pltpu

D_MODEL = 2048
N_META = 16
GLA_HEADS = 4
GLA_DK = 128
GLA_DV = 256
GLA_DK_TOT = GLA_HEADS * GLA_DK
GLA_DV_TOT = GLA_HEADS * GLA_DV
GLA_RANK = 16
GLA_TAU = 16.0
GLA_CHUNK = 64
FOX_HEADS = 8
FOX_HD = 128
FOX_W = FOX_HEADS * FOX_HD
FOX_BLOCK = 128
PAD_FRONT = FOX_BLOCK - N_META
HEAD_ROWS = PAD_FRONT + N_META
N_GROUPS = 4
EXPERTS_PER_GROUP = 8
N_EXPERTS = N_GROUPS * EXPERTS_PER_GROUP
D_EXPERT = 512
EPS = 1e-6

LANES = 128
PROJ_ROWS = HEAD_ROWS
GLA_ROWS = 2 * GLA_CHUNK
FOX_TILE = 1024
FOX_ROWS = 128
FOX_KEYS = 1024
FOX_GROUP = 2
FOX_SKEW = 3
LOG2E = 1.4426950408889634
ROW_TILE = 512
EXPERT_TILE = 256
OUT_TILE = 256
MASK_VALUE = -1e30
PROJ_BIG = 3 * FOX_W + 2 * GLA_DK_TOT + 2 * GLA_DV_TOT
PROJ_FF = 3 * FOX_W
PROJ_GQ = PROJ_FF + FOX_HEADS
PROJ_GZ = PROJ_GQ + 2 * GLA_DK_TOT + 2 * GLA_DV_TOT
D_IN_PROJ = PROJ_GZ + GLA_RANK
PROJ_ALIGNED = (D_IN_PROJ // LANES) * LANES
PROJ_STAGE_COLS = 512
ROW_SUBLANES = D_MODEL // LANES
DMA_UNROLL = 8

F32 = jnp.float32
BF16 = jnp.bfloat16
NT_DIMS = (((1,), (1,)), ((), ()))
TN_DIMS = (((0,), (0,)), ((), ()))
HIGHEST = lax.Precision.HIGHEST


def _log_sigmoid(x):
    return jnp.minimum(x, 0.0) - jnp.log(1.0 + jnp.exp(-jnp.abs(x)))


def _sigmoid(x):
    return 1.0 / (1.0 + jnp.exp(-x))


def _rows_load(ref, n_rows, n_chunks):
    return jnp.concatenate([ref[pl.ds(s, n_rows, stride=n_chunks), :] for s in range(n_chunks)], axis=1)


def _rows_store(ref, value, n_chunks):
    n_rows = value.shape[0]
    for s in range(n_chunks):
        ref[pl.ds(s, n_rows, stride=n_chunks), :] = value[:, s * LANES:(s + 1) * LANES]


def _params(semantics, vmem_mb):
    return pltpu.CompilerParams(dimension_semantics=semantics, vmem_limit_bytes=vmem_mb * 1024 * 1024)


def _in_proj_kernel(head_ref, x_ref, g_ref, wt_hbm, proj_ref, small_ref, w_ref, stage_ref, tail_ref, sems):
    n_chunks = PROJ_ALIGNED // PROJ_STAGE_COLS
    n_tail = D_IN_PROJ - PROJ_ALIGNED

    def chunk_copy(c):
        return pltpu.make_async_copy(wt_hbm.at[pl.ds(c * PROJ_STAGE_COLS, PROJ_STAGE_COLS), :],
                                     stage_ref.at[c % 2], sems.at[c % 2])

    def tail_copy():
        return pltpu.make_async_copy(wt_hbm.at[pl.ds(PROJ_ALIGNED, n_tail), :],
                                     tail_ref.at[pl.ds(0, n_tail), :], sems.at[2])

    @pl.when(pl.program_id(0) == 0)
    def _():
        tail_ref[...] = jnp.zeros_like(tail_ref)
        tail_copy().start()
        chunk_copy(0).start()
        for c in range(n_chunks):
            if c + 1 < n_chunks:
                chunk_copy(c + 1).start()
            chunk_copy(c).wait()
            w_ref[:, c * PROJ_STAGE_COLS:(c + 1) * PROJ_STAGE_COLS] = stage_ref[c % 2].T.astype(BF16)
        tail_copy().wait()
        w_ref[:, PROJ_ALIGNED:] = tail_ref[...].T.astype(BF16)

    x = jnp.where(pl.program_id(0) == 0, head_ref[...], x_ref[...])
    ms = jnp.mean(x * x, axis=-1, keepdims=True)
    xn = (x * lax.rsqrt(ms + EPS) * g_ref[...]).astype(BF16)
    p = jnp.dot(xn, w_ref[...], preferred_element_type=F32)
    lane = lax.broadcasted_iota(jnp.int32, (PROJ_ROWS, LANES), 1)
    small_ref[...] = jnp.where(lane < FOX_HEADS, p[:, PROJ_FF:PROJ_FF + LANES], p[:, PROJ_ALIGNED:])
    proj_ref[...] = jnp.concatenate([p[:, :PROJ_FF], p[:, PROJ_GQ:PROJ_GZ]], axis=1).astype(BF16)


def _in_proj(head, x, g1, w_t):
    t = HEAD_ROWS + x.shape[0]
    fixed = lambda i: (0, 0)
    return pl.pallas_call(
        _in_proj_kernel,
        grid=(t // PROJ_ROWS,),
        in_specs=[
            pl.BlockSpec((PROJ_ROWS, D_MODEL), fixed),
            pl.BlockSpec((PROJ_ROWS, D_MODEL), lambda i: (jnp.maximum(i - 1, 0), 0)),
            pl.BlockSpec((1, D_MODEL), fixed),
            pl.BlockSpec(memory_space=pl.ANY),
        ],
        out_specs=[
            pl.BlockSpec((PROJ_ROWS, PROJ_BIG), lambda i: (i, 0)),
            pl.BlockSpec((PROJ_ROWS, LANES), lambda i: (i, 0)),
        ],
        out_shape=[
            jax.ShapeDtypeStruct((t, PROJ_BIG), BF16),
            jax.ShapeDtypeStruct((t, LANES), F32),
        ],
        scratch_shapes=[pltpu.VMEM((D_MODEL, PROJ_ALIGNED + LANES), BF16),
                        pltpu.VMEM((2, PROJ_STAGE_COLS, D_MODEL), F32),
                        pltpu.VMEM((LANES, D_MODEL), F32),
                        pltpu.SemaphoreType.DMA((3,))],
        compiler_params=_params(("arbitrary",), 56),
    )(head, x, g1, w_t)


def _fox_bias_kernel(small_ref, bf_ref, negc_ref):
    t = small_ref.shape[0]
    r = lax.broadcasted_iota(jnp.int32, (LANES, LANES), 0)
    c = lax.broadcasted_iota(jnp.int32, (LANES, LANES), 1)
    upper = (r <= c).astype(F32)
    lane = lax.broadcasted_iota(jnp.int32, (FOX_HEADS, LANES), 1)

    def body(b, carry):
        off = pl.multiple_of(b * LANES, LANES)
        valid = (off + lane) >= PAD_FRONT
        f_logit = small_ref[pl.ds(off, LANES), :].T[0:FOX_HEADS, :]
        lf = jnp.where(valid, _log_sigmoid(f_logit + bf_ref[...]), 0.0)
        cum = jnp.dot(lf, upper, precision=HIGHEST, preferred_element_type=F32) + carry
        negc_ref[:, pl.ds(off, LANES)] = jnp.where(valid, -LOG2E * cum, MASK_VALUE)
        return cum[:, LANES - 1:LANES]

    lax.fori_loop(0, t // LANES, body, jnp.zeros((FOX_HEADS, 1), F32))


def _fox_bias(small, b_f):
    return pl.pallas_call(
        _fox_bias_kernel,
        out_shape=jax.ShapeDtypeStruct((FOX_HEADS, small.shape[0]), F32),
    )(small, b_f)


def _gla_kernel(q_ref, k_ref, v_ref, r_ref, small_ref, w2_ref, bg_ref, ng_ref, o_ref, st_ref):
    i = pl.program_id(0)

    @pl.when(i == 0)
    def _():
        st_ref[...] = jnp.zeros_like(st_ref)

    gate_logit = jnp.dot(small_ref[...], w2_ref[...], precision=HIGHEST, preferred_element_type=F32) + bg_ref[...]
    g = _log_sigmoid(gate_logit) * (1.0 / GLA_TAU)
    rowid = i * GLA_ROWS + lax.broadcasted_iota(jnp.int32, (GLA_ROWS, 1), 0)
    g = jnp.where(rowid >= PAD_FRONT, g, 0.0)

    ci = lax.broadcasted_iota(jnp.int32, (GLA_CHUNK, GLA_CHUNK), 0)
    cj = lax.broadcasted_iota(jnp.int32, (GLA_CHUNK, GLA_CHUNK), 1)
    causal = cj <= ci
    lower = causal.astype(F32)
    scale = GLA_DK ** -0.5
    mid = GLA_CHUNK // 2

    for c in range(GLA_ROWS // GLA_CHUNK):
        rows = slice(c * GLA_CHUNK, (c + 1) * GLA_CHUNK)
        b = jnp.dot(lower, g[rows], precision=HIGHEST, preferred_element_type=F32)
        b_mid = b[mid:mid + 1]
        b_last = b[GLA_CHUNK - 1:GLA_CHUNK]
        q = q_ref[rows, :].astype(F32) * scale
        k = k_ref[rows, :].astype(F32)
        q_intra = (q * jnp.exp(b - b_mid)).astype(BF16)
        k_intra = (k * jnp.exp(b_mid - b)).astype(BF16)
        q_inter = (q * jnp.exp(b)).astype(BF16)
        k_state = (k * jnp.exp(b_last - b)).astype(BF16)
        decay = jnp.exp(b_last)
        for h in range(GLA_HEADS):
            ks = slice(h * GLA_DK, (h + 1) * GLA_DK)
            vs = slice(h * GLA_DV, (h + 1) * GLA_DV)
            v = v_ref[rows, vs]
            a = lax.dot_general(q_intra[:, ks], k_intra[:, ks], NT_DIMS, preferred_element_type=F32)
            a = jnp.where(causal, a, 0.0).astype(BF16)
            o = jnp.dot(a, v, preferred_element_type=F32)
            st = st_ref[h]
            o = o + lax.dot_general(q_inter[:, ks], st.astype(BF16), NT_DIMS, preferred_element_type=F32)
            u_t = lax.dot_general(v, k_state[:, ks], TN_DIMS, preferred_element_type=F32)
            st_ref[h] = decay[:, ks] * st + u_t
            ms = jnp.mean(o * o, axis=-1, keepdims=True)
            y = o * lax.rsqrt(ms + EPS) * ng_ref[...]
            r = r_ref[rows, vs].astype(F32)
            o_ref[rows, vs] = (y * (r * _sigmoid(r))).astype(BF16)


def _gla(proj, small, w2_pad, b_gate, norm_g):
    t = proj.shape[0]
    q_blk = (3 * FOX_W) // GLA_DK_TOT
    v_blk = (3 * FOX_W + 2 * GLA_DK_TOT) // GLA_DV_TOT
    return pl.pallas_call(
        _gla_kernel,
        grid=(t // GLA_ROWS,),
        in_specs=[
            pl.BlockSpec((GLA_ROWS, GLA_DK_TOT), lambda i: (i, q_blk)),
            pl.BlockSpec((GLA_ROWS, GLA_DK_TOT), lambda i: (i, q_blk + 1)),
            pl.BlockSpec((GLA_ROWS, GLA_DV_TOT), lambda i: (i, v_blk)),
            pl.BlockSpec((GLA_ROWS, GLA_DV_TOT), lambda i: (i, v_blk + 1)),
            pl.BlockSpec((GLA_ROWS, LANES), lambda i: (i, 0)),
            pl.BlockSpec((LANES, GLA_DK_TOT), lambda i: (0, 0)),
            pl.BlockSpec((1, GLA_DK_TOT), lambda i: (0, 0)),
            pl.BlockSpec((1, GLA_DV), lambda i: (0, 0)),
        ],
        out_specs=pl.BlockSpec((GLA_ROWS, GLA_DV_TOT), lambda i: (jnp.maximum(i - 1, 0), 0)),
        out_shape=jax.ShapeDtypeStruct((t - HEAD_ROWS, GLA_DV_TOT), BF16),
        scratch_shapes=[pltpu.VMEM((GLA_HEADS, GLA_DV, GLA_DK), F32)],
        compiler_params=_params(("arbitrary",), 32),
    )(proj, proj, proj, proj, small, w2_pad, b_gate, norm_g)


def _fox_kernel(q_ref, k_ref, v_ref, negc_ref, ng_ref, o_ref, qs_ref, va_ref):
    qi = pl.program_id(1)
    n_blocks = FOX_TILE // FOX_ROWS

    @pl.when(qi == 0)
    def _():
        lane = lax.broadcasted_iota(jnp.int32, (v_ref.shape[0], FOX_HD), 1)
        ones_col = jnp.where(lane == 0, 1.0, 0.0).astype(BF16)
        for hh in range(FOX_GROUP):
            va_ref[:, 2 * hh * FOX_HD:(2 * hh + 1) * FOX_HD] = v_ref[:, hh * FOX_HD:(hh + 1) * FOX_HD]
            va_ref[:, (2 * hh + 1) * FOX_HD:(2 * hh + 2) * FOX_HD] = ones_col

    units = [(hh, rb) for hh in range(FOX_GROUP) for rb in range(n_blocks)]
    q0 = pl.multiple_of(HEAD_ROWS + qi * FOX_TILE, FOX_ROWS)
    qs_ref[...] = (q_ref[pl.ds(q0, FOX_TILE), :].astype(F32) * (FOX_HD ** -0.5 * LOG2E)).astype(BF16)
    row = lax.broadcasted_iota(jnp.int32, (FOX_ROWS, FOX_ROWS), 0)
    col = lax.broadcasted_iota(jnp.int32, (FOX_ROWS, FOX_ROWS), 1)

    def run(state, steps):
        def scores(step):
            u, off, k0, k1, causal_tail = step
            hh, rb = units[u]
            rows = slice(rb * FOX_ROWS, (rb + 1) * FOX_ROWS)
            cols = slice(hh * FOX_HD, (hh + 1) * FOX_HD)
            s = lax.dot_general(qs_ref[rows, cols], k_ref[pl.ds(off + k0, k1 - k0), cols], NT_DIMS,
                                preferred_element_type=F32)
            s = s + negc_ref[hh, :, pl.ds(off + k0, k1 - k0)]
            if causal_tail:
                tail = jnp.where(col <= row, s[:, k1 - k0 - FOX_ROWS:], MASK_VALUE)
                s = tail if k1 - k0 == FOX_ROWS else jnp.concatenate([s[:, :k1 - k0 - FOX_ROWS], tail], axis=1)
            return s

        def update(step, s, state):
            u, off, k0, k1, _ = step
            hh, _ = units[u]
            m_prev, acc_prev = state[u]
            m_new = jnp.maximum(m_prev, jnp.max(s, axis=-1, keepdims=True))
            p = jnp.exp2(s - m_new).astype(BF16)
            acc_new = jnp.exp2(m_prev - m_new) * acc_prev + jnp.dot(
                p, va_ref[pl.ds(off + k0, k1 - k0), 2 * hh * FOX_HD:(2 * hh + 2) * FOX_HD],
                preferred_element_type=F32)
            state[u] = (m_new, acc_new)

        state = list(state)
        pending = [scores(st) for st in steps[:FOX_SKEW]]
        for j, st in enumerate(steps):
            if j + FOX_SKEW < len(steps):
                pending.append(scores(steps[j + FOX_SKEW]))
            update(st, pending[j], state)
            pending[j] = None
        return tuple(state)

    head_steps = [(u, 0, 0, HEAD_ROWS, False) for u in range(len(units))]

    def full_steps(off):
        return [(u, off, k0, k0 + FOX_KEYS, False)
                for k0 in range(0, FOX_TILE, FOX_KEYS) for u in range(len(units))]

    diag_steps = []
    for k0 in range(0, FOX_TILE, FOX_KEYS):
        for u, (_, rb) in enumerate(units):
            last = (rb + 1) * FOX_ROWS
            if last > k0:
                diag_steps.append((u, q0, k0, min(k0 + FOX_KEYS, last), last <= k0 + FOX_KEYS))

    state = tuple((jnp.full((FOX_ROWS, 1), MASK_VALUE, F32), jnp.zeros((FOX_ROWS, 2 * FOX_HD), F32))
                  for _ in units)
    state = run(state, head_steps)
    state = lax.fori_loop(
        0, qi, lambda kt, st: run(st, full_steps(pl.multiple_of(HEAD_ROWS + kt * FOX_TILE, FOX_ROWS))), state)
    state = run(state, diag_steps)
    for u, (hh, rb) in enumerate(units):
        _, acc = state[u]
        o = acc[:, :FOX_HD] / acc[:, FOX_HD:FOX_HD + 1]
        ms = jnp.mean(o * o, axis=-1, keepdims=True)
        o_ref[rb * FOX_ROWS:(rb + 1) * FOX_ROWS, hh * FOX_HD:(hh + 1) * FOX_HD] = (
            o * lax.rsqrt(ms + EPS) * ng_ref[...]).astype(BF16)


def _fox(proj, negc3, norm_g):
    t = proj.shape[0]
    width = FOX_GROUP * FOX_HD
    k_blk = FOX_W // width
    return pl.pallas_call(
        _fox_kernel,
        grid=(FOX_HEADS // FOX_GROUP, (t - HEAD_ROWS) // FOX_TILE),
        in_specs=[
            pl.BlockSpec((t, width), lambda g, i: (0, g)),
            pl.BlockSpec((t, width), lambda g, i: (0, k_blk + g)),
            pl.BlockSpec((t, width), lambda g, i: (0, 2 * k_blk + g)),
            pl.BlockSpec((FOX_GROUP, 1, t), lambda g, i: (g, 0, 0)),
            pl.BlockSpec((1, FOX_HD), lambda g, i: (0, 0)),
        ],
        out_specs=pl.BlockSpec((FOX_TILE, width), lambda g, i: (i, g)),
        out_shape=jax.ShapeDtypeStruct((t - HEAD_ROWS, FOX_W), BF16),
        scratch_shapes=[pltpu.VMEM((FOX_TILE, width), BF16), pltpu.VMEM((t, 2 * width), BF16)],
        compiler_params=_params(("arbitrary", "arbitrary"), 56),
    )(proj, proj, proj, negc3, norm_g)


ROUTE_E1, ROUTE_E2, ROUTE_R1, ROUTE_R2, ROUTE_W1, ROUTE_W2 = range(6)


def _out_proj_kernel(og_ref, of_ref, h_ref, wo_ref, g2_ref, wr_ref, br_ref,
                     h1_ref, u2p_ref, route_ref, cnt_ref, tri_ref, run_ref):
    i = pl.program_id(0)

    @pl.when(i == 0)
    def _():
        r = lax.broadcasted_iota(jnp.int32, (ROW_TILE, ROW_TILE), 0)
        c = lax.broadcasted_iota(jnp.int32, (ROW_TILE, ROW_TILE), 1)
        tri_ref[...] = jnp.where(c < r, 1.0, 0.0).astype(BF16)
        run_ref[...] = jnp.zeros_like(run_ref)

    h1 = (h_ref[...]
          + jnp.dot(og_ref[...], wo_ref[0:GLA_DV_TOT, :], preferred_element_type=F32)
          + jnp.dot(of_ref[...], wo_ref[GLA_DV_TOT:, :], preferred_element_type=F32))
    h1_ref[...] = h1
    ms = jnp.mean(h1 * h1, axis=-1, keepdims=True)
    u2 = h1 * lax.rsqrt(ms + EPS) * g2_ref[...]
    _rows_store(u2p_ref, u2, ROW_SUBLANES)

    logits = jnp.dot(u2.astype(BF16), wr_ref[...], preferred_element_type=F32) + br_ref[...]
    lane = lax.broadcasted_iota(jnp.int32, logits.shape, 1).astype(F32)
    ninf = -jnp.inf

    def first_max(vals):
        top = jnp.max(vals, axis=-1, keepdims=True)
        idx = jnp.min(jnp.where(vals == top, lane, float(LANES)), axis=-1, keepdims=True)
        return top, idx

    gl = jnp.where(lane < N_GROUPS, logits, ninf)
    g_top, g_idx = first_max(gl)
    p_g = 1.0 / jnp.sum(jnp.exp(gl - g_top), axis=-1, keepdims=True)
    e_lo = N_GROUPS + EXPERTS_PER_GROUP * g_idx
    el = jnp.where((lane >= e_lo) & (lane < e_lo + EXPERTS_PER_GROUP), logits, ninf)
    top1, i1 = first_max(el)
    top2, i2 = first_max(jnp.where(lane == i1, ninf, el))
    ratio = jnp.exp(top2 - top1)
    w1 = 1.0 / (1.0 + ratio)
    w2 = ratio * w1

    is1 = lane == i1
    is2 = lane == i2
    onehot = jnp.where(is1 | is2, 1.0, 0.0)
    before = jnp.dot(tri_ref[...], onehot.astype(BF16), preferred_element_type=F32) + run_ref[...]
    r1 = jnp.sum(jnp.where(is1, before, 0.0), axis=-1, keepdims=True)
    r2 = jnp.sum(jnp.where(is2, before, 0.0), axis=-1, keepdims=True)
    run_ref[...] = run_ref[...] + jnp.sum(onehot, axis=0, keepdims=True)
    cnt_ref[...] = run_ref[...]

    rec = jnp.zeros_like(logits)
    for slot, val in ((ROUTE_E1, i1 - N_GROUPS), (ROUTE_E2, i2 - N_GROUPS), (ROUTE_R1, r1), (ROUTE_R2, r2),
                      (ROUTE_W1, p_g * w1), (ROUTE_W2, p_g * w2)):
        rec = jnp.where(lane == slot, val, rec)
    route_ref[...] = rec


def _out_proj(o_gla, o_fox, h0, w_out, g2, w_router, b_router):
    t = h0.shape[0]
    row = lambda i: (i, 0)
    fixed = lambda i: (0, 0)
    return pl.pallas_call(
        _out_proj_kernel,
        grid=(t // ROW_TILE,),
        in_specs=[
            pl.BlockSpec((ROW_TILE, GLA_DV_TOT), row),
            pl.BlockSpec((ROW_TILE, FOX_W), row),
            pl.BlockSpec((ROW_TILE, D_MODEL), row),
            pl.BlockSpec((D_MODEL, D_MODEL), fixed, pipeline_mode=pl.Buffered(1)),
            pl.BlockSpec((1, D_MODEL), fixed),
            pl.BlockSpec((D_MODEL, LANES), fixed),
            pl.BlockSpec((1, LANES), fixed),
        ],
        out_specs=[
            pl.BlockSpec((ROW_TILE, D_MODEL), row),
            pl.BlockSpec((ROW_TILE * ROW_SUBLANES, LANES), row),
            pl.BlockSpec((ROW_TILE, LANES), row),
            pl.BlockSpec((1, LANES), fixed),
        ],
        out_shape=[
            jax.ShapeDtypeStruct((t, D_MODEL), F32),
            jax.ShapeDtypeStruct((t * ROW_SUBLANES, LANES), F32),
            jax.ShapeDtypeStruct((t, LANES), F32),
            jax.ShapeDtypeStruct((1, LANES), F32),
        ],
        scratch_shapes=[pltpu.VMEM((ROW_TILE, ROW_TILE), BF16), pltpu.VMEM((1, LANES), F32)],
        compiler_params=_params(("arbitrary",), 48),
    )(o_gla, o_fox, h0, w_out, g2, w_router, b_router)


def _scatter_kernel(pos1_ref, pos2_ref, last_ref, has_ref, nv_ref, u2p_ref, xs_ref, zero_ref, sem, zsem):
    i = pl.program_id(0)
    tile_rows = EXPERT_TILE * ROW_SUBLANES
    n_tiles = xs_ref.shape[0] // tile_rows

    def zero_copy(start):
        start = pl.multiple_of(start * ROW_SUBLANES, tile_rows)
        return pltpu.make_async_copy(zero_ref, xs_ref.at[pl.ds(start, tile_rows), :], zsem)

    def for_each_zero_tile(action):
        for e in range(N_EXPERTS):
            @pl.when(has_ref[e] > 0)
            def _():
                action(zero_copy(last_ref[e]))

        def unused_tile(j, carry):
            action(zero_copy(j * EXPERT_TILE))
            return carry

        lax.fori_loop(nv_ref[0], n_tiles, unused_tile, 0)

    @pl.when(i == 0)
    def _():
        zero_ref[...] = jnp.zeros_like(zero_ref)
        for_each_zero_tile(lambda cp: cp.start())
        for_each_zero_tile(lambda cp: cp.wait())

    base = i * ROW_TILE

    def row_copy(r, pos_ref):
        src = pl.multiple_of(r * ROW_SUBLANES, ROW_SUBLANES)
        dst = pl.multiple_of(pos_ref[base + r] * ROW_SUBLANES, ROW_SUBLANES)
        return pltpu.make_async_copy(u2p_ref.at[pl.ds(src, ROW_SUBLANES), :],
                                     xs_ref.at[pl.ds(dst, ROW_SUBLANES), :], sem)

    def for_each_row(action):
        def group(j, carry):
            for k in range(DMA_UNROLL):
                action(row_copy(j * DMA_UNROLL + k, pos1_ref))
                action(row_copy(j * DMA_UNROLL + k, pos2_ref))
            return carry

        lax.fori_loop(0, ROW_TILE // DMA_UNROLL, group, 0)

    for_each_row(lambda cp: cp.start())
    for_each_row(lambda cp: cp.wait())


def _scatter(pos1, pos2, last_row, has, n_valid, u2p, n_rows):
    t = u2p.shape[0] // ROW_SUBLANES
    return pl.pallas_call(
        _scatter_kernel,
        grid_spec=pltpu.PrefetchScalarGridSpec(
            num_scalar_prefetch=5,
            grid=(t // ROW_TILE,),
            in_specs=[pl.BlockSpec((ROW_TILE * ROW_SUBLANES, LANES), lambda i, *_: (i, 0))],
            out_specs=pl.BlockSpec(memory_space=pl.ANY),
            scratch_shapes=[pltpu.VMEM((EXPERT_TILE * ROW_SUBLANES, LANES), F32),
                            pltpu.SemaphoreType.DMA, pltpu.SemaphoreType.DMA],
        ),
        out_shape=jax.ShapeDtypeStruct((n_rows * ROW_SUBLANES, LANES), F32),
        compiler_params=_params(("arbitrary",), 32),
    )(pos1, pos2, last_row, has, n_valid, u2p)


def _moe_kernel(te_ref, first_ref, next_ref, nv_ref, xs_ref, wg_hbm, wu_hbm, wd_hbm, y_ref,
                wg_stage, wu_stage, wd_stage, wgb_ref, wub_ref, wdb_ref, sems):
    i = pl.program_id(0)

    def weight_copies(expert):
        return (pltpu.make_async_copy(wg_hbm.at[expert], wg_stage, sems.at[0]),
                pltpu.make_async_copy(wu_hbm.at[expert], wu_stage, sems.at[1]),
                pltpu.make_async_copy(wd_hbm.at[expert], wd_stage, sems.at[2]))

    @pl.when(i >= nv_ref[0])
    def _():
        y_ref[...] = jnp.zeros_like(y_ref)

    @pl.when(i < nv_ref[0])
    def _():
        @pl.when(first_ref[i] == 1)
        def _():
            @pl.when(i == 0)
            def _():
                for cp in weight_copies(te_ref[i]):
                    cp.start()

            for cp in weight_copies(te_ref[i]):
                cp.wait()
            wgb_ref[...] = wg_stage[...].astype(BF16)
            wub_ref[...] = wu_stage[...].astype(BF16)
            wdb_ref[...] = wd_stage[...].astype(BF16)

            @pl.when(next_ref[i] >= 0)
            def _():
                for cp in weight_copies(next_ref[i]):
                    cp.start()

        x = _rows_load(xs_ref, EXPERT_TILE, ROW_SUBLANES).astype(BF16)
        hg = jnp.dot(x, wgb_ref[...], preferred_element_type=F32)
        hu = jnp.dot(x, wub_ref[...], preferred_element_type=F32)
        hm = (hg * _sigmoid(hg) * hu).astype(BF16)
        y = jnp.dot(hm, wdb_ref[...], preferred_element_type=F32)
        _rows_store(y_ref, y, ROW_SUBLANES)


def _moe(tile_expert, tile_first, tile_next, n_valid, xs, w_g, w_u, w_d):
    n_rows = xs.shape[0] // ROW_SUBLANES
    row = lambda i, *_: (i, 0)
    return pl.pallas_call(
        _moe_kernel,
        grid_spec=pltpu.PrefetchScalarGridSpec(
            num_scalar_prefetch=4,
            grid=(n_rows // EXPERT_TILE,),
            in_specs=[
                pl.BlockSpec((EXPERT_TILE * ROW_SUBLANES, LANES), row),
                pl.BlockSpec(memory_space=pl.ANY),
                pl.BlockSpec(memory_space=pl.ANY),
                pl.BlockSpec(memory_space=pl.ANY),
            ],
            out_specs=pl.BlockSpec((EXPERT_TILE * ROW_SUBLANES, LANES), row),
            scratch_shapes=[pltpu.VMEM((D_MODEL, D_EXPERT), F32), pltpu.VMEM((D_MODEL, D_EXPERT), F32),
                            pltpu.VMEM((D_EXPERT, D_MODEL), F32),
                            pltpu.VMEM((D_MODEL, D_EXPERT), BF16), pltpu.VMEM((D_MODEL, D_EXPERT), BF16),
                            pltpu.VMEM((D_EXPERT, D_MODEL), BF16),
                            pltpu.SemaphoreType.DMA((3,))],
        ),
        out_shape=jax.ShapeDtypeStruct((n_rows * ROW_SUBLANES, LANES), F32),
        compiler_params=_params(("arbitrary",), 48),
    )(tile_expert, tile_first, tile_next, n_valid, xs, w_g, w_u, w_d)


def _combine_kernel(pos1_ref, pos2_ref, h1_ref, route_ref, gf_ref, y_ref, out_ref, ya_ref, yb_ref, sems):
    i = pl.program_id(0)
    n_steps = pl.num_programs(0)

    def row_copy(tile, r, pos_ref, buf_ref):
        slot = tile % 2
        src = pl.multiple_of(pos_ref[tile * OUT_TILE + r] * ROW_SUBLANES, ROW_SUBLANES)
        dst = pl.multiple_of(r * ROW_SUBLANES, ROW_SUBLANES)
        return pltpu.make_async_copy(y_ref.at[pl.ds(src, ROW_SUBLANES), :],
                                     buf_ref.at[slot, pl.ds(dst, ROW_SUBLANES), :], sems.at[slot])

    def for_each_row(tile, action):
        def group(j, carry):
            for k in range(DMA_UNROLL):
                action(row_copy(tile, j * DMA_UNROLL + k, pos1_ref, ya_ref))
                action(row_copy(tile, j * DMA_UNROLL + k, pos2_ref, yb_ref))
            return carry

        lax.fori_loop(0, OUT_TILE // DMA_UNROLL, group, 0)

    @pl.when(i == 0)
    def _():
        for_each_row(i, lambda cp: cp.start())

    @pl.when(i + 1 < n_steps)
    def _():
        for_each_row(i + 1, lambda cp: cp.start())

    for_each_row(i, lambda cp: cp.wait())
    slot = i % 2
    rec = route_ref[...]
    ya = _rows_load(ya_ref.at[slot], OUT_TILE, ROW_SUBLANES)
    yb = _rows_load(yb_ref.at[slot], OUT_TILE, ROW_SUBLANES)
    hh = h1_ref[...] + rec[:, ROUTE_W1:ROUTE_W1 + 1] * ya + rec[:, ROUTE_W2:ROUTE_W2 + 1] * yb
    ms = jnp.mean(hh * hh, axis=-1, keepdims=True)
    out_ref[...] = hh * lax.rsqrt(ms + EPS) * gf_ref[...]


def _combine(pos1, pos2, h1, route, g_f, y):
    t = h1.shape[0]
    row = lambda i, *_: (i, 0)
    return pl.pallas_call(
        _combine_kernel,
        grid_spec=pltpu.PrefetchScalarGridSpec(
            num_scalar_prefetch=2,
            grid=(t // OUT_TILE,),
            in_specs=[
                pl.BlockSpec((OUT_TILE, D_MODEL), row),
                pl.BlockSpec((OUT_TILE, LANES), row),
                pl.BlockSpec((1, D_MODEL), lambda i, *_: (0, 0)),
                pl.BlockSpec(memory_space=pl.ANY),
            ],
            out_specs=pl.BlockSpec((OUT_TILE, D_MODEL), row),
            scratch_shapes=[pltpu.VMEM((2, OUT_TILE * ROW_SUBLANES, LANES), F32),
                            pltpu.VMEM((2, OUT_TILE * ROW_SUBLANES, LANES), F32),
                            pltpu.SemaphoreType.DMA((2,))],
        ),
        out_shape=jax.ShapeDtypeStruct((t, D_MODEL), F32),
        compiler_params=_params(("arbitrary",), 40),
    )(pos1, pos2, h1, route, g_f, y)


def _routing_tables(route, cnt, n_tiles):
    counts = cnt[0, N_GROUPS:N_GROUPS + N_EXPERTS].astype(jnp.int32)
    tiles = (counts + EXPERT_TILE - 1) // EXPERT_TILE
    tile_end = jnp.cumsum(tiles)
    row_start = (tile_end - tiles) * EXPERT_TILE
    n_valid = tile_end[-1]
    expert_ids = jnp.arange(N_EXPERTS, dtype=jnp.int32)

    def positions(expert_lane, rank_lane):
        expert = route[:, expert_lane].astype(jnp.int32)
        start = jnp.sum(jnp.where(expert[:, None] == expert_ids[None, :], row_start[None, :], 0), axis=1)
        return start + route[:, rank_lane].astype(jnp.int32)

    pos1 = positions(ROUTE_E1, ROUTE_R1)
    pos2 = positions(ROUTE_E2, ROUTE_R2)
    tile = jnp.minimum(jnp.arange(n_tiles, dtype=jnp.int32), n_valid - 1)
    tile_expert = jnp.sum(tile[:, None] >= tile_end[None, :], axis=-1).astype(jnp.int32)
    owner = tile_expert[:, None] == expert_ids[None, :]
    tile_first = (tile == jnp.sum(jnp.where(owner, (tile_end - tiles)[None, :], 0), axis=1)).astype(jnp.int32)
    later = (expert_ids[None, :] > expert_ids[:, None]) & (tiles[None, :] > 0)
    next_expert = jnp.min(jnp.where(later, expert_ids[None, :], N_EXPERTS), axis=1)
    next_expert = jnp.where(next_expert == N_EXPERTS, -1, next_expert)
    tile_next = jnp.sum(jnp.where(owner, next_expert[None, :], 0), axis=1).astype(jnp.int32)
    last_row = (tile_end - 1) * EXPERT_TILE
    return pos1, pos2, tile_expert, tile_first, tile_next, n_valid.reshape(1), last_row, tiles


def kernel(x, meta_tokens, norm1_g, w_in, b_fox_f, gla_w_gate2, gla_b_gate, gla_norm_g, fox_norm_g, w_out,
           norm2_g, w_router_group, b_router_group, w_router_expert, b_router_expert, w_exp_gate, w_exp_up,
           w_exp_down, norm_f_g):
    batch, seq, _ = x.shape
    assert batch == 1 and norm1_g.shape[0] == 1
    assert seq % FOX_TILE == 0 and seq % ROW_TILE == 0
    t = HEAD_ROWS + seq
    x2 = x[0]
    head = jnp.concatenate([jnp.zeros((PAD_FRONT, D_MODEL), F32), meta_tokens.astype(F32)], axis=0)

    assert w_in.shape == (1, D_MODEL, D_IN_PROJ) and PROJ_ALIGNED % PROJ_STAGE_COLS == 0
    proj, small = _in_proj(head, x2, norm1_g, w_in[0].T)

    negc = _fox_bias(small, b_fox_f[0].reshape(FOX_HEADS, 1))
    w2_pad = jnp.zeros((LANES, GLA_DK_TOT), F32).at[FOX_HEADS:FOX_HEADS + GLA_RANK].set(gla_w_gate2[0])
    o_gla = _gla(proj, small, w2_pad, gla_b_gate, gla_norm_g)
    o_fox = _fox(proj, negc.reshape(FOX_HEADS, 1, t), fox_norm_g)

    w_router = jnp.concatenate(
        [w_router_group[0], jnp.transpose(w_router_expert[0], (1, 0, 2)).reshape(D_MODEL, N_EXPERTS),
         jnp.zeros((D_MODEL, LANES - N_GROUPS - N_EXPERTS), F32)], axis=1).astype(BF16)
    b_router = jnp.concatenate([b_router_group[0], b_router_expert[0].reshape(-1),
                                jnp.zeros((LANES - N_GROUPS - N_EXPERTS,), F32)]).reshape(1, LANES)
    h1, u2p, route, cnt = _out_proj(o_gla, o_fox, x2, w_out[0].astype(BF16), norm2_g, w_router, b_router)

    n_tiles = (2 * seq) // EXPERT_TILE + N_EXPERTS
    pos1, pos2, tile_expert, tile_first, tile_next, n_valid, last_row, tiles = _routing_tables(route, cnt, n_tiles)
    xs = _scatter(pos1, pos2, last_row, tiles, n_valid, u2p, n_tiles * EXPERT_TILE)
    y = _moe(tile_expert, tile_first, tile_next, n_valid, xs,
             w_exp_gate[0].reshape(N_EXPERTS, D_MODEL, D_EXPERT),
             w_exp_up[0].reshape(N_EXPERTS, D_MODEL, D_EXPERT),
             w_exp_down[0].reshape(N_EXPERTS, D_EXPERT, D_MODEL))
    out = _combine(pos1, pos2, h1, route, norm_f_g.reshape(1, D_MODEL), y)
    return out.reshape(1, seq, D_MODEL)
```

```python
import jax
import jax.numpy as jnp
from jax import lax
from jax.experimental import pallas as pl
from jax.experimental.pallas import tpu as pltpu

D_MODEL = 2048
N_META = 16
GLA_HEADS = 4
GLA_DK = 128
GLA_DV = 256
GLA_DK_TOT = GLA_HEADS * GLA_DK
GLA_DV_TOT = GLA_HEADS * GLA_DV
GLA_RANK = 16
GLA_TAU = 16.0
GLA_CHUNK = 64
FOX_HEADS = 8
FOX_HD = 128
FOX_W = FOX_HEADS * FOX_HD
FOX_BLOCK = 128
PAD_FRONT = FOX_BLOCK - N_META
HEAD_ROWS = PAD_FRONT + N_META
N_GROUPS = 4
EXPERTS_PER_GROUP = 8
N_EXPERTS = N_GROUPS * EXPERTS_PER_GROUP
D_EXPERT = 512
EPS = 1e-6

LANES = 128
PROJ_ROWS = HEAD_ROWS
GLA_ROWS = 2 * GLA_CHUNK
FOX_TILE = 1024
FOX_ROWS = 128
FOX_KEYS = 1024
FOX_GROUP = 2
FOX_SKEW = 3
LOG2E = 1.4426950408889634
ROW_TILE = 512
EXPERT_TILE = 256
OUT_TILE = 256
MASK_VALUE = -1e30
PROJ_BIG = 3 * FOX_W + 2 * GLA_DK_TOT + 2 * GLA_DV_TOT
PROJ_FF = 3 * FOX_W
PROJ_GQ = PROJ_FF + FOX_HEADS
PROJ_GZ = PROJ_GQ + 2 * GLA_DK_TOT + 2 * GLA_DV_TOT
D_IN_PROJ = PROJ_GZ + GLA_RANK
PROJ_ALIGNED = (D_IN_PROJ // LANES) * LANES
PROJ_STAGE_COLS = 512
ROW_SUBLANES = D_MODEL // LANES
DMA_UNROLL = 8
SCALAR_UNROLL = 16

F32 = jnp.float32
BF16 = jnp.bfloat16
NT_DIMS = (((1,), (1,)), ((), ()))
TN_DIMS = (((0,), (0,)), ((), ()))


def _log_sigmoid(x):
    return jnp.minimum(x, 0.0) - jnp.log(1.0 + jnp.exp(-jnp.abs(x)))


def _sigmoid(x):
    return 1.0 / (1.0 + jnp.exp(-x))


def _split3(x):
    hi = x.astype(BF16)
    rest = x - hi.astype(F32)
    mid = rest.astype(BF16)
    lo = (rest - mid.astype(F32)).astype(BF16)
    return hi, mid, lo


def _rows_load(ref, n_rows, n_chunks):
    return jnp.concatenate([ref[pl.ds(s, n_rows, stride=n_chunks), :] for s in range(n_chunks)], axis=1)


def _rows_store(ref, value, n_chunks):
    n_rows = value.shape[0]
    for s in range(n_chunks):
        ref[pl.ds(s, n_rows, stride=n_chunks), :] = value[:, s * LANES:(s + 1) * LANES]


def _params(semantics, vmem_mb):
    return pltpu.CompilerParams(dimension_semantics=semantics, vmem_limit_bytes=vmem_mb * 1024 * 1024)


def _in_proj_kernel(head_ref, x_ref, g_ref, wt_hbm, proj_ref, small_ref, w_ref, stage_ref, tail_ref, sems):
    n_chunks = PROJ_ALIGNED // PROJ_STAGE_COLS
    n_tail = D_IN_PROJ - PROJ_ALIGNED

    def chunk_copy(c):
        return pltpu.make_async_copy(wt_hbm.at[pl.ds(c * PROJ_STAGE_COLS, PROJ_STAGE_COLS), :],
                                     stage_ref.at[c % 2], sems.at[c % 2])

    def tail_copy():
        return pltpu.make_async_copy(wt_hbm.at[pl.ds(PROJ_ALIGNED, n_tail), :],
                                     tail_ref.at[pl.ds(0, n_tail), :], sems.at[2])

    @pl.when(pl.program_id(0) == 0)
    def _():
        tail_ref[...] = jnp.zeros_like(tail_ref)
        tail_copy().start()
        chunk_copy(0).start()
        for c in range(n_chunks):
            if c + 1 < n_chunks:
                chunk_copy(c + 1).start()
            chunk_copy(c).wait()
            w_ref[:, c * PROJ_STAGE_COLS:(c + 1) * PROJ_STAGE_COLS] = stage_ref[c % 2].T.astype(BF16)
        tail_copy().wait()
        w_ref[:, PROJ_ALIGNED:] = tail_ref[...].T.astype(BF16)

    x = jnp.where(pl.program_id(0) == 0, head_ref[...], x_ref[...])
    ms = jnp.mean(x * x, axis=-1, keepdims=True)
    xn = (x * lax.rsqrt(ms + EPS) * g_ref[...]).astype(BF16)
    p = jnp.dot(xn, w_ref[...], preferred_element_type=F32)
    lane = lax.broadcasted_iota(jnp.int32, (PROJ_ROWS, LANES), 1)
    small_ref[...] = jnp.where(lane < FOX_HEADS, p[:, PROJ_FF:PROJ_FF + LANES], p[:, PROJ_ALIGNED:])
    proj_ref[...] = jnp.concatenate([p[:, :PROJ_FF], p[:, PROJ_GQ:PROJ_GZ]], axis=1).astype(BF16)


def _in_proj(head, x, g1, w_t):
    t = HEAD_ROWS + x.shape[0]
    fixed = lambda i: (0, 0)
    return pl.pallas_call(
        _in_proj_kernel,
        grid=(t // PROJ_ROWS,),
        in_specs=[
            pl.BlockSpec((PROJ_ROWS, D_MODEL), fixed),
            pl.BlockSpec((PROJ_ROWS, D_MODEL), lambda i: (jnp.maximum(i - 1, 0), 0)),
            pl.BlockSpec((1, D_MODEL), fixed),
            pl.BlockSpec(memory_space=pl.ANY),
        ],
        out_specs=[
            pl.BlockSpec((PROJ_ROWS, PROJ_BIG), lambda i: (i, 0)),
            pl.BlockSpec((PROJ_ROWS, LANES), lambda i: (i, 0)),
        ],
        out_shape=[
            jax.ShapeDtypeStruct((t, PROJ_BIG), BF16),
            jax.ShapeDtypeStruct((t, LANES), F32),
        ],
        scratch_shapes=[pltpu.VMEM((D_MODEL, PROJ_ALIGNED + LANES), BF16),
                        pltpu.VMEM((2, PROJ_STAGE_COLS, D_MODEL), F32),
                        pltpu.VMEM((LANES, D_MODEL), F32),
                        pltpu.SemaphoreType.DMA((3,))],
        compiler_params=_params(("arbitrary",), 56),
    )(head, x, g1, w_t)


def _fox_bias_kernel(small_ref, bf_ref, negc_ref):
    t = small_ref.shape[0]
    r = lax.broadcasted_iota(jnp.int32, (LANES, LANES), 0)
    c = lax.broadcasted_iota(jnp.int32, (LANES, LANES), 1)
    upper = jnp.where(r <= c, 1.0, 0.0).astype(BF16)
    lane = lax.broadcasted_iota(jnp.int32, (FOX_HEADS, LANES), 1)

    def body(b, carry):
        off = pl.multiple_of(b * LANES, LANES)
        valid = (off + lane) >= PAD_FRONT
        f_logit = small_ref[pl.ds(off, LANES), :].T[0:FOX_HEADS, :]
        lf = jnp.where(valid, _log_sigmoid(f_logit + bf_ref[...]), 0.0)
        cum = sum(jnp.dot(piece, upper, preferred_element_type=F32) for piece in _split3(lf)) + carry
        negc_ref[:, pl.ds(off, LANES)] = jnp.where(valid, -LOG2E * cum, MASK_VALUE)
        return cum[:, LANES - 1:LANES]

    lax.fori_loop(0, t // LANES, body, jnp.zeros((FOX_HEADS, 1), F32))


def _fox_bias(small, b_f):
    return pl.pallas_call(
        _fox_bias_kernel,
        out_shape=jax.ShapeDtypeStruct((FOX_HEADS, small.shape[0]), F32),
    )(small, b_f)


def _gla_kernel(q_ref, k_ref, v_ref, r_ref, small_ref, w2_ref, bg_ref, ng_ref, o_ref, st_ref):
    i = pl.program_id(0)

    @pl.when(i == 0)
    def _():
        st_ref[...] = jnp.zeros_like(st_ref)

    z_hi, z_mid, _ = _split3(small_ref[...])
    w_hi, w_mid, _ = _split3(w2_ref[...])
    gate_logit = (jnp.dot(z_hi, w_hi, preferred_element_type=F32) + jnp.dot(z_hi, w_mid, preferred_element_type=F32)
                  + jnp.dot(z_mid, w_hi, preferred_element_type=F32) + bg_ref[...])
    g = _log_sigmoid(gate_logit) * (1.0 / GLA_TAU)
    rowid = i * GLA_ROWS + lax.broadcasted_iota(jnp.int32, (GLA_ROWS, 1), 0)
    g = jnp.where(rowid >= PAD_FRONT, g, 0.0)

    ci = lax.broadcasted_iota(jnp.int32, (GLA_CHUNK, GLA_CHUNK), 0)
    cj = lax.broadcasted_iota(jnp.int32, (GLA_CHUNK, GLA_CHUNK), 1)
    causal = cj <= ci
    lower = jnp.where(causal, 1.0, 0.0).astype(BF16)
    scale = GLA_DK ** -0.5
    mid = GLA_CHUNK // 2

    for c in range(GLA_ROWS // GLA_CHUNK):
        rows = slice(c * GLA_CHUNK, (c + 1) * GLA_CHUNK)
        b = sum(jnp.dot(lower, piece, preferred_element_type=F32) for piece in _split3(g[rows]))
        b_mid = b[mid:mid + 1]
        b_last = b[GLA_CHUNK - 1:GLA_CHUNK]
        q = q_ref[rows, :].astype(F32) * scale
        k = k_ref[rows, :].astype(F32)
        q_intra = (q * jnp.exp(b - b_mid)).astype(BF16)
        k_intra = (k * jnp.exp(b_mid - b)).astype(BF16)
        q_inter = (q * jnp.exp(b)).astype(BF16)
        k_state = (k * jnp.exp(b_last - b)).astype(BF16)
        decay = jnp.exp(b_last)
        heads = range(GLA_HEADS)
        ks = [slice(h * GLA_DK, (h + 1) * GLA_DK) for h in heads]
        vs = [slice(h * GLA_DV, (h + 1) * GLA_DV) for h in heads]
        v = [v_ref[rows, vs[h]] for h in heads]
        st = [st_ref[h] for h in heads]
        a = [lax.dot_general(q_intra[:, ks[h]], k_intra[:, ks[h]], NT_DIMS, preferred_element_type=F32)
             for h in heads]
        o_inter = [lax.dot_general(q_inter[:, ks[h]], st[h].astype(BF16), NT_DIMS, preferred_element_type=F32)
                   for h in heads]
        u_t = [lax.dot_general(v[h], k_state[:, ks[h]], TN_DIMS, preferred_element_type=F32) for h in heads]
        for h in heads:
            st_ref[h] = decay[:, ks[h]] * st[h] + u_t[h]
        for h in heads:
            o = o_inter[h] + jnp.dot(jnp.where(causal, a[h], 0.0).astype(BF16), v[h], preferred_element_type=F32)
            ms = jnp.mean(o * o, axis=-1, keepdims=True)
            y = o * lax.rsqrt(ms + EPS) * ng_ref[...]
            r = r_ref[rows, vs[h]].astype(F32)
            o_ref[rows, vs[h]] = (y * (r * _sigmoid(r))).astype(BF16)


def _gla(proj, small, w2_pad, b_gate, norm_g):
    t = proj.shape[0]
    q_blk = (3 * FOX_W) // GLA_DK_TOT
    v_blk = (3 * FOX_W + 2 * GLA_DK_TOT) // GLA_DV_TOT
    return pl.pallas_call(
        _gla_kernel,
        grid=(t // GLA_ROWS,),
        in_specs=[
            pl.BlockSpec((GLA_ROWS, GLA_DK_TOT), lambda i: (i, q_blk)),
            pl.BlockSpec((GLA_ROWS, GLA_DK_TOT), lambda i: (i, q_blk + 1)),
            pl.BlockSpec((GLA_ROWS, GLA_DV_TOT), lambda i: (i, v_blk)),
            pl.BlockSpec((GLA_ROWS, GLA_DV_TOT), lambda i: (i, v_blk + 1)),
            pl.BlockSpec((GLA_ROWS, LANES), lambda i: (i, 0)),
            pl.BlockSpec((LANES, GLA_DK_TOT), lambda i: (0, 0)),
            pl.BlockSpec((1, GLA_DK_TOT), lambda i: (0, 0)),
            pl.BlockSpec((1, GLA_DV), lambda i: (0, 0)),
        ],
        out_specs=pl.BlockSpec((GLA_ROWS, GLA_DV_TOT), lambda i: (jnp.maximum(i - 1, 0), 0)),
        out_shape=jax.ShapeDtypeStruct((t - HEAD_ROWS, GLA_DV_TOT), BF16),
        scratch_shapes=[pltpu.VMEM((GLA_HEADS, GLA_DV, GLA_DK), F32)],
        compiler_params=_params(("arbitrary",), 32),
    )(proj, proj, proj, proj, small, w2_pad, b_gate, norm_g)


def _fox_kernel(q_ref, k_ref, v_ref, negc_ref, ng_ref, o_ref, qs_ref, va_ref):
    qi = pl.program_id(1)
    n_blocks = FOX_TILE // FOX_ROWS

    @pl.when(qi == 0)
    def _():
        lane = lax.broadcasted_iota(jnp.int32, (v_ref.shape[0], FOX_HD), 1)
        ones_col = jnp.where(lane == 0, 1.0, 0.0).astype(BF16)
        for hh in range(FOX_GROUP):
            va_ref[:, 2 * hh * FOX_HD:(2 * hh + 1) * FOX_HD] = v_ref[:, hh * FOX_HD:(hh + 1) * FOX_HD]
            va_ref[:, (2 * hh + 1) * FOX_HD:(2 * hh + 2) * FOX_HD] = ones_col

    units = [(hh, rb) for hh in range(FOX_GROUP) for rb in range(n_blocks)]
    q0 = pl.multiple_of(HEAD_ROWS + qi * FOX_TILE, FOX_ROWS)
    qs_ref[...] = (q_ref[pl.ds(q0, FOX_TILE), :].astype(F32) * (FOX_HD ** -0.5 * LOG2E)).astype(BF16)
    row = lax.broadcasted_iota(jnp.int32, (FOX_ROWS, FOX_ROWS), 0)
    col = lax.broadcasted_iota(jnp.int32, (FOX_ROWS, FOX_ROWS), 1)

    def run(state, steps):
        def scores(step):
            u, off, k0, k1, causal_tail = step
            hh, rb = units[u]
            rows = slice(rb * FOX_ROWS, (rb + 1) * FOX_ROWS)
            cols = slice(hh * FOX_HD, (hh + 1) * FOX_HD)
            s = lax.dot_general(qs_ref[rows, cols], k_ref[pl.ds(off + k0, k1 - k0), cols], NT_DIMS,
                                preferred_element_type=F32)
            s = s + negc_ref[hh, :, pl.ds(off + k0, k1 - k0)]
            if causal_tail:
                tail = jnp.where(col <= row, s[:, k1 - k0 - FOX_ROWS:], MASK_VALUE)
                s = tail if k1 - k0 == FOX_ROWS else jnp.concatenate([s[:, :k1 - k0 - FOX_ROWS], tail], axis=1)
            return s

        def update(step, s, state):
            u, off, k0, k1, _ = step
            hh, _ = units[u]
            m_prev, acc_prev = state[u]
            m_new = jnp.maximum(m_prev, jnp.max(s, axis=-1, keepdims=True))
            p = jnp.exp2(s - m_new).astype(BF16)
            acc_new = jnp.exp2(m_prev - m_new) * acc_prev + jnp.dot(
                p, va_ref[pl.ds(off + k0, k1 - k0), 2 * hh * FOX_HD:(2 * hh + 2) * FOX_HD],
                preferred_element_type=F32)
            state[u] = (m_new, acc_new)

        state = list(state)
        pending = [scores(st) for st in steps[:FOX_SKEW]]
        for j, st in enumerate(steps):
            if j + FOX_SKEW < len(steps):
                pending.append(scores(steps[j + FOX_SKEW]))
            update(st, pending[j], state)
            pending[j] = None
        return tuple(state)

    head_steps = [(u, 0, 0, HEAD_ROWS, False) for u in range(len(units))]

    def full_steps(off):
        return [(u, off, k0, k0 + FOX_KEYS, False)
                for k0 in range(0, FOX_TILE, FOX_KEYS) for u in range(len(units))]

    diag_steps = []
    for k0 in range(0, FOX_TILE, FOX_KEYS):
        for u, (_, rb) in enumerate(units):
            last = (rb + 1) * FOX_ROWS
            if last > k0:
                diag_steps.append((u, q0, k0, min(k0 + FOX_KEYS, last), last <= k0 + FOX_KEYS))

    state = tuple((jnp.full((FOX_ROWS, 1), MASK_VALUE, F32), jnp.zeros((FOX_ROWS, 2 * FOX_HD), F32))
                  for _ in units)
    state = run(state, head_steps)
    state = lax.fori_loop(
        0, qi, lambda kt, st: run(st, full_steps(pl.multiple_of(HEAD_ROWS + kt * FOX_TILE, FOX_ROWS))), state)
    state = run(state, diag_steps)
    for u, (hh, rb) in enumerate(units):
        _, acc = state[u]
        o = acc[:, :FOX_HD] / acc[:, FOX_HD:FOX_HD + 1]
        ms = jnp.mean(o * o, axis=-1, keepdims=True)
        o_ref[rb * FOX_ROWS:(rb + 1) * FOX_ROWS, hh * FOX_HD:(hh + 1) * FOX_HD] = (
            o * lax.rsqrt(ms + EPS) * ng_ref[...]).astype(BF16)


def _fox(proj, negc3, norm_g):
    t = proj.shape[0]
    width = FOX_GROUP * FOX_HD
    k_blk = FOX_W // width
    return pl.pallas_call(
        _fox_kernel,
        grid=(FOX_HEADS // FOX_GROUP, (t - HEAD_ROWS) // FOX_TILE),
        in_specs=[
            pl.BlockSpec((t, width), lambda g, i: (0, g)),
            pl.BlockSpec((t, width), lambda g, i: (0, k_blk + g)),
            pl.BlockSpec((t, width), lambda g, i: (0, 2 * k_blk + g)),
            pl.BlockSpec((FOX_GROUP, 1, t), lambda g, i: (g, 0, 0)),
            pl.BlockSpec((1, FOX_HD), lambda g, i: (0, 0)),
        ],
        out_specs=pl.BlockSpec((FOX_TILE, width), lambda g, i: (i, g)),
        out_shape=jax.ShapeDtypeStruct((t - HEAD_ROWS, FOX_W), BF16),
        scratch_shapes=[pltpu.VMEM((FOX_TILE, width), BF16), pltpu.VMEM((t, 2 * width), BF16)],
        compiler_params=_params(("arbitrary", "arbitrary"), 56),
    )(proj, proj, proj, negc3, norm_g)


ROUTE_E1, ROUTE_E2, ROUTE_R1, ROUTE_R2, ROUTE_W1, ROUTE_W2 = range(6)


def _out_proj_kernel(og_ref, of_ref, h_ref, wo_ref, g2_ref, wr_ref, br_ref,
                     h1_ref, u2p_ref, route_ref, cnt_ref, tri_ref, run_ref):
    i = pl.program_id(0)

    @pl.when(i == 0)
    def _():
        r = lax.broadcasted_iota(jnp.int32, (ROW_TILE, ROW_TILE), 0)
        c = lax.broadcasted_iota(jnp.int32, (ROW_TILE, ROW_TILE), 1)
        tri_ref[...] = jnp.where(c < r, 1.0, 0.0).astype(BF16)
        run_ref[...] = jnp.zeros_like(run_ref)

    h1 = (h_ref[...]
          + jnp.dot(og_ref[...], wo_ref[0:GLA_DV_TOT, :], preferred_element_type=F32)
          + jnp.dot(of_ref[...], wo_ref[GLA_DV_TOT:, :], preferred_element_type=F32))
    h1_ref[...] = h1
    ms = jnp.mean(h1 * h1, axis=-1, keepdims=True)
    u2 = h1 * lax.rsqrt(ms + EPS) * g2_ref[...]
    _rows_store(u2p_ref, u2, ROW_SUBLANES)

    logits = jnp.dot(u2.astype(BF16), wr_ref[...], preferred_element_type=F32) + br_ref[...]
    lane = lax.broadcasted_iota(jnp.int32, logits.shape, 1).astype(F32)
    ninf = -jnp.inf

    def first_max(vals):
        top = jnp.max(vals, axis=-1, keepdims=True)
        idx = jnp.min(jnp.where(vals == top, lane, float(LANES)), axis=-1, keepdims=True)
        return top, idx

    gl = jnp.where(lane < N_GROUPS, logits, ninf)
    g_top, g_idx = first_max(gl)
    p_g = 1.0 / jnp.sum(jnp.exp(gl - g_top), axis=-1, keepdims=True)
    e_lo = N_GROUPS + EXPERTS_PER_GROUP * g_idx
    el = jnp.where((lane >= e_lo) & (lane < e_lo + EXPERTS_PER_GROUP), logits, ninf)
    top1, i1 = first_max(el)
    top2, i2 = first_max(jnp.where(lane == i1, ninf, el))
    ratio = jnp.exp(top2 - top1)
    w1 = 1.0 / (1.0 + ratio)
    w2 = ratio * w1

    is1 = lane == i1
    is2 = lane == i2
    onehot = jnp.where(is1 | is2, 1.0, 0.0)
    before = jnp.dot(tri_ref[...], onehot.astype(BF16), preferred_element_type=F32) + run_ref[...]
    r1 = jnp.sum(jnp.where(is1, before, 0.0), axis=-1, keepdims=True)
    r2 = jnp.sum(jnp.where(is2, before, 0.0), axis=-1, keepdims=True)
    run_ref[...] = run_ref[...] + jnp.sum(onehot, axis=0, keepdims=True)
    cnt_ref[...] = run_ref[...]

    rec = jnp.zeros_like(logits)
    for slot, val in ((ROUTE_E1, i1 - N_GROUPS), (ROUTE_E2, i2 - N_GROUPS), (ROUTE_R1, r1), (ROUTE_R2, r2),
                      (ROUTE_W1, p_g * w1), (ROUTE_W2, p_g * w2)):
        rec = jnp.where(lane == slot, val, rec)
    route_ref[...] = rec


def _out_proj(o_gla, o_fox, h0, w_out, g2, w_router, b_router):
    t = h0.shape[0]
    row = lambda i: (i, 0)
    fixed = lambda i: (0, 0)
    return pl.pallas_call(
        _out_proj_kernel,
        grid=(t // ROW_TILE,),
        in_specs=[
            pl.BlockSpec((ROW_TILE, GLA_DV_TOT), row),
            pl.BlockSpec((ROW_TILE, FOX_W), row),
            pl.BlockSpec((ROW_TILE, D_MODEL), row),
            pl.BlockSpec((D_MODEL, D_MODEL), fixed, pipeline_mode=pl.Buffered(1)),
            pl.BlockSpec((1, D_MODEL), fixed),
            pl.BlockSpec((D_MODEL, LANES), fixed),
            pl.BlockSpec((1, LANES), fixed),
        ],
        out_specs=[
            pl.BlockSpec((ROW_TILE, D_MODEL), row),
            pl.BlockSpec((ROW_TILE * ROW_SUBLANES, LANES), row),
            pl.BlockSpec((ROW_TILE, LANES), row),
            pl.BlockSpec((1, LANES), fixed),
        ],
        out_shape=[
            jax.ShapeDtypeStruct((t, D_MODEL), F32),
            jax.ShapeDtypeStruct((t * ROW_SUBLANES, LANES), F32),
            jax.ShapeDtypeStruct((t, LANES), F32),
            jax.ShapeDtypeStruct((1, LANES), F32),
        ],
        scratch_shapes=[pltpu.VMEM((ROW_TILE, ROW_TILE), BF16), pltpu.VMEM((1, LANES), F32)],
        compiler_params=_params(("arbitrary",), 48),
    )(o_gla, o_fox, h0, w_out, g2, w_router, b_router)


def _invert_kernel(pos1_ref, pos2_ref, inv_ref):
    n_rows = inv_ref.shape[0]
    n_tokens = pos1_ref.shape[0]

    def fill(j, carry):
        for k in range(SCALAR_UNROLL):
            inv_ref[j * SCALAR_UNROLL + k] = 0
        return carry

    def place(j, carry):
        for k in range(SCALAR_UNROLL):
            tok = j * SCALAR_UNROLL + k
            inv_ref[pos1_ref[tok]] = tok
            inv_ref[pos2_ref[tok]] = tok
        return carry

    lax.fori_loop(0, n_rows // SCALAR_UNROLL, fill, 0)
    lax.fori_loop(0, n_tokens // SCALAR_UNROLL, place, 0)


def _invert(pos1, pos2, n_rows):
    smem = pl.BlockSpec(memory_space=pltpu.SMEM)
    return pl.pallas_call(
        _invert_kernel,
        in_specs=[smem, smem],
        out_specs=smem,
        out_shape=jax.ShapeDtypeStruct((n_rows,), jnp.int32),
    )(pos1, pos2)


def _moe_kernel(te_ref, first_ref, next_ref, nv_ref, inv_ref, u2_hbm, wg_hbm, wu_hbm, wd_hbm, y_ref,
                x_buf, wg_stage, wu_stage, wd_stage, wgb_ref, wub_ref, wdb_ref, xsems, wsems):
    i = pl.program_id(0)
    n_valid = nv_ref[0]

    def weight_copies(expert):
        return (pltpu.make_async_copy(wg_hbm.at[expert], wg_stage, wsems.at[0]),
                pltpu.make_async_copy(wu_hbm.at[expert], wu_stage, wsems.at[1]),
                pltpu.make_async_copy(wd_hbm.at[expert], wd_stage, wsems.at[2]))

    def row_copy(tile, r):
        slot = tile % 2
        src = pl.multiple_of(inv_ref[tile * EXPERT_TILE + r] * ROW_SUBLANES, ROW_SUBLANES)
        dst = pl.multiple_of(r * ROW_SUBLANES, ROW_SUBLANES)
        return pltpu.make_async_copy(u2_hbm.at[pl.ds(src, ROW_SUBLANES), :],
                                     x_buf.at[slot, pl.ds(dst, ROW_SUBLANES), :], xsems.at[slot])

    def for_each_row(tile, action):
        def group(j, carry):
            for k in range(DMA_UNROLL):
                action(row_copy(tile, j * DMA_UNROLL + k))
            return carry

        lax.fori_loop(0, EXPERT_TILE // DMA_UNROLL, group, 0)

    @pl.when(i == 0)
    def _():
        for_each_row(i, lambda cp: cp.start())

    @pl.when(i >= n_valid)
    def _():
        y_ref[...] = jnp.zeros_like(y_ref)

    @pl.when(i < n_valid)
    def _():
        @pl.when(first_ref[i] == 1)
        def _():
            @pl.when(i == 0)
            def _():
                for cp in weight_copies(te_ref[i]):
                    cp.start()

            for cp in weight_copies(te_ref[i]):
                cp.wait()
            wgb_ref[...] = wg_stage[...].astype(BF16)
            wub_ref[...] = wu_stage[...].astype(BF16)
            wdb_ref[...] = wd_stage[...].astype(BF16)

            @pl.when(next_ref[i] >= 0)
            def _():
                for cp in weight_copies(next_ref[i]):
                    cp.start()

        for_each_row(i, lambda cp: cp.wait())
        for r in range(EXPERT_TILE):
            row_copy(i + 1, r).start()
        x = _rows_load(x_buf.at[i % 2], EXPERT_TILE, ROW_SUBLANES).astype(BF16)
        hg = jnp.dot(x, wgb_ref[...], preferred_element_type=F32)
        hu = jnp.dot(x, wub_ref[...], preferred_element_type=F32)
        hm = (hg * _sigmoid(hg) * hu).astype(BF16)
        y = jnp.dot(hm, wdb_ref[...], preferred_element_type=F32)
        _rows_store(y_ref, y, ROW_SUBLANES)

        @pl.when(i == n_valid - 1)
        def _():
            for_each_row(i + 1, lambda cp: cp.wait())


def _moe(tile_expert, tile_first, tile_next, n_valid, inv, u2, w_g, w_u, w_d):
    n_tiles = inv.shape[0] // EXPERT_TILE - 1
    anywhere = pl.BlockSpec(memory_space=pl.ANY)
    return pl.pallas_call(
        _moe_kernel,
        grid_spec=pltpu.PrefetchScalarGridSpec(
            num_scalar_prefetch=5,
            grid=(n_tiles,),
            in_specs=[anywhere, anywhere, anywhere, anywhere],
            out_specs=pl.BlockSpec((EXPERT_TILE * ROW_SUBLANES, LANES), lambda i, *_: (i, 0)),
            scratch_shapes=[pltpu.VMEM((2, EXPERT_TILE * ROW_SUBLANES, LANES), F32),
                            pltpu.VMEM((D_MODEL, D_EXPERT), F32), pltpu.VMEM((D_MODEL, D_EXPERT), F32),
                            pltpu.VMEM((D_EXPERT, D_MODEL), F32),
                            pltpu.VMEM((D_MODEL, D_EXPERT), BF16), pltpu.VMEM((D_MODEL, D_EXPERT), BF16),
                            pltpu.VMEM((D_EXPERT, D_MODEL), BF16),
                            pltpu.SemaphoreType.DMA((2,)), pltpu.SemaphoreType.DMA((3,))],
        ),
        out_shape=jax.ShapeDtypeStruct((n_tiles * EXPERT_TILE * ROW_SUBLANES, LANES), F32),
        compiler_params=_params(("arbitrary",), 48),
    )(tile_expert, tile_first, tile_next, n_valid, inv, u2, w_g, w_u, w_d)


def _combine_kernel(pos1_ref, pos2_ref, h1_ref, route_ref, gf_ref, y_ref, out_ref, ya_ref, yb_ref, sems):
    i = pl.program_id(0)
    n_steps = pl.num_programs(0)

    def row_copy(tile, r, pos_ref, buf_ref):
        slot = tile % 2
        src = pl.multiple_of(pos_ref[tile * OUT_TILE + r] * ROW_SUBLANES, ROW_SUBLANES)
        dst = pl.multiple_of(r * ROW_SUBLANES, ROW_SUBLANES)
        return pltpu.make_async_copy(y_ref.at[pl.ds(src, ROW_SUBLANES), :],
                                     buf_ref.at[slot, pl.ds(dst, ROW_SUBLANES), :], sems.at[slot])

    def for_each_row(tile, action):
        def group(j, carry):
            for k in range(DMA_UNROLL):
                action(row_copy(tile, j * DMA_UNROLL + k, pos1_ref, ya_ref))
                action(row_copy(tile, j * DMA_UNROLL + k, pos2_ref, yb_ref))
            return carry

        lax.fori_loop(0, OUT_TILE // DMA_UNROLL, group, 0)

    @pl.when(i == 0)
    def _():
        for_each_row(i, lambda cp: cp.start())

    @pl.when(i + 1 < n_steps)
    def _():
        for_each_row(i + 1, lambda cp: cp.start())

    for_each_row(i, lambda cp: cp.wait())
    slot = i % 2
    rec = route_ref[...]
    ya = _rows_load(ya_ref.at[slot], OUT_TILE, ROW_SUBLANES)
    yb = _rows_load(yb_ref.at[slot], OUT_TILE, ROW_SUBLANES)
    hh = h1_ref[...] + rec[:, ROUTE_W1:ROUTE_W1 + 1] * ya + rec[:, ROUTE_W2:ROUTE_W2 + 1] * yb
    ms = jnp.mean(hh * hh, axis=-1, keepdims=True)
    out_ref[...] = hh * lax.rsqrt(ms + EPS) * gf_ref[...]


def _combine(pos1, pos2, h1, route, g_f, y):
    t = h1.shape[0]
    row = lambda i, *_: (i, 0)
    return pl.pallas_call(
        _combine_kernel,
        grid_spec=pltpu.PrefetchScalarGridSpec(
            num_scalar_prefetch=2,
            grid=(t // OUT_TILE,),
            in_specs=[
                pl.BlockSpec((OUT_TILE, D_MODEL), row),
                pl.BlockSpec((OUT_TILE, LANES), row),
                pl.BlockSpec((1, D_MODEL), lambda i, *_: (0, 0)),
                pl.BlockSpec(memory_space=pl.ANY),
            ],
            out_specs=pl.BlockSpec((OUT_TILE, D_MODEL), row),
            scratch_shapes=[pltpu.VMEM((2, OUT_TILE * ROW_SUBLANES, LANES), F32),
                            pltpu.VMEM((2, OUT_TILE * ROW_SUBLANES, LANES), F32),
                            pltpu.SemaphoreType.DMA((2,))],
        ),
        out_shape=jax.ShapeDtypeStruct((t, D_MODEL), F32),
        compiler_params=_params(("arbitrary",), 40),
    )(pos1, pos2, h1, route, g_f, y)


def _routing_tables(route, cnt, n_tiles):
    counts = cnt[0, N_GROUPS:N_GROUPS + N_EXPERTS].astype(jnp.int32)
    tiles = (counts + EXPERT_TILE - 1) // EXPERT_TILE
    tile_end = jnp.cumsum(tiles)
    row_start = (tile_end - tiles) * EXPERT_TILE
    n_valid = tile_end[-1]
    expert_ids = jnp.arange(N_EXPERTS, dtype=jnp.int32)

    def positions(expert_lane, rank_lane):
        expert = route[:, expert_lane].astype(jnp.int32)
        start = jnp.sum(jnp.where(expert[:, None] == expert_ids[None, :], row_start[None, :], 0), axis=1)
        return start + route[:, rank_lane].astype(jnp.int32)

    pos1 = positions(ROUTE_E1, ROUTE_R1)
    pos2 = positions(ROUTE_E2, ROUTE_R2)
    tile = jnp.minimum(jnp.arange(n_tiles, dtype=jnp.int32), n_valid - 1)
    tile_expert = jnp.sum(tile[:, None] >= tile_end[None, :], axis=-1).astype(jnp.int32)
    owner = tile_expert[:, None] == expert_ids[None, :]
    tile_first = (tile == jnp.sum(jnp.where(owner, (tile_end - tiles)[None, :], 0), axis=1)).astype(jnp.int32)
    later = (expert_ids[None, :] > expert_ids[:, None]) & (tiles[None, :] > 0)
    next_expert = jnp.min(jnp.where(later, expert_ids[None, :], N_EXPERTS), axis=1)
    next_expert = jnp.where(next_expert == N_EXPERTS, -1, next_expert)
    tile_next = jnp.sum(jnp.where(owner, next_expert[None, :], 0), axis=1).astype(jnp.int32)
    return pos1, pos2, tile_expert, tile_first, tile_next, n_valid.reshape(1)


def kernel(x, meta_tokens, norm1_g, w_in, b_fox_f, gla_w_gate2, gla_b_gate, gla_norm_g, fox_norm_g, w_out,
           norm2_g, w_router_group, b_router_group, w_router_expert, b_router_expert, w_exp_gate, w_exp_up,
           w_exp_down, norm_f_g):
    batch, seq, _ = x.shape
    assert batch == 1 and norm1_g.shape[0] == 1
    assert seq % FOX_TILE == 0 and seq % ROW_TILE == 0
    t = HEAD_ROWS + seq
    x2 = x[0]
    head = jnp.concatenate([jnp.zeros((PAD_FRONT, D_MODEL), F32), meta_tokens.astype(F32)], axis=0)

    assert w_in.shape == (1, D_MODEL, D_IN_PROJ) and PROJ_ALIGNED % PROJ_STAGE_COLS == 0
    proj, small = _in_proj(head, x2, norm1_g, w_in[0].T)

    negc = _fox_bias(small, b_fox_f[0].reshape(FOX_HEADS, 1))
    w2_pad = jnp.zeros((LANES, GLA_DK_TOT), F32).at[FOX_HEADS:FOX_HEADS + GLA_RANK].set(gla_w_gate2[0])
    o_gla = _gla(proj, small, w2_pad, gla_b_gate, gla_norm_g)
    o_fox = _fox(proj, negc.reshape(FOX_HEADS, 1, t), fox_norm_g)

    w_router = jnp.concatenate(
        [w_router_group[0], jnp.transpose(w_router_expert[0], (1, 0, 2)).reshape(D_MODEL, N_EXPERTS),
         jnp.zeros((D_MODEL, LANES - N_GROUPS - N_EXPERTS), F32)], axis=1).astype(BF16)
    b_router = jnp.concatenate([b_router_group[0], b_router_expert[0].reshape(-1),
                                jnp.zeros((LANES - N_GROUPS - N_EXPERTS,), F32)]).reshape(1, LANES)
    h1, u2p, route, cnt = _out_proj(o_gla, o_fox, x2, w_out[0].astype(BF16), norm2_g, w_router, b_router)

    n_tiles = (2 * seq) // EXPERT_TILE + N_EXPERTS
    pos1, pos2, tile_expert, tile_first, tile_next, n_valid = _routing_tables(route, cnt, n_tiles)
    inv = _invert(pos1, pos2, (n_tiles + 1) * EXPERT_TILE)
    y = _moe(tile_expert, tile_first, tile_next, n_valid, inv, u2p,
             w_exp_gate[0].reshape(N_EXPERTS, D_MODEL, D_EXPERT),
             w_exp_up[0].reshape(N_EXPERTS, D_MODEL, D_EXPERT),
             w_exp_down[0].reshape(N_EXPERTS, D_EXPERT, D_MODEL))
    out = _combine(pos1, pos2, h1, route, norm_f_g.reshape(1, D_MODEL), y)
    return out.reshape(1, seq, D_MODEL)
```

```python
import jax
import jax.numpy as jnp
from jax import lax
from jax.experimental import pallas as pl
from jax.experimental.pallas import tpu as pltpu

D_MODEL = 2048
N_META = 16
GLA_HEADS = 4
GLA_DK = 128
GLA_DV = 256
GLA_DK_TOT = GLA_HEADS * GLA_DK
GLA_DV_TOT = GLA_HEADS * GLA_DV
GLA_RANK = 16
GLA_TAU = 16.0
GLA_CHUNK = 64
FOX_HEADS = 8
FOX_HD = 128
FOX_W = FOX_HEADS * FOX_HD
FOX_BLOCK = 128
PAD_FRONT = FOX_BLOCK - N_META
HEAD_ROWS = PAD_FRONT + N_META
N_GROUPS = 4
EXPERTS_PER_GROUP = 8
N_EXPERTS = N_GROUPS * EXPERTS_PER_GROUP
D_EXPERT = 512
EPS = 1e-6

LANES = 128
PROJ_ROWS = HEAD_ROWS
GLA_ROWS = 2 * GLA_CHUNK
FOX_TILE = 1024
FOX_ROWS = 128
FOX_KEYS = 1024
FOX_GROUP = 2
FOX_SKEW = 3
LOG2E = 1.4426950408889634
ROW_TILE = 512
EXPERT_TILE = 256
OUT_TILE = 256
MASK_VALUE = -1e30
PROJ_BIG = 3 * FOX_W + 2 * GLA_DK_TOT + 2 * GLA_DV_TOT
PROJ_FF = 3 * FOX_W
PROJ_GQ = PROJ_FF + FOX_HEADS
PROJ_GZ = PROJ_GQ + 2 * GLA_DK_TOT + 2 * GLA_DV_TOT
D_IN_PROJ = PROJ_GZ + GLA_RANK
PROJ_ALIGNED = (D_IN_PROJ // LANES) * LANES
PROJ_STAGE_COLS = 512
ROW_SUBLANES = D_MODEL // LANES
DMA_UNROLL = 8

F32 = jnp.float32
BF16 = jnp.bfloat16
NT_DIMS = (((1,), (1,)), ((), ()))
TN_DIMS = (((0,), (0,)), ((), ()))


def _log_sigmoid(x):
    return jnp.minimum(x, 0.0) - jnp.log(1.0 + jnp.exp(-jnp.abs(x)))


def _sigmoid(x):
    return 1.0 / (1.0 + jnp.exp(-x))


def _split3(x):
    hi = x.astype(BF16)
    rest = x - hi.astype(F32)
    mid = rest.astype(BF16)
    lo = (rest - mid.astype(F32)).astype(BF16)
    return hi, mid, lo


def _rows_load(ref, n_rows, n_chunks):
    return jnp.concatenate([ref[pl.ds(s, n_rows, stride=n_chunks), :] for s in range(n_chunks)], axis=1)


def _rows_store(ref, value, n_chunks):
    n_rows = value.shape[0]
    for s in range(n_chunks):
        ref[pl.ds(s, n_rows, stride=n_chunks), :] = value[:, s * LANES:(s + 1) * LANES]


def _params(semantics, vmem_mb):
    return pltpu.CompilerParams(dimension_semantics=semantics, vmem_limit_bytes=vmem_mb * 1024 * 1024)


def _in_proj_kernel(head_ref, x_ref, g_ref, wt_hbm, proj_ref, small_ref, w_ref, stage_ref, tail_ref, sems):
    n_chunks = PROJ_ALIGNED // PROJ_STAGE_COLS
    n_tail = D_IN_PROJ - PROJ_ALIGNED

    def chunk_copy(c):
        return pltpu.make_async_copy(wt_hbm.at[pl.ds(c * PROJ_STAGE_COLS, PROJ_STAGE_COLS), :],
                                     stage_ref.at[c % 2], sems.at[c % 2])

    def tail_copy():
        return pltpu.make_async_copy(wt_hbm.at[pl.ds(PROJ_ALIGNED, n_tail), :],
                                     tail_ref.at[pl.ds(0, n_tail), :], sems.at[2])

    @pl.when(pl.program_id(0) == 0)
    def _():
        tail_ref[...] = jnp.zeros_like(tail_ref)
        tail_copy().start()
        chunk_copy(0).start()
        for c in range(n_chunks):
            if c + 1 < n_chunks:
                chunk_copy(c + 1).start()
            chunk_copy(c).wait()
            w_ref[:, c * PROJ_STAGE_COLS:(c + 1) * PROJ_STAGE_COLS] = stage_ref[c % 2].T.astype(BF16)
        tail_copy().wait()
        w_ref[:, PROJ_ALIGNED:] = tail_ref[...].T.astype(BF16)

    x = jnp.where(pl.program_id(0) == 0, head_ref[...], x_ref[...])
    ms = jnp.mean(x * x, axis=-1, keepdims=True)
    xn = (x * lax.rsqrt(ms + EPS) * g_ref[...]).astype(BF16)
    p = jnp.dot(xn, w_ref[...], preferred_element_type=F32)
    lane = lax.broadcasted_iota(jnp.int32, (PROJ_ROWS, LANES), 1)
    small_ref[...] = jnp.where(lane < FOX_HEADS, p[:, PROJ_FF:PROJ_FF + LANES], p[:, PROJ_ALIGNED:])
    proj_ref[...] = jnp.concatenate([p[:, :PROJ_FF], p[:, PROJ_GQ:PROJ_GZ]], axis=1).astype(BF16)


def _in_proj(head, x, g1, w_t):
    t = HEAD_ROWS + x.shape[0]
    fixed = lambda i: (0, 0)
    return pl.pallas_call(
        _in_proj_kernel,
        grid=(t // PROJ_ROWS,),
        in_specs=[
            pl.BlockSpec((PROJ_ROWS, D_MODEL), fixed),
            pl.BlockSpec((PROJ_ROWS, D_MODEL), lambda i: (jnp.maximum(i - 1, 0), 0)),
            pl.BlockSpec((1, D_MODEL), fixed),
            pl.BlockSpec(memory_space=pl.ANY),
        ],
        out_specs=[
            pl.BlockSpec((PROJ_ROWS, PROJ_BIG), lambda i: (i, 0)),
            pl.BlockSpec((PROJ_ROWS, LANES), lambda i: (i, 0)),
        ],
        out_shape=[
            jax.ShapeDtypeStruct((t, PROJ_BIG), BF16),
            jax.ShapeDtypeStruct((t, LANES), F32),
        ],
        scratch_shapes=[pltpu.VMEM((D_MODEL, PROJ_ALIGNED + LANES), BF16),
                        pltpu.VMEM((2, PROJ_STAGE_COLS, D_MODEL), F32),
                        pltpu.VMEM((LANES, D_MODEL), F32),
                        pltpu.SemaphoreType.DMA((3,))],
        compiler_params=_params(("arbitrary",), 56),
    )(head, x, g1, w_t)


def _fox_bias_kernel(small_ref, bf_ref, negc_ref):
    t = small_ref.shape[0]
    r = lax.broadcasted_iota(jnp.int32, (LANES, LANES), 0)
    c = lax.broadcasted_iota(jnp.int32, (LANES, LANES), 1)
    upper = jnp.where(r <= c, 1.0, 0.0).astype(BF16)
    lane = lax.broadcasted_iota(jnp.int32, (FOX_HEADS, LANES), 1)

    def body(b, carry):
        off = pl.multiple_of(b * LANES, LANES)
        valid = (off + lane) >= PAD_FRONT
        f_logit = small_ref[pl.ds(off, LANES), :].T[0:FOX_HEADS, :]
        lf = jnp.where(valid, _log_sigmoid(f_logit + bf_ref[...]), 0.0)
        cum = sum(jnp.dot(piece, upper, preferred_element_type=F32) for piece in _split3(lf)) + carry
        negc_ref[:, pl.ds(off, LANES)] = jnp.where(valid, -LOG2E * cum, MASK_VALUE)
        return cum[:, LANES - 1:LANES]

    lax.fori_loop(0, t // LANES, body, jnp.zeros((FOX_HEADS, 1), F32))


def _fox_bias(small, b_f):
    return pl.pallas_call(
        _fox_bias_kernel,
        out_shape=jax.ShapeDtypeStruct((FOX_HEADS, small.shape[0]), F32),
    )(small, b_f)


def _gla_kernel(q_ref, k_ref, v_ref, r_ref, small_ref, w2_ref, bg_ref, ng_ref, o_ref, st_ref):
    i = pl.program_id(0)

    @pl.when(i == 0)
    def _():
        st_ref[...] = jnp.zeros_like(st_ref)

    z_hi, z_mid, _ = _split3(small_ref[...])
    w_hi, w_mid, _ = _split3(w2_ref[...])
    gate_logit = (jnp.dot(z_hi, w_hi, preferred_element_type=F32) + jnp.dot(z_hi, w_mid, preferred_element_type=F32)
                  + jnp.dot(z_mid, w_hi, preferred_element_type=F32) + bg_ref[...])
    g = _log_sigmoid(gate_logit) * (1.0 / GLA_TAU)
    rowid = i * GLA_ROWS + lax.broadcasted_iota(jnp.int32, (GLA_ROWS, 1), 0)
    g = jnp.where(rowid >= PAD_FRONT, g, 0.0)

    ci = lax.broadcasted_iota(jnp.int32, (GLA_CHUNK, GLA_CHUNK), 0)
    cj = lax.broadcasted_iota(jnp.int32, (GLA_CHUNK, GLA_CHUNK), 1)
    causal = cj <= ci
    lower = jnp.where(causal, 1.0, 0.0).astype(BF16)
    scale = GLA_DK ** -0.5
    mid = GLA_CHUNK // 2

    for c in range(GLA_ROWS // GLA_CHUNK):
        rows = slice(c * GLA_CHUNK, (c + 1) * GLA_CHUNK)
        b = sum(jnp.dot(lower, piece, preferred_element_type=F32) for piece in _split3(g[rows]))
        b_mid = b[mid:mid + 1]
        b_last = b[GLA_CHUNK - 1:GLA_CHUNK]
        q = q_ref[rows, :].astype(F32) * scale
        k = k_ref[rows, :].astype(F32)
        q_intra = (q * jnp.exp(b - b_mid)).astype(BF16)
        k_intra = (k * jnp.exp(b_mid - b)).astype(BF16)
        q_inter = (q * jnp.exp(b)).astype(BF16)
        k_state = (k * jnp.exp(b_last - b)).astype(BF16)
        decay = jnp.exp(b_last)
        heads = range(GLA_HEADS)
        ks = [slice(h * GLA_DK, (h + 1) * GLA_DK) for h in heads]
        vs = [slice(h * GLA_DV, (h + 1) * GLA_DV) for h in heads]
        v = [v_ref[rows, vs[h]] for h in heads]
        st = [st_ref[h] for h in heads]
        a = [lax.dot_general(q_intra[:, ks[h]], k_intra[:, ks[h]], NT_DIMS, preferred_element_type=F32)
             for h in heads]
        o_inter = [lax.dot_general(q_inter[:, ks[h]], st[h].astype(BF16), NT_DIMS, preferred_element_type=F32)
                   for h in heads]
        u_t = [lax.dot_general(v[h], k_state[:, ks[h]], TN_DIMS, preferred_element_type=F32) for h in heads]
        for h in heads:
            st_ref[h] = decay[:, ks[h]] * st[h] + u_t[h]
        for h in heads:
            o = o_inter[h] + jnp.dot(jnp.where(causal, a[h], 0.0).astype(BF16), v[h], preferred_element_type=F32)
            ms = jnp.mean(o * o, axis=-1, keepdims=True)
            y = o * lax.rsqrt(ms + EPS) * ng_ref[...]
            r = r_ref[rows, vs[h]].astype(F32)
            o_ref[rows, vs[h]] = (y * (r * _sigmoid(r))).astype(BF16)


def _gla(proj, small, w2_pad, b_gate, norm_g):
    t = proj.shape[0]
    q_blk = (3 * FOX_W) // GLA_DK_TOT
    v_blk = (3 * FOX_W + 2 * GLA_DK_TOT) // GLA_DV_TOT
    return pl.pallas_call(
        _gla_kernel,
        grid=(t // GLA_ROWS,),
        in_specs=[
            pl.BlockSpec((GLA_ROWS, GLA_DK_TOT), lambda i: (i, q_blk)),
            pl.BlockSpec((GLA_ROWS, GLA_DK_TOT), lambda i: (i, q_blk + 1)),
            pl.BlockSpec((GLA_ROWS, GLA_DV_TOT), lambda i: (i, v_blk)),
            pl.BlockSpec((GLA_ROWS, GLA_DV_TOT), lambda i: (i, v_blk + 1)),
            pl.BlockSpec((GLA_ROWS, LANES), lambda i: (i, 0)),
            pl.BlockSpec((LANES, GLA_DK_TOT), lambda i: (0, 0)),
            pl.BlockSpec((1, GLA_DK_TOT), lambda i: (0, 0)),
            pl.BlockSpec((1, GLA_DV), lambda i: (0, 0)),
        ],
        out_specs=pl.BlockSpec((GLA_ROWS, GLA_DV_TOT), lambda i: (jnp.maximum(i - 1, 0), 0)),
        out_shape=jax.ShapeDtypeStruct((t - HEAD_ROWS, GLA_DV_TOT), BF16),
        scratch_shapes=[pltpu.VMEM((GLA_HEADS, GLA_DV, GLA_DK), F32)],
        compiler_params=_params(("arbitrary",), 32),
    )(proj, proj, proj, proj, small, w2_pad, b_gate, norm_g)


def _fox_kernel(q_ref, k_ref, v_ref, negc_ref, ng_ref, o_ref, qs_ref, va_ref):
    qi = pl.program_id(1)
    n_blocks = FOX_TILE // FOX_ROWS

    @pl.when(qi == 0)
    def _():
        lane = lax.broadcasted_iota(jnp.int32, (v_ref.shape[0], FOX_HD), 1)
        ones_col = jnp.where(lane == 0, 1.0, 0.0).astype(BF16)
        for hh in range(FOX_GROUP):
            va_ref[:, 2 * hh * FOX_HD:(2 * hh + 1) * FOX_HD] = v_ref[:, hh * FOX_HD:(hh + 1) * FOX_HD]
            va_ref[:, (2 * hh + 1) * FOX_HD:(2 * hh + 2) * FOX_HD] = ones_col

    units = [(hh, rb) for hh in range(FOX_GROUP) for rb in range(n_blocks)]
    q0 = pl.multiple_of(HEAD_ROWS + qi * FOX_TILE, FOX_ROWS)
    qs_ref[...] = (q_ref[pl.ds(q0, FOX_TILE), :].astype(F32) * (FOX_HD ** -0.5 * LOG2E)).astype(BF16)
    row = lax.broadcasted_iota(jnp.int32, (FOX_ROWS, FOX_ROWS), 0)
    col = lax.broadcasted_iota(jnp.int32, (FOX_ROWS, FOX_ROWS), 1)

    def run(state, steps):
        def scores(step):
            u, off, k0, k1, causal_tail = step
            hh, rb = units[u]
            rows = slice(rb * FOX_ROWS, (rb + 1) * FOX_ROWS)
            cols = slice(hh * FOX_HD, (hh + 1) * FOX_HD)
            s = lax.dot_general(qs_ref[rows, cols], k_ref[pl.ds(off + k0, k1 - k0), cols], NT_DIMS,
                                preferred_element_type=F32)
            s = s + negc_ref[hh, :, pl.ds(off + k0, k1 - k0)]
            if causal_tail:
                tail = jnp.where(col <= row, s[:, k1 - k0 - FOX_ROWS:], MASK_VALUE)
                s = tail if k1 - k0 == FOX_ROWS else jnp.concatenate([s[:, :k1 - k0 - FOX_ROWS], tail], axis=1)
            return s

        def update(step, s, state):
            u, off, k0, k1, _ = step
            hh, _ = units[u]
            m_prev, acc_prev = state[u]
            m_new = jnp.maximum(m_prev, jnp.max(s, axis=-1, keepdims=True))
            p = jnp.exp2(s - m_new).astype(BF16)
            acc_new = jnp.exp2(m_prev - m_new) * acc_prev + jnp.dot(
                p, va_ref[pl.ds(off + k0, k1 - k0), 2 * hh * FOX_HD:(2 * hh + 2) * FOX_HD],
                preferred_element_type=F32)
            state[u] = (m_new, acc_new)

        state = list(state)
        pending = [scores(st) for st in steps[:FOX_SKEW]]
        for j, st in enumerate(steps):
            if j + FOX_SKEW < len(steps):
                pending.append(scores(steps[j + FOX_SKEW]))
            update(st, pending[j], state)
            pending[j] = None
        return tuple(state)

    head_steps = [(u, 0, 0, HEAD_ROWS, False) for u in range(len(units))]

    def full_steps(off):
        return [(u, off, k0, k0 + FOX_KEYS, False)
                for k0 in range(0, FOX_TILE, FOX_KEYS) for u in range(len(units))]

    diag_steps = []
    for k0 in range(0, FOX_TILE, FOX_KEYS):
        for u, (_, rb) in enumerate(units):
            last = (rb + 1) * FOX_ROWS
            if last > k0:
                diag_steps.append((u, q0, k0, min(k0 + FOX_KEYS, last), last <= k0 + FOX_KEYS))

    state = tuple((jnp.full((FOX_ROWS, 1), MASK_VALUE, F32), jnp.zeros((FOX_ROWS, 2 * FOX_HD), F32))
                  for _ in units)
    state = run(state, head_steps)
    state = lax.fori_loop(
        0, qi, lambda kt, st: run(st, full_steps(pl.multiple_of(HEAD_ROWS + kt * FOX_TILE, FOX_ROWS))), state)
    state = run(state, diag_steps)
    for u, (hh, rb) in enumerate(units):
        _, acc = state[u]
        o = acc[:, :FOX_HD] / acc[:, FOX_HD:FOX_HD + 1]
        ms = jnp.mean(o * o, axis=-1, keepdims=True)
        o_ref[rb * FOX_ROWS:(rb + 1) * FOX_ROWS, hh * FOX_HD:(hh + 1) * FOX_HD] = (
            o * lax.rsqrt(ms + EPS) * ng_ref[...]).astype(BF16)


def _fox(proj, negc3, norm_g):
    t = proj.shape[0]
    width = FOX_GROUP * FOX_HD
    k_blk = FOX_W // width
    return pl.pallas_call(
        _fox_kernel,
        grid=(FOX_HEADS // FOX_GROUP, (t - HEAD_ROWS) // FOX_TILE),
        in_specs=[
            pl.BlockSpec((t, width), lambda g, i: (0, g)),
            pl.BlockSpec((t, width), lambda g, i: (0, k_blk + g)),
            pl.BlockSpec((t, width), lambda g, i: (0, 2 * k_blk + g)),
            pl.BlockSpec((FOX_GROUP, 1, t), lambda g, i: (g, 0, 0)),
            pl.BlockSpec((1, FOX_HD), lambda g, i: (0, 0)),
        ],
        out_specs=pl.BlockSpec((FOX_TILE, width), lambda g, i: (i, g)),
        out_shape=jax.ShapeDtypeStruct((t - HEAD_ROWS, FOX_W), BF16),
        scratch_shapes=[pltpu.VMEM((FOX_TILE, width), BF16), pltpu.VMEM((t, 2 * width), BF16)],
        compiler_params=_params(("arbitrary", "arbitrary"), 56),
    )(proj, proj, proj, negc3, norm_g)


ROUTE_E1, ROUTE_E2, ROUTE_R1, ROUTE_R2, ROUTE_W1, ROUTE_W2 = range(6)


def _out_proj_kernel(og_ref, of_ref, h_ref, wo_ref, g2_ref, wr_ref, br_ref,
                     h1_ref, u2_ref, route_ref, cnt_ref, tri_ref, run_ref):
    i = pl.program_id(0)

    @pl.when(i == 0)
    def _():
        r = lax.broadcasted_iota(jnp.int32, (ROW_TILE, ROW_TILE), 0)
        c = lax.broadcasted_iota(jnp.int32, (ROW_TILE, ROW_TILE), 1)
        tri_ref[...] = jnp.where(c < r, 1.0, 0.0).astype(BF16)
        run_ref[...] = jnp.zeros_like(run_ref)

    h1 = (h_ref[...]
          + jnp.dot(og_ref[...], wo_ref[0:GLA_DV_TOT, :], preferred_element_type=F32)
          + jnp.dot(of_ref[...], wo_ref[GLA_DV_TOT:, :], preferred_element_type=F32))
    h1_ref[...] = h1
    ms = jnp.mean(h1 * h1, axis=-1, keepdims=True)
    u2 = (h1 * lax.rsqrt(ms + EPS) * g2_ref[...]).astype(BF16)
    u2_ref[...] = u2

    logits = jnp.dot(u2, wr_ref[...], preferred_element_type=F32) + br_ref[...]
    lane = lax.broadcasted_iota(jnp.int32, logits.shape, 1).astype(F32)
    ninf = -jnp.inf

    def first_max(vals):
        top = jnp.max(vals, axis=-1, keepdims=True)
        idx = jnp.min(jnp.where(vals == top, lane, float(LANES)), axis=-1, keepdims=True)
        return top, idx

    gl = jnp.where(lane < N_GROUPS, logits, ninf)
    g_top, g_idx = first_max(gl)
    p_g = 1.0 / jnp.sum(jnp.exp(gl - g_top), axis=-1, keepdims=True)
    e_lo = N_GROUPS + EXPERTS_PER_GROUP * g_idx
    el = jnp.where((lane >= e_lo) & (lane < e_lo + EXPERTS_PER_GROUP), logits, ninf)
    top1, i1 = first_max(el)
    top2, i2 = first_max(jnp.where(lane == i1, ninf, el))
    ratio = jnp.exp(top2 - top1)
    w1 = 1.0 / (1.0 + ratio)
    w2 = ratio * w1

    is1 = lane == i1
    is2 = lane == i2
    onehot = jnp.where(is1 | is2, 1.0, 0.0)
    before = jnp.dot(tri_ref[...], onehot.astype(BF16), preferred_element_type=F32) + run_ref[...]
    r1 = jnp.sum(jnp.where(is1, before, 0.0), axis=-1, keepdims=True)
    r2 = jnp.sum(jnp.where(is2, before, 0.0), axis=-1, keepdims=True)
    run_ref[...] = run_ref[...] + jnp.sum(onehot, axis=0, keepdims=True)
    cnt_ref[...] = run_ref[...]

    rec = jnp.zeros_like(logits)
    for slot, val in ((ROUTE_E1, i1 - N_GROUPS), (ROUTE_E2, i2 - N_GROUPS), (ROUTE_R1, r1), (ROUTE_R2, r2),
                      (ROUTE_W1, p_g * w1), (ROUTE_W2, p_g * w2)):
        rec = jnp.where(lane == slot, val, rec)
    route_ref[...] = rec


def _out_proj(o_gla, o_fox, h0, w_out, g2, w_router, b_router):
    t = h0.shape[0]
    row = lambda i: (i, 0)
    fixed = lambda i: (0, 0)
    return pl.pallas_call(
        _out_proj_kernel,
        grid=(t // ROW_TILE,),
        in_specs=[
            pl.BlockSpec((ROW_TILE, GLA_DV_TOT), row),
            pl.BlockSpec((ROW_TILE, FOX_W), row),
            pl.BlockSpec((ROW_TILE, D_MODEL), row),
            pl.BlockSpec((D_MODEL, D_MODEL), fixed, pipeline_mode=pl.Buffered(1)),
            pl.BlockSpec((1, D_MODEL), fixed),
            pl.BlockSpec((D_MODEL, LANES), fixed),
            pl.BlockSpec((1, LANES), fixed),
        ],
        out_specs=[
            pl.BlockSpec((ROW_TILE, D_MODEL), row),
            pl.BlockSpec((ROW_TILE, D_MODEL), row),
            pl.BlockSpec((ROW_TILE, LANES), row),
            pl.BlockSpec((1, LANES), fixed),
        ],
        out_shape=[
            jax.ShapeDtypeStruct((t, D_MODEL), F32),
            jax.ShapeDtypeStruct((t, D_MODEL), BF16),
            jax.ShapeDtypeStruct((t, LANES), F32),
            jax.ShapeDtypeStruct((1, LANES), F32),
        ],
        scratch_shapes=[pltpu.VMEM((ROW_TILE, ROW_TILE), BF16), pltpu.VMEM((1, LANES), F32)],
        compiler_params=_params(("arbitrary",), 48),
    )(o_gla, o_fox, h0, w_out, g2, w_router, b_router)


def _scatter_kernel(pos1_ref, pos2_ref, last_ref, has_ref, nv_ref, u2_ref, xs_ref, rows_ref, zero_ref, sems, zsem):
    i = pl.program_id(0)
    n_steps = pl.num_programs(0)
    tile_rows = EXPERT_TILE * ROW_SUBLANES
    n_tiles = xs_ref.shape[0] // tile_rows

    def zero_copy(start):
        start = pl.multiple_of(start * ROW_SUBLANES, tile_rows)
        return pltpu.make_async_copy(zero_ref, xs_ref.at[pl.ds(start, tile_rows), :], zsem)

    def for_each_zero_tile(action):
        for e in range(N_EXPERTS):
            @pl.when(has_ref[e] > 0)
            def _():
                action(zero_copy(last_ref[e]))

        def unused_tile(j, carry):
            action(zero_copy(j * EXPERT_TILE))
            return carry

        lax.fori_loop(nv_ref[0], n_tiles, unused_tile, 0)

    @pl.when(i == 0)
    def _():
        zero_ref[...] = jnp.zeros_like(zero_ref)
        for_each_zero_tile(lambda cp: cp.start())
        for_each_zero_tile(lambda cp: cp.wait())

    def row_copy(step, r, pos_ref):
        slot = step % 2
        src = pl.multiple_of(r * ROW_SUBLANES, ROW_SUBLANES)
        dst = pl.multiple_of(pos_ref[step * ROW_TILE + r] * ROW_SUBLANES, ROW_SUBLANES)
        return pltpu.make_async_copy(rows_ref.at[slot, pl.ds(src, ROW_SUBLANES), :],
                                     xs_ref.at[pl.ds(dst, ROW_SUBLANES), :], sems.at[slot])

    def for_each_row(step, action):
        def group(j, carry):
            for k in range(DMA_UNROLL):
                action(row_copy(step, j * DMA_UNROLL + k, pos1_ref))
                action(row_copy(step, j * DMA_UNROLL + k, pos2_ref))
            return carry

        lax.fori_loop(0, ROW_TILE // DMA_UNROLL, group, 0)

    _rows_store(rows_ref.at[i % 2], u2_ref[...].astype(F32), ROW_SUBLANES)
    for_each_row(i, lambda cp: cp.start())

    @pl.when(i > 0)
    def _():
        for_each_row(i - 1, lambda cp: cp.wait())

    @pl.when(i == n_steps - 1)
    def _():
        for_each_row(i, lambda cp: cp.wait())


def _scatter(pos1, pos2, last_row, has, n_valid, u2, n_rows):
    t = u2.shape[0]
    return pl.pallas_call(
        _scatter_kernel,
        grid_spec=pltpu.PrefetchScalarGridSpec(
            num_scalar_prefetch=5,
            grid=(t // ROW_TILE,),
            in_specs=[pl.BlockSpec((ROW_TILE, D_MODEL), lambda i, *_: (i, 0))],
            out_specs=pl.BlockSpec(memory_space=pl.ANY),
            scratch_shapes=[pltpu.VMEM((2, ROW_TILE * ROW_SUBLANES, LANES), F32),
                            pltpu.VMEM((EXPERT_TILE * ROW_SUBLANES, LANES), F32),
                            pltpu.SemaphoreType.DMA((2,)), pltpu.SemaphoreType.DMA],
        ),
        out_shape=jax.ShapeDtypeStruct((n_rows * ROW_SUBLANES, LANES), F32),
        compiler_params=_params(("arbitrary",), 40),
    )(pos1, pos2, last_row, has, n_valid, u2)


def _moe_kernel(te_ref, first_ref, next_ref, nv_ref, xs_ref, wg_hbm, wu_hbm, wd_hbm, y_ref,
                wg_stage, wu_stage, wd_stage, wgb_ref, wub_ref, wdb_ref, sems):
    i = pl.program_id(0)

    def weight_copies(expert):
        return (pltpu.make_async_copy(wg_hbm.at[expert], wg_stage, sems.at[0]),
                pltpu.make_async_copy(wu_hbm.at[expert], wu_stage, sems.at[1]),
                pltpu.make_async_copy(wd_hbm.at[expert], wd_stage, sems.at[2]))

    @pl.when(i >= nv_ref[0])
    def _():
        y_ref[...] = jnp.zeros_like(y_ref)

    @pl.when(i < nv_ref[0])
    def _():
        @pl.when(first_ref[i] == 1)
        def _():
            @pl.when(i == 0)
            def _():
                for cp in weight_copies(te_ref[i]):
                    cp.start()

            for cp in weight_copies(te_ref[i]):
                cp.wait()
            wgb_ref[...] = wg_stage[...].astype(BF16)
            wub_ref[...] = wu_stage[...].astype(BF16)
            wdb_ref[...] = wd_stage[...].astype(BF16)

            @pl.when(next_ref[i] >= 0)
            def _():
                for cp in weight_copies(next_ref[i]):
                    cp.start()

        x = _rows_load(xs_ref, EXPERT_TILE, ROW_SUBLANES).astype(BF16)
        hg = jnp.dot(x, wgb_ref[...], preferred_element_type=F32)
        hu = jnp.dot(x, wub_ref[...], preferred_element_type=F32)
        hm = (hg * _sigmoid(hg) * hu).astype(BF16)
        y = jnp.dot(hm, wdb_ref[...], preferred_element_type=F32)
        _rows_store(y_ref, y, ROW_SUBLANES)


def _moe(tile_expert, tile_first, tile_next, n_valid, xs, w_g, w_u, w_d):
    n_rows = xs.shape[0] // ROW_SUBLANES
    row = lambda i, *_: (i, 0)
    return pl.pallas_call(
        _moe_kernel,
        grid_spec=pltpu.PrefetchScalarGridSpec(
            num_scalar_prefetch=4,
            grid=(n_rows // EXPERT_TILE,),
            in_specs=[
                pl.BlockSpec((EXPERT_TILE * ROW_SUBLANES, LANES), row),
                pl.BlockSpec(memory_space=pl.ANY),
                pl.BlockSpec(memory_space=pl.ANY),
                pl.BlockSpec(memory_space=pl.ANY),
            ],
            out_specs=pl.BlockSpec((EXPERT_TILE * ROW_SUBLANES, LANES), row),
            scratch_shapes=[pltpu.VMEM((D_MODEL, D_EXPERT), F32), pltpu.VMEM((D_MODEL, D_EXPERT), F32),
                            pltpu.VMEM((D_EXPERT, D_MODEL), F32),
                            pltpu.VMEM((D_MODEL, D_EXPERT), BF16), pltpu.VMEM((D_MODEL, D_EXPERT), BF16),
                            pltpu.VMEM((D_EXPERT, D_MODEL), BF16),
                            pltpu.SemaphoreType.DMA((3,))],
        ),
        out_shape=jax.ShapeDtypeStruct((n_rows * ROW_SUBLANES, LANES), F32),
        compiler_params=_params(("arbitrary",), 48),
    )(tile_expert, tile_first, tile_next, n_valid, xs, w_g, w_u, w_d)


def _combine_kernel(pos1_ref, pos2_ref, h1_ref, route_ref, gf_ref, y_ref, out_ref, ya_ref, yb_ref, sems):
    i = pl.program_id(0)
    n_steps = pl.num_programs(0)

    def row_copy(tile, r, pos_ref, buf_ref):
        slot = tile % 2
        src = pl.multiple_of(pos_ref[tile * OUT_TILE + r] * ROW_SUBLANES, ROW_SUBLANES)
        dst = pl.multiple_of(r * ROW_SUBLANES, ROW_SUBLANES)
        return pltpu.make_async_copy(y_ref.at[pl.ds(src, ROW_SUBLANES), :],
                                     buf_ref.at[slot, pl.ds(dst, ROW_SUBLANES), :], sems.at[slot])

    def for_each_row(tile, action):
        def group(j, carry):
            for k in range(DMA_UNROLL):
                action(row_copy(tile, j * DMA_UNROLL + k, pos1_ref, ya_ref))
                action(row_copy(tile, j * DMA_UNROLL + k, pos2_ref, yb_ref))
            return carry

        lax.fori_loop(0, OUT_TILE // DMA_UNROLL, group, 0)

    @pl.when(i == 0)
    def _():
        for_each_row(i, lambda cp: cp.start())

    @pl.when(i + 1 < n_steps)
    def _():
        for_each_row(i + 1, lambda cp: cp.start())

    for_each_row(i, lambda cp: cp.wait())
    slot = i % 2
    rec = route_ref[...]
    ya = _rows_load(ya_ref.at[slot], OUT_TILE, ROW_SUBLANES)
    yb = _rows_load(yb_ref.at[slot], OUT_TILE, ROW_SUBLANES)
    hh = h1_ref[...] + rec[:, ROUTE_W1:ROUTE_W1 + 1] * ya + rec[:, ROUTE_W2:ROUTE_W2 + 1] * yb
    ms = jnp.mean(hh * hh, axis=-1, keepdims=True)
    out_ref[...] = hh * lax.rsqrt(ms + EPS) * gf_ref[...]


def _combine(pos1, pos2, h1, route, g_f, y):
    t = h1.shape[0]
    row = lambda i, *_: (i, 0)
    return pl.pallas_call(
        _combine_kernel,
        grid_spec=pltpu.PrefetchScalarGridSpec(
            num_scalar_prefetch=2,
            grid=(t // OUT_TILE,),
            in_specs=[
                pl.BlockSpec((OUT_TILE, D_MODEL), row),
                pl.BlockSpec((OUT_TILE, LANES), row),
                pl.BlockSpec((1, D_MODEL), lambda i, *_: (0, 0)),
                pl.BlockSpec(memory_space=pl.ANY),
            ],
            out_specs=pl.BlockSpec((OUT_TILE, D_MODEL), row),
            scratch_shapes=[pltpu.VMEM((2, OUT_TILE * ROW_SUBLANES, LANES), F32),
                            pltpu.VMEM((2, OUT_TILE * ROW_SUBLANES, LANES), F32),
                            pltpu.SemaphoreType.DMA((2,))],
        ),
        out_shape=jax.ShapeDtypeStruct((t, D_MODEL), F32),
        compiler_params=_params(("arbitrary",), 40),
    )(pos1, pos2, h1, route, g_f, y)


def _routing_tables(route, cnt, n_tiles):
    counts = cnt[0, N_GROUPS:N_GROUPS + N_EXPERTS].astype(jnp.int32)
    tiles = (counts + EXPERT_TILE - 1) // EXPERT_TILE
    tile_end = jnp.cumsum(tiles)
    row_start = (tile_end - tiles) * EXPERT_TILE
    n_valid = tile_end[-1]
    expert_ids = jnp.arange(N_EXPERTS, dtype=jnp.int32)

    def positions(expert_lane, rank_lane):
        expert = route[:, expert_lane].astype(jnp.int32)
        start = jnp.sum(jnp.where(expert[:, None] == expert_ids[None, :], row_start[None, :], 0), axis=1)
        return start + route[:, rank_lane].astype(jnp.int32)

    pos1 = positions(ROUTE_E1, ROUTE_R1)
    pos2 = positions(ROUTE_E2, ROUTE_R2)
    tile = jnp.minimum(jnp.arange(n_tiles, dtype=jnp.int32), n_valid - 1)
    tile_expert = jnp.sum(tile[:, None] >= tile_end[None, :], axis=-1).astype(jnp.int32)
    owner = tile_expert[:, None] == expert_ids[None, :]
    tile_first = (tile == jnp.sum(jnp.where(owner, (tile_end - tiles)[None, :], 0), axis=1)).astype(jnp.int32)
    later = (expert_ids[None, :] > expert_ids[:, None]) & (tiles[None, :] > 0)
    next_expert = jnp.min(jnp.where(later, expert_ids[None, :], N_EXPERTS), axis=1)
    next_expert = jnp.where(next_expert == N_EXPERTS, -1, next_expert)
    tile_next = jnp.sum(jnp.where(owner, next_expert[None, :], 0), axis=1).astype(jnp.int32)
    last_row = (tile_end - 1) * EXPERT_TILE
    return pos1, pos2, tile_expert, tile_first, tile_next, n_valid.reshape(1), last_row, tiles


def kernel(x, meta_tokens, norm1_g, w_in, b_fox_f, gla_w_gate2, gla_b_gate, gla_norm_g, fox_norm_g, w_out,
           norm2_g, w_router_group, b_router_group, w_router_expert, b_router_expert, w_exp_gate, w_exp_up,
           w_exp_down, norm_f_g):
    batch, seq, _ = x.shape
    assert batch == 1 and norm1_g.shape[0] == 1
    assert seq % FOX_TILE == 0 and seq % ROW_TILE == 0
    t = HEAD_ROWS + seq
    x2 = x[0]
    head = jnp.concatenate([jnp.zeros((PAD_FRONT, D_MODEL), F32), meta_tokens.astype(F32)], axis=0)

    assert w_in.shape == (1, D_MODEL, D_IN_PROJ) and PROJ_ALIGNED % PROJ_STAGE_COLS == 0
    proj, small = _in_proj(head, x2, norm1_g, w_in[0].T)

    negc = _fox_bias(small, b_fox_f[0].reshape(FOX_HEADS, 1))
    w2_pad = jnp.zeros((LANES, GLA_DK_TOT), F32).at[FOX_HEADS:FOX_HEADS + GLA_RANK].set(gla_w_gate2[0])
    o_gla = _gla(proj, small, w2_pad, gla_b_gate, gla_norm_g)
    o_fox = _fox(proj, negc.reshape(FOX_HEADS, 1, t), fox_norm_g)

    w_router = jnp.concatenate(
        [w_router_group[0], jnp.transpose(w_router_expert[0], (1, 0, 2)).reshape(D_MODEL, N_EXPERTS),
         jnp.zeros((D_MODEL, LANES - N_GROUPS - N_EXPERTS), F32)], axis=1).astype(BF16)
    b_router = jnp.concatenate([b_router_group[0], b_router_expert[0].reshape(-1),
                                jnp.zeros((LANES - N_GROUPS - N_EXPERTS,), F32)]).reshape(1, LANES)
    h1, u2, route, cnt = _out_proj(o_gla, o_fox, x2, w_out[0].astype(BF16), norm2_g, w_router, b_router)

    n_tiles = (2 * seq) // EXPERT_TILE + N_EXPERTS
    pos1, pos2, tile_expert, tile_first, tile_next, n_valid, last_row, tiles = _routing_tables(route, cnt, n_tiles)
    xs = _scatter(pos1, pos2, last_row, tiles, n_valid, u2, n_tiles * EXPERT_TILE)
    y = _moe(tile_expert, tile_first, tile_next, n_valid, xs,
             w_exp_gate[0].reshape(N_EXPERTS, D_MODEL, D_EXPERT),
             w_exp_up[0].reshape(N_EXPERTS, D_MODEL, D_EXPERT),
             w_exp_down[0].reshape(N_EXPERTS, D_EXPERT, D_MODEL))
    out = _combine(pos1, pos2, h1, route, norm_f_g.reshape(1, D_MODEL), y)
    return out.reshape(1, seq, D_MODEL)
```

```python
import jax
import jax.numpy as jnp
from jax import lax
from jax.experimental import pallas as pl
from jax.experimental.pallas import tpu as pltpu

D_MODEL = 2048
N_META = 16
GLA_HEADS = 4
GLA_DK = 128
GLA_DV = 256
GLA_DK_TOT = GLA_HEADS * GLA_DK
GLA_DV_TOT = GLA_HEADS * GLA_DV
GLA_RANK = 16
GLA_TAU = 16.0
GLA_CHUNK = 64
FOX_HEADS = 8
FOX_HD = 128
FOX_W = FOX_HEADS * FOX_HD
FOX_BLOCK = 128
PAD_FRONT = FOX_BLOCK - N_META
HEAD_ROWS = PAD_FRONT + N_META
N_GROUPS = 4
EXPERTS_PER_GROUP = 8
N_EXPERTS = N_GROUPS * EXPERTS_PER_GROUP
D_EXPERT = 512
EPS = 1e-6

LANES = 128
PROJ_ROWS = 2 * HEAD_ROWS
PROJ_SKIP = PROJ_ROWS - HEAD_ROWS
BIAS_BLOCK = 640
GLA_ROWS = 2 * GLA_CHUNK
FOX_TILE = 1024
FOX_ROWS = 128
FOX_KEYS = 1024
FOX_GROUP = 2
FOX_SKEW = 3
LOG2E = 1.4426950408889634
ROW_TILE = 512
EXPERT_TILE = 256
OUT_TILE = 256
MASK_VALUE = -1e30
PROJ_BIG = 3 * FOX_W + 2 * GLA_DK_TOT + 2 * GLA_DV_TOT
PROJ_FF = 3 * FOX_W
PROJ_GQ = PROJ_FF + FOX_HEADS
PROJ_GZ = PROJ_GQ + 2 * GLA_DK_TOT + 2 * GLA_DV_TOT
D_IN_PROJ = PROJ_GZ + GLA_RANK
PROJ_ALIGNED = (D_IN_PROJ // LANES) * LANES
PROJ_STAGE_COLS = 256
ROW_SUBLANES = D_MODEL // LANES
DMA_UNROLL = 8

F32 = jnp.float32
BF16 = jnp.bfloat16
NT_DIMS = (((1,), (1,)), ((), ()))
TN_DIMS = (((0,), (0,)), ((), ()))


def _log_sigmoid(x):
    return jnp.minimum(x, 0.0) - jnp.log(1.0 + jnp.exp(-jnp.abs(x)))


def _sigmoid(x):
    return 1.0 / (1.0 + jnp.exp(-x))


def _split3(x):
    hi = x.astype(BF16)
    rest = x - hi.astype(F32)
    mid = rest.astype(BF16)
    lo = (rest - mid.astype(F32)).astype(BF16)
    return hi, mid, lo


def _rows_load(ref, n_rows, n_chunks):
    return jnp.concatenate([ref[pl.ds(s, n_rows, stride=n_chunks), :] for s in range(n_chunks)], axis=1)


def _rows_store(ref, value, n_chunks):
    n_rows = value.shape[0]
    for s in range(n_chunks):
        ref[pl.ds(s, n_rows, stride=n_chunks), :] = value[:, s * LANES:(s + 1) * LANES]


def _params(semantics, vmem_mb):
    return pltpu.CompilerParams(dimension_semantics=semantics, vmem_limit_bytes=vmem_mb * 1024 * 1024)


def _in_proj_kernel(head_ref, x_ref, g_ref, wt_hbm, proj_ref, small_ref, w_ref, stage_ref, tail_ref, sems):
    n_chunks = PROJ_ALIGNED // PROJ_STAGE_COLS
    n_tail = D_IN_PROJ - PROJ_ALIGNED

    def chunk_copy(c):
        return pltpu.make_async_copy(wt_hbm.at[pl.ds(c * PROJ_STAGE_COLS, PROJ_STAGE_COLS), :],
                                     stage_ref.at[c % 2], sems.at[c % 2])

    def tail_copy():
        return pltpu.make_async_copy(wt_hbm.at[pl.ds(PROJ_ALIGNED, n_tail), :],
                                     tail_ref.at[pl.ds(0, n_tail), :], sems.at[2])

    @pl.when(pl.program_id(0) == 0)
    def _():
        tail_ref[...] = jnp.zeros_like(tail_ref)
        tail_copy().start()
        chunk_copy(0).start()
        for c in range(n_chunks):
            if c + 1 < n_chunks:
                chunk_copy(c + 1).start()
            chunk_copy(c).wait()
            w_ref[:, c * PROJ_STAGE_COLS:(c + 1) * PROJ_STAGE_COLS] = stage_ref[c % 2].T.astype(BF16)
        tail_copy().wait()
        w_ref[:, PROJ_ALIGNED:] = tail_ref[...].T.astype(BF16)

    x = jnp.where(pl.program_id(0) == 0, head_ref[...], x_ref[...])
    ms = jnp.mean(x * x, axis=-1, keepdims=True)
    xn = (x * lax.rsqrt(ms + EPS) * g_ref[...]).astype(BF16)
    p = jnp.dot(xn, w_ref[...], preferred_element_type=F32)
    lane = lax.broadcasted_iota(jnp.int32, (PROJ_ROWS, LANES), 1)
    small_ref[...] = jnp.where(lane < FOX_HEADS, p[:, PROJ_FF:PROJ_FF + LANES], p[:, PROJ_ALIGNED:])
    proj_ref[...] = jnp.concatenate([p[:, :PROJ_FF], p[:, PROJ_GQ:PROJ_GZ]], axis=1).astype(BF16)


def _in_proj(head, x, g1, w_t):
    t = PROJ_ROWS + x.shape[0]
    fixed = lambda i: (0, 0)
    return pl.pallas_call(
        _in_proj_kernel,
        grid=(t // PROJ_ROWS,),
        in_specs=[
            pl.BlockSpec((PROJ_ROWS, D_MODEL), fixed),
            pl.BlockSpec((PROJ_ROWS, D_MODEL), lambda i: (jnp.maximum(i - 1, 0), 0)),
            pl.BlockSpec((1, D_MODEL), fixed),
            pl.BlockSpec(memory_space=pl.ANY),
        ],
        out_specs=[
            pl.BlockSpec((PROJ_ROWS, PROJ_BIG), lambda i: (i, 0)),
            pl.BlockSpec((PROJ_ROWS, LANES), lambda i: (i, 0)),
        ],
        out_shape=[
            jax.ShapeDtypeStruct((t, PROJ_BIG), BF16),
            jax.ShapeDtypeStruct((t, LANES), F32),
        ],
        scratch_shapes=[pltpu.VMEM((D_MODEL, PROJ_ALIGNED + LANES), BF16),
                        pltpu.VMEM((2, PROJ_STAGE_COLS, D_MODEL), F32),
                        pltpu.VMEM((LANES, D_MODEL), F32),
                        pltpu.SemaphoreType.DMA((3,))],
        compiler_params=_params(("arbitrary",), 56),
    )(head, x, g1, w_t)


def _fox_bias_kernel(small_ref, bf_ref, negc_ref):
    t = negc_ref.shape[1]
    r = lax.broadcasted_iota(jnp.int32, (BIAS_BLOCK, BIAS_BLOCK), 0)
    c = lax.broadcasted_iota(jnp.int32, (BIAS_BLOCK, BIAS_BLOCK), 1)
    upper = jnp.where(r <= c, 1.0, 0.0).astype(BF16)
    lane = lax.broadcasted_iota(jnp.int32, (FOX_HEADS, BIAS_BLOCK), 1)

    def body(b, carry):
        off = pl.multiple_of(b * BIAS_BLOCK, BIAS_BLOCK)
        valid = (off + lane) >= PAD_FRONT
        f_logit = small_ref[pl.ds(PROJ_SKIP + off, BIAS_BLOCK), :].T[0:FOX_HEADS, :]
        lf = jnp.where(valid, _log_sigmoid(f_logit + bf_ref[...]), 0.0)
        cum = sum(jnp.dot(piece, upper, preferred_element_type=F32) for piece in _split3(lf)) + carry
        negc_ref[:, pl.ds(off, BIAS_BLOCK)] = jnp.where(valid, -LOG2E * cum, MASK_VALUE)
        return cum[:, BIAS_BLOCK - 1:BIAS_BLOCK]

    lax.fori_loop(0, t // BIAS_BLOCK, body, jnp.zeros((FOX_HEADS, 1), F32))


def _fox_bias(small, b_f):
    return pl.pallas_call(
        _fox_bias_kernel,
        out_shape=jax.ShapeDtypeStruct((FOX_HEADS, small.shape[0] - PROJ_SKIP), F32),
    )(small, b_f)


def _gla_kernel(q_ref, k_ref, v_ref, r_ref, small_ref, w2_ref, bg_ref, ng_ref, o_ref, st_ref):
    i = pl.program_id(0)

    @pl.when(i == 0)
    def _():
        st_ref[...] = jnp.zeros_like(st_ref)

    z_hi, z_mid, _ = _split3(small_ref[...])
    w_hi, w_mid, _ = _split3(w2_ref[...])
    gate_logit = (jnp.dot(z_hi, w_hi, preferred_element_type=F32) + jnp.dot(z_hi, w_mid, preferred_element_type=F32)
                  + jnp.dot(z_mid, w_hi, preferred_element_type=F32) + bg_ref[...])
    g = _log_sigmoid(gate_logit) * (1.0 / GLA_TAU)
    rowid = i * GLA_ROWS + lax.broadcasted_iota(jnp.int32, (GLA_ROWS, 1), 0)
    g = jnp.where(rowid >= PAD_FRONT, g, 0.0)

    ci = lax.broadcasted_iota(jnp.int32, (GLA_CHUNK, GLA_CHUNK), 0)
    cj = lax.broadcasted_iota(jnp.int32, (GLA_CHUNK, GLA_CHUNK), 1)
    causal = cj <= ci
    lower = jnp.where(causal, 1.0, 0.0).astype(BF16)
    scale = GLA_DK ** -0.5
    mid = GLA_CHUNK // 2

    for c in range(GLA_ROWS // GLA_CHUNK):
        rows = slice(c * GLA_CHUNK, (c + 1) * GLA_CHUNK)
        b = sum(jnp.dot(lower, piece, preferred_element_type=F32) for piece in _split3(g[rows]))
        b_mid = b[mid:mid + 1]
        b_last = b[GLA_CHUNK - 1:GLA_CHUNK]
        q = q_ref[rows, :].astype(F32) * scale
        k = k_ref[rows, :].astype(F32)
        q_intra = (q * jnp.exp(b - b_mid)).astype(BF16)
        k_intra = (k * jnp.exp(b_mid - b)).astype(BF16)
        q_inter = (q * jnp.exp(b)).astype(BF16)
        k_state = (k * jnp.exp(b_last - b)).astype(BF16)
        decay = jnp.exp(b_last)
        heads = range(GLA_HEADS)
        ks = [slice(h * GLA_DK, (h + 1) * GLA_DK) for h in heads]
        vs = [slice(h * GLA_DV, (h + 1) * GLA_DV) for h in heads]
        v = [v_ref[rows, vs[h]] for h in heads]
        st = [st_ref[h] for h in heads]
        a = [lax.dot_general(q_intra[:, ks[h]], k_intra[:, ks[h]], NT_DIMS, preferred_element_type=F32)
             for h in heads]
        o_inter = [lax.dot_general(q_inter[:, ks[h]], st[h].astype(BF16), NT_DIMS, preferred_element_type=F32)
                   for h in heads]
        u_t = [lax.dot_general(v[h], k_state[:, ks[h]], TN_DIMS, preferred_element_type=F32) for h in heads]
        for h in heads:
            st_ref[h] = decay[:, ks[h]] * st[h] + u_t[h]
        for h in heads:
            o = o_inter[h] + jnp.dot(jnp.where(causal, a[h], 0.0).astype(BF16), v[h], preferred_element_type=F32)
            ms = jnp.mean(o * o, axis=-1, keepdims=True)
            y = o * lax.rsqrt(ms + EPS) * ng_ref[...]
            r = r_ref[rows, vs[h]].astype(F32)
            o_ref[rows, vs[h]] = (y * (r * _sigmoid(r))).astype(BF16)


def _gla(proj, small, w2_pad, b_gate, norm_g):
    t = proj.shape[0] - PROJ_SKIP
    skip = PROJ_SKIP // GLA_ROWS
    q_blk = (3 * FOX_W) // GLA_DK_TOT
    v_blk = (3 * FOX_W + 2 * GLA_DK_TOT) // GLA_DV_TOT
    return pl.pallas_call(
        _gla_kernel,
        grid=(t // GLA_ROWS,),
        in_specs=[
            pl.BlockSpec((GLA_ROWS, GLA_DK_TOT), lambda i: (i + skip, q_blk)),
            pl.BlockSpec((GLA_ROWS, GLA_DK_TOT), lambda i: (i + skip, q_blk + 1)),
            pl.BlockSpec((GLA_ROWS, GLA_DV_TOT), lambda i: (i + skip, v_blk)),
            pl.BlockSpec((GLA_ROWS, GLA_DV_TOT), lambda i: (i + skip, v_blk + 1)),
            pl.BlockSpec((GLA_ROWS, LANES), lambda i: (i + skip, 0)),
            pl.BlockSpec((LANES, GLA_DK_TOT), lambda i: (0, 0)),
            pl.BlockSpec((1, GLA_DK_TOT), lambda i: (0, 0)),
            pl.BlockSpec((1, GLA_DV), lambda i: (0, 0)),
        ],
        out_specs=pl.BlockSpec((GLA_ROWS, GLA_DV_TOT), lambda i: (jnp.maximum(i - 1, 0), 0)),
        out_shape=jax.ShapeDtypeStruct((t - HEAD_ROWS, GLA_DV_TOT), BF16),
        scratch_shapes=[pltpu.VMEM((GLA_HEADS, GLA_DV, GLA_DK), F32)],
        compiler_params=_params(("arbitrary",), 32),
    )(proj, proj, proj, proj, small, w2_pad, b_gate, norm_g)


def _fox_kernel(q_ref, k_ref, v_ref, negc_ref, ng_ref, o_ref, qs_ref, va_ref):
    qi = pl.program_id(1)
    n_blocks = FOX_TILE // FOX_ROWS

    @pl.when(qi == 0)
    def _():
        lane = lax.broadcasted_iota(jnp.int32, (v_ref.shape[0], FOX_HD), 1)
        ones_col = jnp.where(lane == 0, 1.0, 0.0).astype(BF16)
        for hh in range(FOX_GROUP):
            va_ref[:, 2 * hh * FOX_HD:(2 * hh + 1) * FOX_HD] = v_ref[:, hh * FOX_HD:(hh + 1) * FOX_HD]
            va_ref[:, (2 * hh + 1) * FOX_HD:(2 * hh + 2) * FOX_HD] = ones_col

    units = [(hh, rb) for hh in range(FOX_GROUP) for rb in range(n_blocks)]
    q0 = pl.multiple_of(HEAD_ROWS + qi * FOX_TILE, FOX_ROWS)
    qs_ref[...] = (q_ref[pl.ds(PROJ_SKIP + q0, FOX_TILE), :].astype(F32) * (FOX_HD ** -0.5 * LOG2E)).astype(BF16)
    row = lax.broadcasted_iota(jnp.int32, (FOX_ROWS, FOX_ROWS), 0)
    col = lax.broadcasted_iota(jnp.int32, (FOX_ROWS, FOX_ROWS), 1)

    def run(state, steps):
        def scores(step):
            u, off, k0, k1, causal_tail = step
            hh, rb = units[u]
            rows = slice(rb * FOX_ROWS, (rb + 1) * FOX_ROWS)
            cols = slice(hh * FOX_HD, (hh + 1) * FOX_HD)
            s = lax.dot_general(qs_ref[rows, cols], k_ref[pl.ds(PROJ_SKIP + off + k0, k1 - k0), cols], NT_DIMS,
                                preferred_element_type=F32)
            s = s + negc_ref[hh, :, pl.ds(off + k0, k1 - k0)]
            if causal_tail:
                tail = jnp.where(col <= row, s[:, k1 - k0 - FOX_ROWS:], MASK_VALUE)
                s = tail if k1 - k0 == FOX_ROWS else jnp.concatenate([s[:, :k1 - k0 - FOX_ROWS], tail], axis=1)
            return s

        def update(step, s, state):
            u, off, k0, k1, _ = step
            hh, _ = units[u]
            m_prev, acc_prev = state[u]
            m_new = jnp.maximum(m_prev, jnp.max(s, axis=-1, keepdims=True))
            p = jnp.exp2(s - m_new).astype(BF16)
            acc_new = jnp.exp2(m_prev - m_new) * acc_prev + jnp.dot(
                p, va_ref[pl.ds(PROJ_SKIP + off + k0, k1 - k0), 2 * hh * FOX_HD:(2 * hh + 2) * FOX_HD],
                preferred_element_type=F32)
            state[u] = (m_new, acc_new)

        state = list(state)
        pending = [scores(st) for st in steps[:FOX_SKEW]]
        for j, st in enumerate(steps):
            if j + FOX_SKEW < len(steps):
                pending.append(scores(steps[j + FOX_SKEW]))
            update(st, pending[j], state)
            pending[j] = None
        return tuple(state)

    head_steps = [(u, 0, 0, HEAD_ROWS, False) for u in range(len(units))]

    def full_steps(off):
        return [(u, off, k0, k0 + FOX_KEYS, False)
                for k0 in range(0, FOX_TILE, FOX_KEYS) for u in range(len(units))]

    diag_steps = []
    for k0 in range(0, FOX_TILE, FOX_KEYS):
        for u, (_, rb) in enumerate(units):
            last = (rb + 1) * FOX_ROWS
            if last > k0:
                diag_steps.append((u, q0, k0, min(k0 + FOX_KEYS, last), last <= k0 + FOX_KEYS))

    state = tuple((jnp.full((FOX_ROWS, 1), MASK_VALUE, F32), jnp.zeros((FOX_ROWS, 2 * FOX_HD), F32))
                  for _ in units)
    state = run(state, head_steps)
    state = lax.fori_loop(
        0, qi, lambda kt, st: run(st, full_steps(pl.multiple_of(HEAD_ROWS + kt * FOX_TILE, FOX_ROWS))), state)
    state = run(state, diag_steps)
    for u, (hh, rb) in enumerate(units):
        _, acc = state[u]
        o = acc[:, :FOX_HD] / acc[:, FOX_HD:FOX_HD + 1]
        ms = jnp.mean(o * o, axis=-1, keepdims=True)
        o_ref[rb * FOX_ROWS:(rb + 1) * FOX_ROWS, hh * FOX_HD:(hh + 1) * FOX_HD] = (
            o * lax.rsqrt(ms + EPS) * ng_ref[...]).astype(BF16)


def _fox(proj, negc3, norm_g):
    rows = proj.shape[0]
    t = rows - PROJ_SKIP
    width = FOX_GROUP * FOX_HD
    k_blk = FOX_W // width
    return pl.pallas_call(
        _fox_kernel,
        grid=(FOX_HEADS // FOX_GROUP, (t - HEAD_ROWS) // FOX_TILE),
        in_specs=[
            pl.BlockSpec((rows, width), lambda g, i: (0, g)),
            pl.BlockSpec((rows, width), lambda g, i: (0, k_blk + g)),
            pl.BlockSpec((rows, width), lambda g, i: (0, 2 * k_blk + g)),
            pl.BlockSpec((FOX_GROUP, 1, t), lambda g, i: (g, 0, 0)),
            pl.BlockSpec((1, FOX_HD), lambda g, i: (0, 0)),
        ],
        out_specs=pl.BlockSpec((FOX_TILE, width), lambda g, i: (i, g)),
        out_shape=jax.ShapeDtypeStruct((t - HEAD_ROWS, FOX_W), BF16),
        scratch_shapes=[pltpu.VMEM((FOX_TILE, width), BF16), pltpu.VMEM((rows, 2 * width), BF16)],
        compiler_params=_params(("arbitrary", "arbitrary"), 56),
    )(proj, proj, proj, negc3, norm_g)


ROUTE_E1, ROUTE_E2, ROUTE_R1, ROUTE_R2, ROUTE_W1, ROUTE_W2 = range(6)


def _out_proj_kernel(og_ref, of_ref, h_ref, wo_ref, g2_ref, wr_ref, br_ref,
                     h1_ref, u2_ref, route_ref, cnt_ref, tri_ref, run_ref):
    i = pl.program_id(0)

    @pl.when(i == 0)
    def _():
        r = lax.broadcasted_iota(jnp.int32, (ROW_TILE, ROW_TILE), 0)
        c = lax.broadcasted_iota(jnp.int32, (ROW_TILE, ROW_TILE), 1)
        tri_ref[...] = jnp.where(c < r, 1.0, 0.0).astype(BF16)
        run_ref[...] = jnp.zeros_like(run_ref)

    h1 = (h_ref[...]
          + jnp.dot(og_ref[...], wo_ref[0:GLA_DV_TOT, :], preferred_element_type=F32)
          + jnp.dot(of_ref[...], wo_ref[GLA_DV_TOT:, :], preferred_element_type=F32))
    h1_ref[...] = h1
    ms = jnp.mean(h1 * h1, axis=-1, keepdims=True)
    u2 = (h1 * lax.rsqrt(ms + EPS) * g2_ref[...]).astype(BF16)
    u2_ref[...] = u2

    logits = jnp.dot(u2, wr_ref[...], preferred_element_type=F32) + br_ref[...]
    lane = lax.broadcasted_iota(jnp.int32, logits.shape, 1).astype(F32)
    ninf = -jnp.inf

    def first_max(vals):
        top = jnp.max(vals, axis=-1, keepdims=True)
        idx = jnp.min(jnp.where(vals == top, lane, float(LANES)), axis=-1, keepdims=True)
        return top, idx

    gl = jnp.where(lane < N_GROUPS, logits, ninf)
    g_top, g_idx = first_max(gl)
    p_g = 1.0 / jnp.sum(jnp.exp(gl - g_top), axis=-1, keepdims=True)
    e_lo = N_GROUPS + EXPERTS_PER_GROUP * g_idx
    el = jnp.where((lane >= e_lo) & (lane < e_lo + EXPERTS_PER_GROUP), logits, ninf)
    top1, i1 = first_max(el)
    top2, i2 = first_max(jnp.where(lane == i1, ninf, el))
    ratio = jnp.exp(top2 - top1)
    w1 = 1.0 / (1.0 + ratio)
    w2 = ratio * w1

    is1 = lane == i1
    is2 = lane == i2
    onehot = jnp.where(is1 | is2, 1.0, 0.0)
    before = jnp.dot(tri_ref[...], onehot.astype(BF16), preferred_element_type=F32) + run_ref[...]
    r1 = jnp.sum(jnp.where(is1, before, 0.0), axis=-1, keepdims=True)
    r2 = jnp.sum(jnp.where(is2, before, 0.0), axis=-1, keepdims=True)
    run_ref[...] = run_ref[...] + jnp.sum(onehot, axis=0, keepdims=True)
    cnt_ref[...] = run_ref[...]

    rec = jnp.zeros_like(logits)
    for slot, val in ((ROUTE_E1, i1 - N_GROUPS), (ROUTE_E2, i2 - N_GROUPS), (ROUTE_R1, r1), (ROUTE_R2, r2),
                      (ROUTE_W1, p_g * w1), (ROUTE_W2, p_g * w2)):
        rec = jnp.where(lane == slot, val, rec)
    route_ref[...] = rec


def _out_proj(o_gla, o_fox, h0, w_out, g2, w_router, b_router):
    t = h0.shape[0]
    row = lambda i: (i, 0)
    fixed = lambda i: (0, 0)
    return pl.pallas_call(
        _out_proj_kernel,
        grid=(t // ROW_TILE,),
        in_specs=[
            pl.BlockSpec((ROW_TILE, GLA_DV_TOT), row),
            pl.BlockSpec((ROW_TILE, FOX_W), row),
            pl.BlockSpec((ROW_TILE, D_MODEL), row),
            pl.BlockSpec((D_MODEL, D_MODEL), fixed, pipeline_mode=pl.Buffered(1)),
            pl.BlockSpec((1, D_MODEL), fixed),
            pl.BlockSpec((D_MODEL, LANES), fixed),
            pl.BlockSpec((1, LANES), fixed),
        ],
        out_specs=[
            pl.BlockSpec((ROW_TILE, D_MODEL), row),
            pl.BlockSpec((ROW_TILE, D_MODEL), row),
            pl.BlockSpec((ROW_TILE, LANES), row),
            pl.BlockSpec((1, LANES), fixed),
        ],
        out_shape=[
            jax.ShapeDtypeStruct((t, D_MODEL), F32),
            jax.ShapeDtypeStruct((t, D_MODEL), BF16),
            jax.ShapeDtypeStruct((t, LANES), F32),
            jax.ShapeDtypeStruct((1, LANES), F32),
        ],
        scratch_shapes=[pltpu.VMEM((ROW_TILE, ROW_TILE), BF16), pltpu.VMEM((1, LANES), F32)],
        compiler_params=_params(("arbitrary",), 48),
    )(o_gla, o_fox, h0, w_out, g2, w_router, b_router)


def _scatter_kernel(pos1_ref, pos2_ref, last_ref, has_ref, nv_ref, u2_ref, xs_ref, rows_ref, zero_ref, sems, zsem):
    i = pl.program_id(0)
    n_steps = pl.num_programs(0)
    tile_rows = EXPERT_TILE * ROW_SUBLANES
    n_tiles = xs_ref.shape[0] // tile_rows

    def zero_copy(start):
        start = pl.multiple_of(start * ROW_SUBLANES, tile_rows)
        return pltpu.make_async_copy(zero_ref, xs_ref.at[pl.ds(start, tile_rows), :], zsem)

    def for_each_zero_tile(action):
        for e in range(N_EXPERTS):
            @pl.when(has_ref[e] > 0)
            def _():
                action(zero_copy(last_ref[e]))

        def unused_tile(j, carry):
            action(zero_copy(j * EXPERT_TILE))
            return carry

        lax.fori_loop(nv_ref[0], n_tiles, unused_tile, 0)

    @pl.when(i == 0)
    def _():
        zero_ref[...] = jnp.zeros_like(zero_ref)
        for_each_zero_tile(lambda cp: cp.start())
        for_each_zero_tile(lambda cp: cp.wait())

    def row_copy(step, r, pos_ref):
        slot = step % 2
        src = pl.multiple_of(r * ROW_SUBLANES, ROW_SUBLANES)
        dst = pl.multiple_of(pos_ref[step * ROW_TILE + r] * ROW_SUBLANES, ROW_SUBLANES)
        return pltpu.make_async_copy(rows_ref.at[slot, pl.ds(src, ROW_SUBLANES), :],
                                     xs_ref.at[pl.ds(dst, ROW_SUBLANES), :], sems.at[slot])

    def for_each_row(step, action):
        def group(j, carry):
            for k in range(DMA_UNROLL):
                action(row_copy(step, j * DMA_UNROLL + k, pos1_ref))
                action(row_copy(step, j * DMA_UNROLL + k, pos2_ref))
            return carry

        lax.fori_loop(0, ROW_TILE // DMA_UNROLL, group, 0)

    _rows_store(rows_ref.at[i % 2], u2_ref[...].astype(F32), ROW_SUBLANES)
    for_each_row(i, lambda cp: cp.start())

    @pl.when(i > 0)
    def _():
        for_each_row(i - 1, lambda cp: cp.wait())

    @pl.when(i == n_steps - 1)
    def _():
        for_each_row(i, lambda cp: cp.wait())


def _scatter(pos1, pos2, last_row, has, n_valid, u2, n_rows):
    t = u2.shape[0]
    return pl.pallas_call(
        _scatter_kernel,
        grid_spec=pltpu.PrefetchScalarGridSpec(
            num_scalar_prefetch=5,
            grid=(t // ROW_TILE,),
            in_specs=[pl.BlockSpec((ROW_TILE, D_MODEL), lambda i, *_: (i, 0))],
            out_specs=pl.BlockSpec(memory_space=pl.ANY),
            scratch_shapes=[pltpu.VMEM((2, ROW_TILE * ROW_SUBLANES, LANES), F32),
                            pltpu.VMEM((EXPERT_TILE * ROW_SUBLANES, LANES), F32),
                            pltpu.SemaphoreType.DMA((2,)), pltpu.SemaphoreType.DMA],
        ),
        out_shape=jax.ShapeDtypeStruct((n_rows * ROW_SUBLANES, LANES), F32),
        compiler_params=_params(("arbitrary",), 40),
    )(pos1, pos2, last_row, has, n_valid, u2)


def _moe_kernel(te_ref, first_ref, next_ref, nv_ref, xs_ref, wg_hbm, wu_hbm, wd_hbm, y_ref,
                wg_stage, wu_stage, wd_stage, wgb_ref, wub_ref, wdb_ref, sems):
    i = pl.program_id(0)

    def weight_copies(expert):
        return (pltpu.make_async_copy(wg_hbm.at[expert], wg_stage, sems.at[0]),
                pltpu.make_async_copy(wu_hbm.at[expert], wu_stage, sems.at[1]),
                pltpu.make_async_copy(wd_hbm.at[expert], wd_stage, sems.at[2]))

    @pl.when(i >= nv_ref[0])
    def _():
        y_ref[...] = jnp.zeros_like(y_ref)

    @pl.when(i < nv_ref[0])
    def _():
        @pl.when(first_ref[i] == 1)
        def _():
            @pl.when(i == 0)
            def _():
                for cp in weight_copies(te_ref[i]):
                    cp.start()

            for cp in weight_copies(te_ref[i]):
                cp.wait()
            wgb_ref[...] = wg_stage[...].astype(BF16)
            wub_ref[...] = wu_stage[...].astype(BF16)
            wdb_ref[...] = wd_stage[...].astype(BF16)

            @pl.when(next_ref[i] >= 0)
            def _():
                for cp in weight_copies(next_ref[i]):
                    cp.start()

        x = _rows_load(xs_ref, EXPERT_TILE, ROW_SUBLANES).astype(BF16)
        hg = jnp.dot(x, wgb_ref[...], preferred_element_type=F32)
        hu = jnp.dot(x, wub_ref[...], preferred_element_type=F32)
        hm = (hg * _sigmoid(hg) * hu).astype(BF16)
        y = jnp.dot(hm, wdb_ref[...], preferred_element_type=F32)
        _rows_store(y_ref, y, ROW_SUBLANES)


def _moe(tile_expert, tile_first, tile_next, n_valid, xs, w_g, w_u, w_d):
    n_rows = xs.shape[0] // ROW_SUBLANES
    row = lambda i, *_: (i, 0)
    return pl.pallas_call(
        _moe_kernel,
        grid_spec=pltpu.PrefetchScalarGridSpec(
            num_scalar_prefetch=4,
            grid=(n_rows // EXPERT_TILE,),
            in_specs=[
                pl.BlockSpec((EXPERT_TILE * ROW_SUBLANES, LANES), row),
                pl.BlockSpec(memory_space=pl.ANY),
                pl.BlockSpec(memory_space=pl.ANY),
                pl.BlockSpec(memory_space=pl.ANY),
            ],
            out_specs=pl.BlockSpec((EXPERT_TILE * ROW_SUBLANES, LANES), row),
            scratch_shapes=[pltpu.VMEM((D_MODEL, D_EXPERT), F32), pltpu.VMEM((D_MODEL, D_EXPERT), F32),
                            pltpu.VMEM((D_EXPERT, D_MODEL), F32),
                            pltpu.VMEM((D_MODEL, D_EXPERT), BF16), pltpu.VMEM((D_MODEL, D_EXPERT), BF16),
                            pltpu.VMEM((D_EXPERT, D_MODEL), BF16),
                            pltpu.SemaphoreType.DMA((3,))],
        ),
        out_shape=jax.ShapeDtypeStruct((n_rows * ROW_SUBLANES, LANES), F32),
        compiler_params=_params(("arbitrary",), 48),
    )(tile_expert, tile_first, tile_next, n_valid, xs, w_g, w_u, w_d)


def _combine_kernel(pos1_ref, pos2_ref, h1_ref, route_ref, gf_ref, y_ref, out_ref, ya_ref, yb_ref, sems):
    i = pl.program_id(0)
    n_steps = pl.num_programs(0)

    def row_copy(tile, r, pos_ref, buf_ref):
        slot = tile % 2
        src = pl.multiple_of(pos_ref[tile * OUT_TILE + r] * ROW_SUBLANES, ROW_SUBLANES)
        dst = pl.multiple_of(r * ROW_SUBLANES, ROW_SUBLANES)
        return pltpu.make_async_copy(y_ref.at[pl.ds(src, ROW_SUBLANES), :],
                                     buf_ref.at[slot, pl.ds(dst, ROW_SUBLANES), :], sems.at[slot])

    def for_each_row(tile, action):
        def group(j, carry):
            for k in range(DMA_UNROLL):
                action(row_copy(tile, j * DMA_UNROLL + k, pos1_ref, ya_ref))
                action(row_copy(tile, j * DMA_UNROLL + k, pos2_ref, yb_ref))
            return carry

        lax.fori_loop(0, OUT_TILE // DMA_UNROLL, group, 0)

    @pl.when(i == 0)
    def _():
        for_each_row(i, lambda cp: cp.start())

    @pl.when(i + 1 < n_steps)
    def _():
        for_each_row(i + 1, lambda cp: cp.start())

    for_each_row(i, lambda cp: cp.wait())
    slot = i % 2
    rec = route_ref[...]
    ya = _rows_load(ya_ref.at[slot], OUT_TILE, ROW_SUBLANES)
    yb = _rows_load(yb_ref.at[slot], OUT_TILE, ROW_SUBLANES)
    hh = h1_ref[...] + rec[:, ROUTE_W1:ROUTE_W1 + 1] * ya + rec[:, ROUTE_W2:ROUTE_W2 + 1] * yb
    ms = jnp.mean(hh * hh, axis=-1, keepdims=True)
    out_ref[...] = hh * lax.rsqrt(ms + EPS) * gf_ref[...]


def _combine(pos1, pos2, h1, route, g_f, y):
    t = h1.shape[0]
    row = lambda i, *_: (i, 0)
    return pl.pallas_call(
        _combine_kernel,
        grid_spec=pltpu.PrefetchScalarGridSpec(
            num_scalar_prefetch=2,
            grid=(t // OUT_TILE,),
            in_specs=[
                pl.BlockSpec((OUT_TILE, D_MODEL), row),
                pl.BlockSpec((OUT_TILE, LANES), row),
                pl.BlockSpec((1, D_MODEL), lambda i, *_: (0, 0)),
                pl.BlockSpec(memory_space=pl.ANY),
            ],
            out_specs=pl.BlockSpec((OUT_TILE, D_MODEL), row),
            scratch_shapes=[pltpu.VMEM((2, OUT_TILE * ROW_SUBLANES, LANES), F32),
                            pltpu.VMEM((2, OUT_TILE * ROW_SUBLANES, LANES), F32),
                            pltpu.SemaphoreType.DMA((2,))],
        ),
        out_shape=jax.ShapeDtypeStruct((t, D_MODEL), F32),
        compiler_params=_params(("arbitrary",), 40),
    )(pos1, pos2, h1, route, g_f, y)


def _routing_tables(route, cnt, n_tiles):
    counts = cnt[0, N_GROUPS:N_GROUPS + N_EXPERTS].astype(jnp.int32)
    tiles = (counts + EXPERT_TILE - 1) // EXPERT_TILE
    tile_end = jnp.cumsum(tiles)
    row_start = (tile_end - tiles) * EXPERT_TILE
    n_valid = tile_end[-1]
    expert_ids = jnp.arange(N_EXPERTS, dtype=jnp.int32)

    def positions(expert_lane, rank_lane):
        expert = route[:, expert_lane].astype(jnp.int32)
        start = jnp.sum(jnp.where(expert[:, None] == expert_ids[None, :], row_start[None, :], 0), axis=1)
        return start + route[:, rank_lane].astype(jnp.int32)

    pos1 = positions(ROUTE_E1, ROUTE_R1)
    pos2 = positions(ROUTE_E2, ROUTE_R2)
    tile = jnp.minimum(jnp.arange(n_tiles, dtype=jnp.int32), n_valid - 1)
    tile_expert = jnp.sum(tile[:, None] >= tile_end[None, :], axis=-1).astype(jnp.int32)
    owner = tile_expert[:, None] == expert_ids[None, :]
    tile_first = (tile == jnp.sum(jnp.where(owner, (tile_end - tiles)[None, :], 0), axis=1)).astype(jnp.int32)
    later = (expert_ids[None, :] > expert_ids[:, None]) & (tiles[None, :] > 0)
    next_expert = jnp.min(jnp.where(later, expert_ids[None, :], N_EXPERTS), axis=1)
    next_expert = jnp.where(next_expert == N_EXPERTS, -1, next_expert)
    tile_next = jnp.sum(jnp.where(owner, next_expert[None, :], 0), axis=1).astype(jnp.int32)
    last_row = (tile_end - 1) * EXPERT_TILE
    return pos1, pos2, tile_expert, tile_first, tile_next, n_valid.reshape(1), last_row, tiles


def kernel(x, meta_tokens, norm1_g, w_in, b_fox_f, gla_w_gate2, gla_b_gate, gla_norm_g, fox_norm_g, w_out,
           norm2_g, w_router_group, b_router_group, w_router_expert, b_router_expert, w_exp_gate, w_exp_up,
           w_exp_down, norm_f_g):
    batch, seq, _ = x.shape
    assert batch == 1 and norm1_g.shape[0] == 1
    assert seq % FOX_TILE == 0 and seq % ROW_TILE == 0 and (HEAD_ROWS + seq) % BIAS_BLOCK == 0
    t = HEAD_ROWS + seq
    x2 = x[0]
    head = jnp.concatenate([jnp.zeros((PROJ_SKIP + PAD_FRONT, D_MODEL), F32), meta_tokens.astype(F32)], axis=0)

    assert w_in.shape == (1, D_MODEL, D_IN_PROJ) and PROJ_ALIGNED % PROJ_STAGE_COLS == 0
    proj, small = _in_proj(head, x2, norm1_g, w_in[0].T)

    negc = _fox_bias(small, b_fox_f[0].reshape(FOX_HEADS, 1))
    w2_pad = jnp.zeros((LANES, GLA_DK_TOT), F32).at[FOX_HEADS:FOX_HEADS + GLA_RANK].set(gla_w_gate2[0])
    o_gla = _gla(proj, small, w2_pad, gla_b_gate, gla_norm_g)
    o_fox = _fox(proj, negc.reshape(FOX_HEADS, 1, t), fox_norm_g)

    w_router = jnp.concatenate(
        [w_router_group[0], jnp.transpose(w_router_expert[0], (1, 0, 2)).reshape(D_MODEL, N_EXPERTS),
         jnp.zeros((D_MODEL, LANES - N_GROUPS - N_EXPERTS), F32)], axis=1).astype(BF16)
    b_router = jnp.concatenate([b_router_group[0], b_router_expert[0].reshape(-1),
                                jnp.zeros((LANES - N_GROUPS - N_EXPERTS,), F32)]).reshape(1, LANES)
    h1, u2, route, cnt = _out_proj(o_gla, o_fox, x2, w_out[0].astype(BF16), norm2_g, w_router, b_router)

    n_tiles = (2 * seq) // EXPERT_TILE + N_EXPERTS
    pos1, pos2, tile_expert, tile_first, tile_next, n_valid, last_row, tiles = _routing_tables(route, cnt, n_tiles)
    xs = _scatter(pos1, pos2, last_row, tiles, n_valid, u2, n_tiles * EXPERT_TILE)
    y = _moe(tile_expert, tile_first, tile_next, n_valid, xs,
             w_exp_gate[0].reshape(N_EXPERTS, D_MODEL, D_EXPERT),
             w_exp_up[0].reshape(N_EXPERTS, D_MODEL, D_EXPERT),
             w_exp_down[0].reshape(N_EXPERTS, D_EXPERT, D_MODEL))
    out = _combine(pos1, pos2, h1, route, norm_f_g.reshape(1, D_MODEL), y)
    return out.reshape(1, seq, D_MODEL)
```

```python
import jax
import jax.numpy as jnp
from jax import lax
from jax.experimental import pallas as pl
from jax.experimental.pallas import tpu as pltpu

D_MODEL = 2048
N_META = 16
GLA_HEADS = 4
GLA_DK = 128
GLA_DV = 256
GLA_DK_TOT = GLA_HEADS * GLA_DK
GLA_DV_TOT = GLA_HEADS * GLA_DV
GLA_RANK = 16
GLA_TAU = 16.0
GLA_CHUNK = 64
FOX_HEADS = 8
FOX_HD = 128
FOX_W = FOX_HEADS * FOX_HD
FOX_BLOCK = 128
PAD_FRONT = FOX_BLOCK - N_META
HEAD_ROWS = PAD_FRONT + N_META
N_GROUPS = 4
EXPERTS_PER_GROUP = 8
N_EXPERTS = N_GROUPS * EXPERTS_PER_GROUP
D_EXPERT = 512
EPS = 1e-6

LANES = 128
PROJ_ROWS = 2 * HEAD_ROWS
PROJ_SKIP = PROJ_ROWS - HEAD_ROWS
BIAS_BLOCK = 640
GLA_ROWS = 2 * GLA_CHUNK
FOX_TILE = 1024
FOX_ROWS = 128
FOX_KEYS = 1024
FOX_GROUP = 2
FOX_SKEW = 3
LOG2E = 1.4426950408889634
ROW_TILE = 512
EXPERT_TILE = 256
OUT_TILE = 256
MASK_VALUE = -1e30
PROJ_BIG = 3 * FOX_W + 2 * GLA_DK_TOT + 2 * GLA_DV_TOT
PROJ_FF = 3 * FOX_W
PROJ_GQ = PROJ_FF + FOX_HEADS
PROJ_GZ = PROJ_GQ + 2 * GLA_DK_TOT + 2 * GLA_DV_TOT
D_IN_PROJ = PROJ_GZ + GLA_RANK
PROJ_ALIGNED = (D_IN_PROJ // LANES) * LANES
PROJ_STAGE_COLS = 256
ROW_SUBLANES = D_MODEL // LANES
DMA_UNROLL = 8

F32 = jnp.float32
BF16 = jnp.bfloat16
NT_DIMS = (((1,), (1,)), ((), ()))
TN_DIMS = (((0,), (0,)), ((), ()))


def _log_sigmoid(x):
    return jnp.minimum(x, 0.0) - jnp.log(1.0 + jnp.exp(-jnp.abs(x)))


def _sigmoid(x):
    return 1.0 / (1.0 + jnp.exp(-x))


def _split3(x):
    hi = x.astype(BF16)
    rest = x - hi.astype(F32)
    mid = rest.astype(BF16)
    lo = (rest - mid.astype(F32)).astype(BF16)
    return hi, mid, lo


def _rows_load(ref, n_rows, n_chunks):
    return jnp.concatenate([ref[pl.ds(s, n_rows, stride=n_chunks), :] for s in range(n_chunks)], axis=1)


def _rows_store(ref, value, n_chunks):
    n_rows = value.shape[0]
    for s in range(n_chunks):
        ref[pl.ds(s, n_rows, stride=n_chunks), :] = value[:, s * LANES:(s + 1) * LANES]


def _params(semantics, vmem_mb):
    return pltpu.CompilerParams(dimension_semantics=semantics, vmem_limit_bytes=vmem_mb * 1024 * 1024)


def _in_proj_kernel(head_ref, x_ref, g_ref, wt_hbm, proj_ref, small_ref, w_ref, stage_ref, tail_ref, sems):
    n_chunks = PROJ_ALIGNED // PROJ_STAGE_COLS
    n_tail = D_IN_PROJ - PROJ_ALIGNED

    def chunk_copy(c):
        return pltpu.make_async_copy(wt_hbm.at[pl.ds(c * PROJ_STAGE_COLS, PROJ_STAGE_COLS), :],
                                     stage_ref.at[c % 2], sems.at[c % 2])

    def tail_copy():
        return pltpu.make_async_copy(wt_hbm.at[pl.ds(PROJ_ALIGNED, n_tail), :],
                                     tail_ref.at[pl.ds(0, n_tail), :], sems.at[2])

    @pl.when(pl.program_id(0) == 0)
    def _():
        tail_ref[...] = jnp.zeros_like(tail_ref)
        tail_copy().start()
        chunk_copy(0).start()
        for c in range(n_chunks):
            if c + 1 < n_chunks:
                chunk_copy(c + 1).start()
            chunk_copy(c).wait()
            w_ref[:, c * PROJ_STAGE_COLS:(c + 1) * PROJ_STAGE_COLS] = stage_ref[c % 2].T.astype(BF16)
        tail_copy().wait()
        w_ref[:, PROJ_ALIGNED:] = tail_ref[...].T.astype(BF16)

    x = jnp.where(pl.program_id(0) == 0, head_ref[...], x_ref[...])
    ms = jnp.mean(x * x, axis=-1, keepdims=True)
    xn = (x * lax.rsqrt(ms + EPS) * g_ref[...]).astype(BF16)
    p = jnp.dot(xn, w_ref[...], preferred_element_type=F32)
    lane = lax.broadcasted_iota(jnp.int32, (PROJ_ROWS, LANES), 1)
    small_ref[...] = jnp.where(lane < FOX_HEADS, p[:, PROJ_FF:PROJ_FF + LANES], p[:, PROJ_ALIGNED:])
    proj_ref[...] = jnp.concatenate([p[:, :PROJ_FF], p[:, PROJ_GQ:PROJ_GZ]], axis=1).astype(BF16)


def _in_proj(head, x, g1, w_t):
    t = PROJ_ROWS + x.shape[0]
    fixed = lambda i: (0, 0)
    return pl.pallas_call(
        _in_proj_kernel,
        grid=(t // PROJ_ROWS,),
        in_specs=[
            pl.BlockSpec((PROJ_ROWS, D_MODEL), fixed),
            pl.BlockSpec((PROJ_ROWS, D_MODEL), lambda i: (jnp.maximum(i - 1, 0), 0)),
            pl.BlockSpec((1, D_MODEL), fixed),
            pl.BlockSpec(memory_space=pl.ANY),
        ],
        out_specs=[
            pl.BlockSpec((PROJ_ROWS, PROJ_BIG), lambda i: (i, 0)),
            pl.BlockSpec((PROJ_ROWS, LANES), lambda i: (i, 0)),
        ],
        out_shape=[
            jax.ShapeDtypeStruct((t, PROJ_BIG), BF16),
            jax.ShapeDtypeStruct((t, LANES), F32),
        ],
        scratch_shapes=[pltpu.VMEM((D_MODEL, PROJ_ALIGNED + LANES), BF16),
                        pltpu.VMEM((2, PROJ_STAGE_COLS, D_MODEL), F32),
                        pltpu.VMEM((LANES, D_MODEL), F32),
                        pltpu.SemaphoreType.DMA((3,))],
        compiler_params=_params(("arbitrary",), 56),
    )(head, x, g1, w_t)


def _fox_bias_kernel(small_ref, bf_ref, negc_ref):
    t = negc_ref.shape[1]
    r = lax.broadcasted_iota(jnp.int32, (BIAS_BLOCK, BIAS_BLOCK), 0)
    c = lax.broadcasted_iota(jnp.int32, (BIAS_BLOCK, BIAS_BLOCK), 1)
    upper = jnp.where(r <= c, 1.0, 0.0).astype(BF16)
    lane = lax.broadcasted_iota(jnp.int32, (FOX_HEADS, BIAS_BLOCK), 1)

    def body(b, carry):
        off = pl.multiple_of(b * BIAS_BLOCK, BIAS_BLOCK)
        valid = (off + lane) >= PAD_FRONT
        f_logit = small_ref[pl.ds(PROJ_SKIP + off, BIAS_BLOCK), :].T[0:FOX_HEADS, :]
        lf = jnp.where(valid, _log_sigmoid(f_logit + bf_ref[...]), 0.0)
        cum = sum(jnp.dot(piece, upper, preferred_element_type=F32) for piece in _split3(lf)) + carry
        negc_ref[:, pl.ds(off, BIAS_BLOCK)] = jnp.where(valid, -LOG2E * cum, MASK_VALUE)
        return cum[:, BIAS_BLOCK - 1:BIAS_BLOCK]

    lax.fori_loop(0, t // BIAS_BLOCK, body, jnp.zeros((FOX_HEADS, 1), F32))


def _fox_bias(small, b_f):
    return pl.pallas_call(
        _fox_bias_kernel,
        out_shape=jax.ShapeDtypeStruct((FOX_HEADS, small.shape[0] - PROJ_SKIP), F32),
    )(small, b_f)


def _gla_kernel(q_ref, k_ref, v_ref, r_ref, small_ref, w2_ref, bg_ref, ng_ref, o_ref, st_ref):
    i = pl.program_id(0)

    @pl.when(i == 0)
    def _():
        st_ref[...] = jnp.zeros_like(st_ref)

    z_hi, z_mid, _ = _split3(small_ref[...])
    w_hi, w_mid, _ = _split3(w2_ref[...])
    gate_logit = (jnp.dot(z_hi, w_hi, preferred_element_type=F32) + jnp.dot(z_hi, w_mid, preferred_element_type=F32)
                  + jnp.dot(z_mid, w_hi, preferred_element_type=F32) + bg_ref[...])
    g = _log_sigmoid(gate_logit) * (1.0 / GLA_TAU)
    rowid = i * GLA_ROWS + lax.broadcasted_iota(jnp.int32, (GLA_ROWS, 1), 0)
    g = jnp.where(rowid >= PAD_FRONT, g, 0.0)

    ci = lax.broadcasted_iota(jnp.int32, (GLA_CHUNK, GLA_CHUNK), 0)
    cj = lax.broadcasted_iota(jnp.int32, (GLA_CHUNK, GLA_CHUNK), 1)
    causal = cj <= ci
    lower = jnp.where(causal, 1.0, 0.0).astype(BF16)
    scale = GLA_DK ** -0.5
    mid = GLA_CHUNK // 2

    for c in range(GLA_ROWS // GLA_CHUNK):
        rows = slice(c * GLA_CHUNK, (c + 1) * GLA_CHUNK)
        b = sum(jnp.dot(lower, piece, preferred_element_type=F32) for piece in _split3(g[rows]))
        b_mid = b[mid:mid + 1]
        b_last = b[GLA_CHUNK - 1:GLA_CHUNK]
        q = q_ref[rows, :].astype(F32) * scale
        k = k_ref[rows, :].astype(F32)
        q_intra = (q * jnp.exp(b - b_mid)).astype(BF16)
        k_intra = (k * jnp.exp(b_mid - b)).astype(BF16)
        q_inter = (q * jnp.exp(b)).astype(BF16)
        k_state = (k * jnp.exp(b_last - b)).astype(BF16)
        decay = jnp.exp(b_last)
        heads = range(GLA_HEADS)
        ks = [slice(h * GLA_DK, (h + 1) * GLA_DK) for h in heads]
        vs = [slice(h * GLA_DV, (h + 1) * GLA_DV) for h in heads]
        v = [v_ref[rows, vs[h]] for h in heads]
        st = [st_ref[h] for h in heads]
        a = [lax.dot_general(q_intra[:, ks[h]], k_intra[:, ks[h]], NT_DIMS, preferred_element_type=F32)
             for h in heads]
        o_inter = [lax.dot_general(q_inter[:, ks[h]], st[h].astype(BF16), NT_DIMS, preferred_element_type=F32)
                   for h in heads]
        u_t = [lax.dot_general(v[h], k_state[:, ks[h]], TN_DIMS, preferred_element_type=F32) for h in heads]
        for h in heads:
            st_ref[h] = decay[:, ks[h]] * st[h] + u_t[h]
        for h in heads:
            o = o_inter[h] + jnp.dot(jnp.where(causal, a[h], 0.0).astype(BF16), v[h], preferred_element_type=F32)
            ms = jnp.mean(o * o, axis=-1, keepdims=True)
            y = o * lax.rsqrt(ms + EPS) * ng_ref[...]
            r = r_ref[rows, vs[h]].astype(F32)
            o_ref[rows, vs[h]] = (y * (r * _sigmoid(r))).astype(BF16)


def _gla(proj, small, w2_pad, b_gate, norm_g):
    t = proj.shape[0] - PROJ_SKIP
    skip = PROJ_SKIP // GLA_ROWS
    q_blk = (3 * FOX_W) // GLA_DK_TOT
    v_blk = (3 * FOX_W + 2 * GLA_DK_TOT) // GLA_DV_TOT
    return pl.pallas_call(
        _gla_kernel,
        grid=(t // GLA_ROWS,),
        in_specs=[
            pl.BlockSpec((GLA_ROWS, GLA_DK_TOT), lambda i: (i + skip, q_blk)),
            pl.BlockSpec((GLA_ROWS, GLA_DK_TOT), lambda i: (i + skip, q_blk + 1)),
            pl.BlockSpec((GLA_ROWS, GLA_DV_TOT), lambda i: (i + skip, v_blk)),
            pl.BlockSpec((GLA_ROWS, GLA_DV_TOT), lambda i: (i + skip, v_blk + 1)),
            pl.BlockSpec((GLA_ROWS, LANES), lambda i: (i + skip, 0)),
            pl.BlockSpec((LANES, GLA_DK_TOT), lambda i: (0, 0)),
            pl.BlockSpec((1, GLA_DK_TOT), lambda i: (0, 0)),
            pl.BlockSpec((1, GLA_DV), lambda i: (0, 0)),
        ],
        out_specs=pl.BlockSpec((GLA_ROWS, GLA_DV_TOT), lambda i: (jnp.maximum(i - 1, 0), 0)),
        out_shape=jax.ShapeDtypeStruct((t - HEAD_ROWS, GLA_DV_TOT), BF16),
        scratch_shapes=[pltpu.VMEM((GLA_HEADS, GLA_DV, GLA_DK), F32)],
        compiler_params=_params(("arbitrary",), 32),
    )(proj, proj, proj, proj, small, w2_pad, b_gate, norm_g)


def _fox_kernel(q_ref, k_ref, v_ref, negc_ref, ng_ref, o_ref, qs_ref, va_ref):
    qi = pl.program_id(1)
    n_blocks = FOX_TILE // FOX_ROWS

    @pl.when(qi == 0)
    def _():
        lane = lax.broadcasted_iota(jnp.int32, (v_ref.shape[0], FOX_HD), 1)
        ones_col = jnp.where(lane == 0, 1.0, 0.0).astype(BF16)
        for hh in range(FOX_GROUP):
            va_ref[:, 2 * hh * FOX_HD:(2 * hh + 1) * FOX_HD] = v_ref[:, hh * FOX_HD:(hh + 1) * FOX_HD]
            va_ref[:, (2 * hh + 1) * FOX_HD:(2 * hh + 2) * FOX_HD] = ones_col

    units = [(hh, rb) for hh in range(FOX_GROUP) for rb in range(n_blocks)]
    q0 = pl.multiple_of(HEAD_ROWS + qi * FOX_TILE, FOX_ROWS)
    qs_ref[...] = (q_ref[pl.ds(PROJ_SKIP + q0, FOX_TILE), :].astype(F32) * (FOX_HD ** -0.5 * LOG2E)).astype(BF16)
    row = lax.broadcasted_iota(jnp.int32, (FOX_ROWS, FOX_ROWS), 0)
    col = lax.broadcasted_iota(jnp.int32, (FOX_ROWS, FOX_ROWS), 1)

    def run(state, steps):
        def scores(step):
            u, off, k0, k1, causal_tail = step
            hh, rb = units[u]
            rows = slice(rb * FOX_ROWS, (rb + 1) * FOX_ROWS)
            cols = slice(hh * FOX_HD, (hh + 1) * FOX_HD)
            s = lax.dot_general(qs_ref[rows, cols], k_ref[pl.ds(PROJ_SKIP + off + k0, k1 - k0), cols], NT_DIMS,
                                preferred_element_type=F32)
            s = s + negc_ref[hh, :, pl.ds(off + k0, k1 - k0)]
            if causal_tail:
                tail = jnp.where(col <= row, s[:, k1 - k0 - FOX_ROWS:], MASK_VALUE)
                s = tail if k1 - k0 == FOX_ROWS else jnp.concatenate([s[:, :k1 - k0 - FOX_ROWS], tail], axis=1)
            return s

        def update(step, s, state):
            u, off, k0, k1, _ = step
            hh, _ = units[u]
            m_prev, acc_prev = state[u]
            m_new = jnp.maximum(m_prev, jnp.max(s, axis=-1, keepdims=True))
            p = jnp.exp2(s - m_new).astype(BF16)
            acc_new = jnp.exp2(m_prev - m_new) * acc_prev + jnp.dot(
                p, va_ref[pl.ds(PROJ_SKIP + off + k0, k1 - k0), 2 * hh * FOX_HD:(2 * hh + 2) * FOX_HD],
                preferred_element_type=F32)
            state[u] = (m_new, acc_new)

        state = list(state)
        pending = [scores(st) for st in steps[:FOX_SKEW]]
        for j, st in enumerate(steps):
            if j + FOX_SKEW < len(steps):
                pending.append(scores(steps[j + FOX_SKEW]))
            update(st, pending[j], state)
            pending[j] = None
        return tuple(state)

    head_steps = [(u, 0, 0, HEAD_ROWS, False) for u in range(len(units))]

    def full_steps(off):
        return [(u, off, k0, k0 + FOX_KEYS, False)
                for k0 in range(0, FOX_TILE, FOX_KEYS) for u in range(len(units))]

    diag_steps = []
    for k0 in range(0, FOX_TILE, FOX_KEYS):
        for u, (_, rb) in enumerate(units):
            last = (rb + 1) * FOX_ROWS
            if last > k0:
                diag_steps.append((u, q0, k0, min(k0 + FOX_KEYS, last), last <= k0 + FOX_KEYS))

    state = tuple((jnp.full((FOX_ROWS, 1), MASK_VALUE, F32), jnp.zeros((FOX_ROWS, 2 * FOX_HD), F32))
                  for _ in units)
    state = run(state, head_steps)
    state = lax.fori_loop(
        0, qi, lambda kt, st: run(st, full_steps(pl.multiple_of(HEAD_ROWS + kt * FOX_TILE, FOX_ROWS))), state)
    state = run(state, diag_steps)
    for u, (hh, rb) in enumerate(units):
        _, acc = state[u]
        o = acc[:, :FOX_HD] / acc[:, FOX_HD:FOX_HD + 1]
        ms = jnp.mean(o * o, axis=-1, keepdims=True)
        o_ref[rb * FOX_ROWS:(rb + 1) * FOX_ROWS, hh * FOX_HD:(hh + 1) * FOX_HD] = (
            o * lax.rsqrt(ms + EPS) * ng_ref[...]).astype(BF16)


def _fox(proj, negc3, norm_g):
    rows = proj.shape[0]
    t = rows - PROJ_SKIP
    width = FOX_GROUP * FOX_HD
    k_blk = FOX_W // width
    return pl.pallas_call(
        _fox_kernel,
        grid=(FOX_HEADS // FOX_GROUP, (t - HEAD_ROWS) // FOX_TILE),
        in_specs=[
            pl.BlockSpec((rows, width), lambda g, i: (0, g)),
            pl.BlockSpec((rows, width), lambda g, i: (0, k_blk + g)),
            pl.BlockSpec((rows, width), lambda g, i: (0, 2 * k_blk + g)),
            pl.BlockSpec((FOX_GROUP, 1, t), lambda g, i: (g, 0, 0)),
            pl.BlockSpec((1, FOX_HD), lambda g, i: (0, 0)),
        ],
        out_specs=pl.BlockSpec((FOX_TILE, width), lambda g, i: (i, g)),
        out_shape=jax.ShapeDtypeStruct((t - HEAD_ROWS, FOX_W), BF16),
        scratch_shapes=[pltpu.VMEM((FOX_TILE, width), BF16), pltpu.VMEM((rows, 2 * width), BF16)],
        compiler_params=_params(("arbitrary", "arbitrary"), 56),
    )(proj, proj, proj, negc3, norm_g)


ROUTE_E1, ROUTE_E2, ROUTE_R1, ROUTE_R2, ROUTE_W1, ROUTE_W2 = range(6)


def _out_proj_kernel(og_ref, of_ref, h_ref, wo_ref, g2_ref, wr_ref, br_ref,
                     h1_ref, u2_ref, route_ref, cnt_ref, tri_ref, run_ref):
    i = pl.program_id(0)
    half = ROW_TILE // 2

    @pl.when(i == 0)
    def _():
        r = lax.broadcasted_iota(jnp.int32, (half, half), 0)
        c = lax.broadcasted_iota(jnp.int32, (half, half), 1)
        tri_ref[...] = jnp.where(c < r, 1.0, 0.0).astype(BF16)
        run_ref[...] = jnp.zeros_like(run_ref)

    lane = lax.broadcasted_iota(jnp.int32, (half, LANES), 1).astype(F32)
    ninf = -jnp.inf

    def first_max(vals):
        top = jnp.max(vals, axis=-1, keepdims=True)
        idx = jnp.min(jnp.where(vals == top, lane, float(LANES)), axis=-1, keepdims=True)
        return top, idx

    halves = [slice(0, half), slice(half, ROW_TILE)]
    h1 = [h_ref[rows, :]
          + jnp.dot(og_ref[rows, :], wo_ref[0:GLA_DV_TOT, :], preferred_element_type=F32)
          + jnp.dot(of_ref[rows, :], wo_ref[GLA_DV_TOT:, :], preferred_element_type=F32) for rows in halves]
    running = run_ref[...]
    for rows, h1_half in zip(halves, h1):
        h1_ref[rows, :] = h1_half
        ms = jnp.mean(h1_half * h1_half, axis=-1, keepdims=True)
        u2 = (h1_half * lax.rsqrt(ms + EPS) * g2_ref[...]).astype(BF16)
        u2_ref[rows, :] = u2

        logits = jnp.dot(u2, wr_ref[...], preferred_element_type=F32) + br_ref[...]
        gl = jnp.where(lane < N_GROUPS, logits, ninf)
        g_top, g_idx = first_max(gl)
        p_g = 1.0 / jnp.sum(jnp.exp(gl - g_top), axis=-1, keepdims=True)
        e_lo = N_GROUPS + EXPERTS_PER_GROUP * g_idx
        el = jnp.where((lane >= e_lo) & (lane < e_lo + EXPERTS_PER_GROUP), logits, ninf)
        top1, i1 = first_max(el)
        top2, i2 = first_max(jnp.where(lane == i1, ninf, el))
        ratio = jnp.exp(top2 - top1)
        w1 = 1.0 / (1.0 + ratio)
        w2 = ratio * w1

        is1 = lane == i1
        is2 = lane == i2
        onehot = jnp.where(is1 | is2, 1.0, 0.0)
        before = jnp.dot(tri_ref[...], onehot.astype(BF16), preferred_element_type=F32) + running
        r1 = jnp.sum(jnp.where(is1, before, 0.0), axis=-1, keepdims=True)
        r2 = jnp.sum(jnp.where(is2, before, 0.0), axis=-1, keepdims=True)
        running = running + jnp.sum(onehot, axis=0, keepdims=True)

        rec = jnp.zeros_like(logits)
        for slot, val in ((ROUTE_E1, i1 - N_GROUPS), (ROUTE_E2, i2 - N_GROUPS), (ROUTE_R1, r1), (ROUTE_R2, r2),
                          (ROUTE_W1, p_g * w1), (ROUTE_W2, p_g * w2)):
            rec = jnp.where(lane == slot, val, rec)
        route_ref[rows, :] = rec
    run_ref[...] = running
    cnt_ref[...] = running


def _out_proj(o_gla, o_fox, h0, w_out, g2, w_router, b_router):
    t = h0.shape[0]
    row = lambda i: (i, 0)
    fixed = lambda i: (0, 0)
    return pl.pallas_call(
        _out_proj_kernel,
        grid=(t // ROW_TILE,),
        in_specs=[
            pl.BlockSpec((ROW_TILE, GLA_DV_TOT), row),
            pl.BlockSpec((ROW_TILE, FOX_W), row),
            pl.BlockSpec((ROW_TILE, D_MODEL), row),
            pl.BlockSpec((D_MODEL, D_MODEL), fixed, pipeline_mode=pl.Buffered(1)),
            pl.BlockSpec((1, D_MODEL), fixed),
            pl.BlockSpec((D_MODEL, LANES), fixed),
            pl.BlockSpec((1, LANES), fixed),
        ],
        out_specs=[
            pl.BlockSpec((ROW_TILE, D_MODEL), row),
            pl.BlockSpec((ROW_TILE, D_MODEL), row),
            pl.BlockSpec((ROW_TILE, LANES), row),
            pl.BlockSpec((1, LANES), fixed),
        ],
        out_shape=[
            jax.ShapeDtypeStruct((t, D_MODEL), F32),
            jax.ShapeDtypeStruct((t, D_MODEL), BF16),
            jax.ShapeDtypeStruct((t, LANES), F32),
            jax.ShapeDtypeStruct((1, LANES), F32),
        ],
        scratch_shapes=[pltpu.VMEM((ROW_TILE // 2, ROW_TILE // 2), BF16), pltpu.VMEM((1, LANES), F32)],
        compiler_params=_params(("arbitrary",), 48),
    )(o_gla, o_fox, h0, w_out, g2, w_router, b_router)


def _scatter_kernel(pos1_ref, pos2_ref, last_ref, has_ref, nv_ref, u2_ref, xs_ref, rows_ref, zero_ref, sems, zsem):
    i = pl.program_id(0)
    n_steps = pl.num_programs(0)
    tile_rows = EXPERT_TILE * ROW_SUBLANES
    n_tiles = xs_ref.shape[0] // tile_rows

    def zero_copy(start):
        start = pl.multiple_of(start * ROW_SUBLANES, tile_rows)
        return pltpu.make_async_copy(zero_ref, xs_ref.at[pl.ds(start, tile_rows), :], zsem)

    def for_each_zero_tile(action):
        for e in range(N_EXPERTS):
            @pl.when(has_ref[e] > 0)
            def _():
                action(zero_copy(last_ref[e]))

        def unused_tile(j, carry):
            action(zero_copy(j * EXPERT_TILE))
            return carry

        lax.fori_loop(nv_ref[0], n_tiles, unused_tile, 0)

    @pl.when(i == 0)
    def _():
        zero_ref[...] = jnp.zeros_like(zero_ref)
        for_each_zero_tile(lambda cp: cp.start())
        for_each_zero_tile(lambda cp: cp.wait())

    def row_copy(step, r, pos_ref):
        slot = step % 2
        src = pl.multiple_of(r * ROW_SUBLANES, ROW_SUBLANES)
        dst = pl.multiple_of(pos_ref[step * ROW_TILE + r] * ROW_SUBLANES, ROW_SUBLANES)
        return pltpu.make_async_copy(rows_ref.at[slot, pl.ds(src, ROW_SUBLANES), :],
                                     xs_ref.at[pl.ds(dst, ROW_SUBLANES), :], sems.at[slot])

    def for_each_row(step, action):
        def group(j, carry):
            for k in range(DMA_UNROLL):
                action(row_copy(step, j * DMA_UNROLL + k, pos1_ref))
                action(row_copy(step, j * DMA_UNROLL + k, pos2_ref))
            return carry

        lax.fori_loop(0, ROW_TILE // DMA_UNROLL, group, 0)

    _rows_store(rows_ref.at[i % 2], u2_ref[...].astype(F32), ROW_SUBLANES)
    for_each_row(i, lambda cp: cp.start())

    @pl.when(i > 0)
    def _():
        for_each_row(i - 1, lambda cp: cp.wait())

    @pl.when(i == n_steps - 1)
    def _():
        for_each_row(i, lambda cp: cp.wait())


def _scatter(pos1, pos2, last_row, has, n_valid, u2, n_rows):
    t = u2.shape[0]
    return pl.pallas_call(
        _scatter_kernel,
        grid_spec=pltpu.PrefetchScalarGridSpec(
            num_scalar_prefetch=5,
            grid=(t // ROW_TILE,),
            in_specs=[pl.BlockSpec((ROW_TILE, D_MODEL), lambda i, *_: (i, 0))],
            out_specs=pl.BlockSpec(memory_space=pl.ANY),
            scratch_shapes=[pltpu.VMEM((2, ROW_TILE * ROW_SUBLANES, LANES), F32),
                            pltpu.VMEM((EXPERT_TILE * ROW_SUBLANES, LANES), F32),
                            pltpu.SemaphoreType.DMA((2,)), pltpu.SemaphoreType.DMA],
        ),
        out_shape=jax.ShapeDtypeStruct((n_rows * ROW_SUBLANES, LANES), F32),
        compiler_params=_params(("arbitrary",), 40),
    )(pos1, pos2, last_row, has, n_valid, u2)


def _moe_kernel(te_ref, first_ref, next_ref, nv_ref, xs_ref, wg_hbm, wu_hbm, wd_hbm, y_ref,
                wg_stage, wu_stage, wd_stage, wgb_ref, wub_ref, wdb_ref, sems):
    i = pl.program_id(0)

    def weight_copies(expert):
        return (pltpu.make_async_copy(wg_hbm.at[expert], wg_stage, sems.at[0]),
                pltpu.make_async_copy(wu_hbm.at[expert], wu_stage, sems.at[1]),
                pltpu.make_async_copy(wd_hbm.at[expert], wd_stage, sems.at[2]))

    @pl.when(i >= nv_ref[0])
    def _():
        y_ref[...] = jnp.zeros_like(y_ref)

    @pl.when(i < nv_ref[0])
    def _():
        @pl.when(first_ref[i] == 1)
        def _():
            @pl.when(i == 0)
            def _():
                for cp in weight_copies(te_ref[i]):
                    cp.start()

            for cp in weight_copies(te_ref[i]):
                cp.wait()
            wgb_ref[...] = wg_stage[...].astype(BF16)
            wub_ref[...] = wu_stage[...].astype(BF16)
            wdb_ref[...] = wd_stage[...].astype(BF16)

            @pl.when(next_ref[i] >= 0)
            def _():
                for cp in weight_copies(next_ref[i]):
                    cp.start()

        x = _rows_load(xs_ref, EXPERT_TILE, ROW_SUBLANES).astype(BF16)
        hg = jnp.dot(x, wgb_ref[...], preferred_element_type=F32)
        hu = jnp.dot(x, wub_ref[...], preferred_element_type=F32)
        hm = (hg * _sigmoid(hg) * hu).astype(BF16)
        y = jnp.dot(hm, wdb_ref[...], preferred_element_type=F32)
        _rows_store(y_ref, y, ROW_SUBLANES)


def _moe(tile_expert, tile_first, tile_next, n_valid, xs, w_g, w_u, w_d):
    n_rows = xs.shape[0] // ROW_SUBLANES
    row = lambda i, *_: (i, 0)
    return pl.pallas_call(
        _moe_kernel,
        grid_spec=pltpu.PrefetchScalarGridSpec(
            num_scalar_prefetch=4,
            grid=(n_rows // EXPERT_TILE,),
            in_specs=[
                pl.BlockSpec((EXPERT_TILE * ROW_SUBLANES, LANES), row),
                pl.BlockSpec(memory_space=pl.ANY),
                pl.BlockSpec(memory_space=pl.ANY),
                pl.BlockSpec(memory_space=pl.ANY),
            ],
            out_specs=pl.BlockSpec((EXPERT_TILE * ROW_SUBLANES, LANES), row),
            scratch_shapes=[pltpu.VMEM((D_MODEL, D_EXPERT), F32), pltpu.VMEM((D_MODEL, D_EXPERT), F32),
                            pltpu.VMEM((D_EXPERT, D_MODEL), F32),
                            pltpu.VMEM((D_MODEL, D_EXPERT), BF16), pltpu.VMEM((D_MODEL, D_EXPERT), BF16),
                            pltpu.VMEM((D_EXPERT, D_MODEL), BF16),
                            pltpu.SemaphoreType.DMA((3,))],
        ),
        out_shape=jax.ShapeDtypeStruct((n_rows * ROW_SUBLANES, LANES), F32),
        compiler_params=_params(("arbitrary",), 48),
    )(tile_expert, tile_first, tile_next, n_valid, xs, w_g, w_u, w_d)


def _combine_kernel(pos1_ref, pos2_ref, h1_ref, route_ref, gf_ref, y_ref, out_ref, ya_ref, yb_ref, sems):
    i = pl.program_id(0)
    n_steps = pl.num_programs(0)

    def row_copy(tile, r, pos_ref, buf_ref):
        slot = tile % 2
        src = pl.multiple_of(pos_ref[tile * OUT_TILE + r] * ROW_SUBLANES, ROW_SUBLANES)
        dst = pl.multiple_of(r * ROW_SUBLANES, ROW_SUBLANES)
        return pltpu.make_async_copy(y_ref.at[pl.ds(src, ROW_SUBLANES), :],
                                     buf_ref.at[slot, pl.ds(dst, ROW_SUBLANES), :], sems.at[slot])

    def for_each_row(tile, action):
        def group(j, carry):
            for k in range(DMA_UNROLL):
                action(row_copy(tile, j * DMA_UNROLL + k, pos1_ref, ya_ref))
                action(row_copy(tile, j * DMA_UNROLL + k, pos2_ref, yb_ref))
            return carry

        lax.fori_loop(0, OUT_TILE // DMA_UNROLL, group, 0)

    @pl.when(i == 0)
    def _():
        for_each_row(i, lambda cp: cp.start())

    @pl.when(i + 1 < n_steps)
    def _():
        for_each_row(i + 1, lambda cp: cp.start())

    for_each_row(i, lambda cp: cp.wait())
    slot = i % 2
    rec = route_ref[...]
    ya = _rows_load(ya_ref.at[slot], OUT_TILE, ROW_SUBLANES)
    yb = _rows_load(yb_ref.at[slot], OUT_TILE, ROW_SUBLANES)
    hh = h1_ref[...] + rec[:, ROUTE_W1:ROUTE_W1 + 1] * ya + rec[:, ROUTE_W2:ROUTE_W2 + 1] * yb
    ms = jnp.mean(hh * hh, axis=-1, keepdims=True)
    out_ref[...] = hh * lax.rsqrt(ms + EPS) * gf_ref[...]


def _combine(pos1, pos2, h1, route, g_f, y):
    t = h1.shape[0]
    row = lambda i, *_: (i, 0)
    return pl.pallas_call(
        _combine_kernel,
        grid_spec=pltpu.PrefetchScalarGridSpec(
            num_scalar_prefetch=2,
            grid=(t // OUT_TILE,),
            in_specs=[
                pl.BlockSpec((OUT_TILE, D_MODEL), row),
                pl.BlockSpec((OUT_TILE, LANES), row),
                pl.BlockSpec((1, D_MODEL), lambda i, *_: (0, 0)),
                pl.BlockSpec(memory_space=pl.ANY),
            ],
            out_specs=pl.BlockSpec((OUT_TILE, D_MODEL), row),
            scratch_shapes=[pltpu.VMEM((2, OUT_TILE * ROW_SUBLANES, LANES), F32),
                            pltpu.VMEM((2, OUT_TILE * ROW_SUBLANES, LANES), F32),
                            pltpu.SemaphoreType.DMA((2,))],
        ),
        out_shape=jax.ShapeDtypeStruct((t, D_MODEL), F32),
        compiler_params=_params(("arbitrary",), 40),
    )(pos1, pos2, h1, route, g_f, y)


def _routing_tables(route, cnt, n_tiles):
    counts = cnt[0, N_GROUPS:N_GROUPS + N_EXPERTS].astype(jnp.int32)
    tiles = (counts + EXPERT_TILE - 1) // EXPERT_TILE
    tile_end = jnp.cumsum(tiles)
    row_start = (tile_end - tiles) * EXPERT_TILE
    n_valid = tile_end[-1]
    expert_ids = jnp.arange(N_EXPERTS, dtype=jnp.int32)

    def positions(expert_lane, rank_lane):
        expert = route[:, expert_lane].astype(jnp.int32)
        start = jnp.sum(jnp.where(expert[:, None] == expert_ids[None, :], row_start[None, :], 0), axis=1)
        return start + route[:, rank_lane].astype(jnp.int32)

    pos1 = positions(ROUTE_E1, ROUTE_R1)
    pos2 = positions(ROUTE_E2, ROUTE_R2)
    tile = jnp.minimum(jnp.arange(n_tiles, dtype=jnp.int32), n_valid - 1)
    tile_expert = jnp.sum(tile[:, None] >= tile_end[None, :], axis=-1).astype(jnp.int32)
    owner = tile_expert[:, None] == expert_ids[None, :]
    tile_first = (tile == jnp.sum(jnp.where(owner, (tile_end - tiles)[None, :], 0), axis=1)).astype(jnp.int32)
    later = (expert_ids[None, :] > expert_ids[:, None]) & (tiles[None, :] > 0)
    next_expert = jnp.min(jnp.where(later, expert_ids[None, :], N_EXPERTS), axis=1)
    next_expert = jnp.where(next_expert == N_EXPERTS, -1, next_expert)
    tile_next = jnp.sum(jnp.where(owner, next_expert[None, :], 0), axis=1).astype(jnp.int32)
    last_row = (tile_end - 1) * EXPERT_TILE
    return pos1, pos2, tile_expert, tile_first, tile_next, n_valid.reshape(1), last_row, tiles


def kernel(x, meta_tokens, norm1_g, w_in, b_fox_f, gla_w_gate2, gla_b_gate, gla_norm_g, fox_norm_g, w_out,
           norm2_g, w_router_group, b_router_group, w_router_expert, b_router_expert, w_exp_gate, w_exp_up,
           w_exp_down, norm_f_g):
    batch, seq, _ = x.shape
    assert batch == 1 and norm1_g.shape[0] == 1
    assert seq % FOX_TILE == 0 and seq % ROW_TILE == 0 and (HEAD_ROWS + seq) % BIAS_BLOCK == 0
    t = HEAD_ROWS + seq
    x2 = x[0]
    head = jnp.concatenate([jnp.zeros((PROJ_SKIP + PAD_FRONT, D_MODEL), F32), meta_tokens.astype(F32)], axis=0)

    assert w_in.shape == (1, D_MODEL, D_IN_PROJ) and PROJ_ALIGNED % PROJ_STAGE_COLS == 0
    proj, small = _in_proj(head, x2, norm1_g, w_in[0].T)

    negc = _fox_bias(small, b_fox_f[0].reshape(FOX_HEADS, 1))
    w2_pad = jnp.zeros((LANES, GLA_DK_TOT), F32).at[FOX_HEADS:FOX_HEADS + GLA_RANK].set(gla_w_gate2[0])
    o_gla = _gla(proj, small, w2_pad, gla_b_gate, gla_norm_g)
    o_fox = _fox(proj, negc.reshape(FOX_HEADS, 1, t), fox_norm_g)

    w_router = jnp.concatenate(
        [w_router_group[0], jnp.transpose(w_router_expert[0], (1, 0, 2)).reshape(D_MODEL, N_EXPERTS),
         jnp.zeros((D_MODEL, LANES - N_GROUPS - N_EXPERTS), F32)], axis=1).astype(BF16)
    b_router = jnp.concatenate([b_router_group[0], b_router_expert[0].reshape(-1),
                                jnp.zeros((LANES - N_GROUPS - N_EXPERTS,), F32)]).reshape(1, LANES)
    h1, u2, route, cnt = _out_proj(o_gla, o_fox, x2, w_out[0].astype(BF16), norm2_g, w_router, b_router)

    n_tiles = (2 * seq) // EXPERT_TILE + N_EXPERTS
    pos1, pos2, tile_expert, tile_first, tile_next, n_valid, last_row, tiles = _routing_tables(route, cnt, n_tiles)
    xs = _scatter(pos1, pos2, last_row, tiles, n_valid, u2, n_tiles * EXPERT_TILE)
    y = _moe(tile_expert, tile_first, tile_next, n_valid, xs,
             w_exp_gate[0].reshape(N_EXPERTS, D_MODEL, D_EXPERT),
             w_exp_up[0].reshape(N_EXPERTS, D_MODEL, D_EXPERT),
             w_exp_down[0].reshape(N_EXPERTS, D_EXPERT, D_MODEL))
    out = _combine(pos1, pos2, h1, route, norm_f_g.reshape(1, D_MODEL), y)
    return out.reshape(1, seq, D_MODEL)
```

```python
import jax
import jax.numpy as jnp
from jax import lax
from jax.experimental import pallas as pl
from jax.experimental.pallas import tpu as pltpu

D_MODEL = 2048
N_META = 16
GLA_HEADS = 4
GLA_DK = 128
GLA_DV = 256
GLA_DK_TOT = GLA_HEADS * GLA_DK
GLA_DV_TOT = GLA_HEADS * GLA_DV
GLA_RANK = 16
GLA_TAU = 16.0
GLA_CHUNK = 64
FOX_HEADS = 8
FOX_HD = 128
FOX_W = FOX_HEADS * FOX_HD
FOX_BLOCK = 128
PAD_FRONT = FOX_BLOCK - N_META
HEAD_ROWS = PAD_FRONT + N_META
N_GROUPS = 4
EXPERTS_PER_GROUP = 8
N_EXPERTS = N_GROUPS * EXPERTS_PER_GROUP
D_EXPERT = 512
EPS = 1e-6

LANES = 128
PROJ_ROWS = 2 * HEAD_ROWS
PROJ_SKIP = PROJ_ROWS - HEAD_ROWS
BIAS_BLOCK = 640
GLA_ROWS = 4 * GLA_CHUNK
FOX_TILE = 1024
FOX_ROWS = 128
FOX_KEYS = 1024
FOX_GROUP = 2
FOX_SKEW = 3
LOG2E = 1.4426950408889634
ROW_TILE = 512
EXPERT_TILE = 256
OUT_TILE = 256
MASK_VALUE = -1e30
PROJ_BIG = 3 * FOX_W + 2 * GLA_DK_TOT + 2 * GLA_DV_TOT
PROJ_FF = 3 * FOX_W
PROJ_GQ = PROJ_FF + FOX_HEADS
PROJ_GZ = PROJ_GQ + 2 * GLA_DK_TOT + 2 * GLA_DV_TOT
D_IN_PROJ = PROJ_GZ + GLA_RANK
PROJ_ALIGNED = (D_IN_PROJ // LANES) * LANES
PROJ_STAGE_COLS = 256
ROW_SUBLANES = D_MODEL // LANES
DMA_UNROLL = 8

F32 = jnp.float32
BF16 = jnp.bfloat16
NT_DIMS = (((1,), (1,)), ((), ()))
TN_DIMS = (((0,), (0,)), ((), ()))


def _log_sigmoid(x):
    return jnp.minimum(x, 0.0) - jnp.log(1.0 + jnp.exp(-jnp.abs(x)))


def _sigmoid(x):
    return 1.0 / (1.0 + jnp.exp(-x))


def _split3(x):
    hi = x.astype(BF16)
    rest = x - hi.astype(F32)
    mid = rest.astype(BF16)
    lo = (rest - mid.astype(F32)).astype(BF16)
    return hi, mid, lo


def _rows_load(ref, n_rows, n_chunks):
    return jnp.concatenate([ref[pl.ds(s, n_rows, stride=n_chunks), :] for s in range(n_chunks)], axis=1)


def _rows_store(ref, value, n_chunks):
    n_rows = value.shape[0]
    for s in range(n_chunks):
        ref[pl.ds(s, n_rows, stride=n_chunks), :] = value[:, s * LANES:(s + 1) * LANES]


def _params(semantics, vmem_mb):
    return pltpu.CompilerParams(dimension_semantics=semantics, vmem_limit_bytes=vmem_mb * 1024 * 1024)


def _in_proj_kernel(head_ref, x_ref, g_ref, wt_hbm, proj_ref, small_ref, w_ref, stage_ref, tail_ref, sems):
    n_chunks = PROJ_ALIGNED // PROJ_STAGE_COLS
    n_tail = D_IN_PROJ - PROJ_ALIGNED

    def chunk_copy(c):
        return pltpu.make_async_copy(wt_hbm.at[pl.ds(c * PROJ_STAGE_COLS, PROJ_STAGE_COLS), :],
                                     stage_ref.at[c % 2], sems.at[c % 2])

    def tail_copy():
        return pltpu.make_async_copy(wt_hbm.at[pl.ds(PROJ_ALIGNED, n_tail), :],
                                     tail_ref.at[pl.ds(0, n_tail), :], sems.at[2])

    @pl.when(pl.program_id(0) == 0)
    def _():
        tail_ref[...] = jnp.zeros_like(tail_ref)
        tail_copy().start()
        chunk_copy(0).start()
        for c in range(n_chunks):
            if c + 1 < n_chunks:
                chunk_copy(c + 1).start()
            chunk_copy(c).wait()
            w_ref[:, c * PROJ_STAGE_COLS:(c + 1) * PROJ_STAGE_COLS] = stage_ref[c % 2].T.astype(BF16)
        tail_copy().wait()
        w_ref[:, PROJ_ALIGNED:] = tail_ref[...].T.astype(BF16)

    x = jnp.where(pl.program_id(0) == 0, head_ref[...], x_ref[...])
    ms = jnp.mean(x * x, axis=-1, keepdims=True)
    xn = (x * lax.rsqrt(ms + EPS) * g_ref[...]).astype(BF16)
    p = jnp.dot(xn, w_ref[...], preferred_element_type=F32)
    lane = lax.broadcasted_iota(jnp.int32, (PROJ_ROWS, LANES), 1)
    small_ref[...] = jnp.where(lane < FOX_HEADS, p[:, PROJ_FF:PROJ_FF + LANES], p[:, PROJ_ALIGNED:])
    proj_ref[...] = jnp.concatenate([p[:, :PROJ_FF], p[:, PROJ_GQ:PROJ_GZ]], axis=1).astype(BF16)


def _in_proj(head, x, g1, w_t):
    t = PROJ_ROWS + x.shape[0]
    fixed = lambda i: (0, 0)
    return pl.pallas_call(
        _in_proj_kernel,
        grid=(t // PROJ_ROWS,),
        in_specs=[
            pl.BlockSpec((PROJ_ROWS, D_MODEL), fixed),
            pl.BlockSpec((PROJ_ROWS, D_MODEL), lambda i: (jnp.maximum(i - 1, 0), 0)),
            pl.BlockSpec((1, D_MODEL), fixed),
            pl.BlockSpec(memory_space=pl.ANY),
        ],
        out_specs=[
            pl.BlockSpec((PROJ_ROWS, PROJ_BIG), lambda i: (i, 0)),
            pl.BlockSpec((PROJ_ROWS, LANES), lambda i: (i, 0)),
        ],
        out_shape=[
            jax.ShapeDtypeStruct((t, PROJ_BIG), BF16),
            jax.ShapeDtypeStruct((t, LANES), F32),
        ],
        scratch_shapes=[pltpu.VMEM((D_MODEL, PROJ_ALIGNED + LANES), BF16),
                        pltpu.VMEM((2, PROJ_STAGE_COLS, D_MODEL), F32),
                        pltpu.VMEM((LANES, D_MODEL), F32),
                        pltpu.SemaphoreType.DMA((3,))],
        compiler_params=_params(("arbitrary",), 56),
    )(head, x, g1, w_t)


def _fox_bias_kernel(small_ref, bf_ref, negc_ref):
    t = negc_ref.shape[1]
    r = lax.broadcasted_iota(jnp.int32, (BIAS_BLOCK, BIAS_BLOCK), 0)
    c = lax.broadcasted_iota(jnp.int32, (BIAS_BLOCK, BIAS_BLOCK), 1)
    upper = jnp.where(r <= c, 1.0, 0.0).astype(BF16)
    lane = lax.broadcasted_iota(jnp.int32, (FOX_HEADS, BIAS_BLOCK), 1)

    def body(b, carry):
        off = pl.multiple_of(b * BIAS_BLOCK, BIAS_BLOCK)
        valid = (off + lane) >= PAD_FRONT
        f_logit = small_ref[pl.ds(PROJ_SKIP + off, BIAS_BLOCK), :].T[0:FOX_HEADS, :]
        lf = jnp.where(valid, _log_sigmoid(f_logit + bf_ref[...]), 0.0)
        cum = sum(jnp.dot(piece, upper, preferred_element_type=F32) for piece in _split3(lf)) + carry
        negc_ref[:, pl.ds(off, BIAS_BLOCK)] = jnp.where(valid, -LOG2E * cum, MASK_VALUE)
        return cum[:, BIAS_BLOCK - 1:BIAS_BLOCK]

    lax.fori_loop(0, t // BIAS_BLOCK, body, jnp.zeros((FOX_HEADS, 1), F32))


def _fox_bias(small, b_f):
    return pl.pallas_call(
        _fox_bias_kernel,
        out_shape=jax.ShapeDtypeStruct((FOX_HEADS, small.shape[0] - PROJ_SKIP), F32),
    )(small, b_f)


def _gla_kernel(q_ref, k_ref, v_ref, r_ref, small_ref, w2_ref, bg_ref, ng_ref, o_ref, st_ref):
    i = pl.program_id(0)

    @pl.when(i == 0)
    def _():
        st_ref[...] = jnp.zeros_like(st_ref)

    z_hi, z_mid, _ = _split3(small_ref[...])
    w_hi, w_mid, _ = _split3(w2_ref[...])
    gate_logit = (jnp.dot(z_hi, w_hi, preferred_element_type=F32) + jnp.dot(z_hi, w_mid, preferred_element_type=F32)
                  + jnp.dot(z_mid, w_hi, preferred_element_type=F32) + bg_ref[...])
    g = _log_sigmoid(gate_logit) * (1.0 / GLA_TAU)
    rowid = i * GLA_ROWS + lax.broadcasted_iota(jnp.int32, (GLA_ROWS, 1), 0)
    g = jnp.where(rowid >= PROJ_SKIP + PAD_FRONT, g, 0.0)

    ci = lax.broadcasted_iota(jnp.int32, (GLA_CHUNK, GLA_CHUNK), 0)
    cj = lax.broadcasted_iota(jnp.int32, (GLA_CHUNK, GLA_CHUNK), 1)
    causal = cj <= ci
    lower = jnp.where(causal, 1.0, 0.0).astype(BF16)
    scale = GLA_DK ** -0.5
    mid = GLA_CHUNK // 2

    chunks = [slice(c * GLA_CHUNK, (c + 1) * GLA_CHUNK) for c in range(GLA_ROWS // GLA_CHUNK)]
    cums = [sum(jnp.dot(lower, piece, preferred_element_type=F32) for piece in _split3(g[rows])) for rows in chunks]

    heads = range(GLA_HEADS)
    ks = [slice(h * GLA_DK, (h + 1) * GLA_DK) for h in heads]
    vs = [slice(h * GLA_DV, (h + 1) * GLA_DV) for h in heads]

    prep = []
    for rows, b in zip(chunks, cums):
        b_mid = b[mid:mid + 1]
        b_last = b[GLA_CHUNK - 1:GLA_CHUNK]
        q = q_ref[rows, :].astype(F32) * scale
        k = k_ref[rows, :].astype(F32)
        q_intra = (q * jnp.exp(b - b_mid)).astype(BF16)
        k_intra = (k * jnp.exp(b_mid - b)).astype(BF16)
        q_inter = (q * jnp.exp(b)).astype(BF16)
        k_state = (k * jnp.exp(b_last - b)).astype(BF16)
        decay = jnp.exp(b_last)
        v = [v_ref[rows, vs[h]] for h in heads]
        a = [lax.dot_general(q_intra[:, ks[h]], k_intra[:, ks[h]], NT_DIMS, preferred_element_type=F32)
             for h in heads]
        u_t = [lax.dot_general(v[h], k_state[:, ks[h]], TN_DIMS, preferred_element_type=F32) for h in heads]
        prep.append((q_inter, decay, v, a, u_t))

    st = [st_ref[h] for h in heads]
    for rows, (q_inter, decay, v, a, u_t) in zip(chunks, prep):
        o_inter = [lax.dot_general(q_inter[:, ks[h]], st[h].astype(BF16), NT_DIMS, preferred_element_type=F32)
                   for h in heads]
        st = [decay[:, ks[h]] * st[h] + u_t[h] for h in heads]
        for h in heads:
            o = o_inter[h] + jnp.dot(jnp.where(causal, a[h], 0.0).astype(BF16), v[h], preferred_element_type=F32)
            ms = jnp.mean(o * o, axis=-1, keepdims=True)
            y = o * lax.rsqrt(ms + EPS) * ng_ref[...]
            r = r_ref[rows, vs[h]].astype(F32)
            o_ref[rows, vs[h]] = (y * (r * _sigmoid(r))).astype(BF16)
    for h in heads:
        st_ref[h] = st[h]


def _gla(proj, small, w2_pad, b_gate, norm_g):
    rows = proj.shape[0]
    assert GLA_ROWS == PROJ_SKIP + HEAD_ROWS
    q_blk = (3 * FOX_W) // GLA_DK_TOT
    v_blk = (3 * FOX_W + 2 * GLA_DK_TOT) // GLA_DV_TOT
    return pl.pallas_call(
        _gla_kernel,
        grid=(rows // GLA_ROWS,),
        in_specs=[
            pl.BlockSpec((GLA_ROWS, GLA_DK_TOT), lambda i: (i, q_blk)),
            pl.BlockSpec((GLA_ROWS, GLA_DK_TOT), lambda i: (i, q_blk + 1)),
            pl.BlockSpec((GLA_ROWS, GLA_DV_TOT), lambda i: (i, v_blk)),
            pl.BlockSpec((GLA_ROWS, GLA_DV_TOT), lambda i: (i, v_blk + 1)),
            pl.BlockSpec((GLA_ROWS, LANES), lambda i: (i, 0)),
            pl.BlockSpec((LANES, GLA_DK_TOT), lambda i: (0, 0)),
            pl.BlockSpec((1, GLA_DK_TOT), lambda i: (0, 0)),
            pl.BlockSpec((1, GLA_DV), lambda i: (0, 0)),
        ],
        out_specs=pl.BlockSpec((GLA_ROWS, GLA_DV_TOT), lambda i: (jnp.maximum(i - 1, 0), 0)),
        out_shape=jax.ShapeDtypeStruct((rows - GLA_ROWS, GLA_DV_TOT), BF16),
        scratch_shapes=[pltpu.VMEM((GLA_HEADS, GLA_DV, GLA_DK), F32)],
        compiler_params=_params(("arbitrary",), 32),
    )(proj, proj, proj, proj, small, w2_pad, b_gate, norm_g)


def _fox_kernel(q_ref, k_ref, v_ref, negc_ref, ng_ref, o_ref, qs_ref, va_ref):
    qi = pl.program_id(1)
    n_blocks = FOX_TILE // FOX_ROWS

    @pl.when(qi == 0)
    def _():
        lane = lax.broadcasted_iota(jnp.int32, (v_ref.shape[0], FOX_HD), 1)
        ones_col = jnp.where(lane == 0, 1.0, 0.0).astype(BF16)
        for hh in range(FOX_GROUP):
            va_ref[:, 2 * hh * FOX_HD:(2 * hh + 1) * FOX_HD] = v_ref[:, hh * FOX_HD:(hh + 1) * FOX_HD]
            va_ref[:, (2 * hh + 1) * FOX_HD:(2 * hh + 2) * FOX_HD] = ones_col

    units = [(hh, rb) for hh in range(FOX_GROUP) for rb in range(n_blocks)]
    q0 = pl.multiple_of(HEAD_ROWS + qi * FOX_TILE, FOX_ROWS)
    qs_ref[...] = (q_ref[pl.ds(PROJ_SKIP + q0, FOX_TILE), :].astype(F32) * (FOX_HD ** -0.5 * LOG2E)).astype(BF16)
    row = lax.broadcasted_iota(jnp.int32, (FOX_ROWS, FOX_ROWS), 0)
    col = lax.broadcasted_iota(jnp.int32, (FOX_ROWS, FOX_ROWS), 1)

    def run(state, steps):
        def scores(step):
            u, off, k0, k1, causal_tail = step
            hh, rb = units[u]
            rows = slice(rb * FOX_ROWS, (rb + 1) * FOX_ROWS)
            cols = slice(hh * FOX_HD, (hh + 1) * FOX_HD)
            s = lax.dot_general(qs_ref[rows, cols], k_ref[pl.ds(PROJ_SKIP + off + k0, k1 - k0), cols], NT_DIMS,
                                preferred_element_type=F32)
            s = s + negc_ref[hh, :, pl.ds(off + k0, k1 - k0)]
            if causal_tail:
                tail = jnp.where(col <= row, s[:, k1 - k0 - FOX_ROWS:], MASK_VALUE)
                s = tail if k1 - k0 == FOX_ROWS else jnp.concatenate([s[:, :k1 - k0 - FOX_ROWS], tail], axis=1)
            return s

        def update(step, s, state):
            u, off, k0, k1, _ = step
            hh, _ = units[u]
            m_prev, acc_prev = state[u]
            m_new = jnp.maximum(m_prev, jnp.max(s, axis=-1, keepdims=True))
            p = jnp.exp2(s - m_new).astype(BF16)
            acc_new = jnp.exp2(m_prev - m_new) * acc_prev + jnp.dot(
                p, va_ref[pl.ds(PROJ_SKIP + off + k0, k1 - k0), 2 * hh * FOX_HD:(2 * hh + 2) * FOX_HD],
                preferred_element_type=F32)
            state[u] = (m_new, acc_new)

        state = list(state)
        pending = [scores(st) for st in steps[:FOX_SKEW]]
        for j, st in enumerate(steps):
            if j + FOX_SKEW < len(steps):
                pending.append(scores(steps[j + FOX_SKEW]))
            update(st, pending[j], state)
            pending[j] = None
        return tuple(state)

    head_steps = [(u, 0, 0, HEAD_ROWS, False) for u in range(len(units))]

    def full_steps(off):
        return [(u, off, k0, k0 + FOX_KEYS, False)
                for k0 in range(0, FOX_TILE, FOX_KEYS) for u in range(len(units))]

    diag_steps = []
    for k0 in range(0, FOX_TILE, FOX_KEYS):
        for u, (_, rb) in enumerate(units):
            last = (rb + 1) * FOX_ROWS
            if last > k0:
                diag_steps.append((u, q0, k0, min(k0 + FOX_KEYS, last), last <= k0 + FOX_KEYS))

    state = tuple((jnp.full((FOX_ROWS, 1), MASK_VALUE, F32), jnp.zeros((FOX_ROWS, 2 * FOX_HD), F32))
                  for _ in units)
    state = run(state, head_steps)
    state = lax.fori_loop(
        0, qi, lambda kt, st: run(st, full_steps(pl.multiple_of(HEAD_ROWS + kt * FOX_TILE, FOX_ROWS))), state)
    state = run(state, diag_steps)
    for u, (hh, rb) in enumerate(units):
        _, acc = state[u]
        o = acc[:, :FOX_HD] / acc[:, FOX_HD:FOX_HD + 1]
        ms = jnp.mean(o * o, axis=-1, keepdims=True)
        o_ref[rb * FOX_ROWS:(rb + 1) * FOX_ROWS, hh * FOX_HD:(hh + 1) * FOX_HD] = (
            o * lax.rsqrt(ms + EPS) * ng_ref[...]).astype(BF16)


def _fox(proj, negc3, norm_g):
    rows = proj.shape[0]
    t = rows - PROJ_SKIP
    width = FOX_GROUP * FOX_HD
    k_blk = FOX_W // width
    return pl.pallas_call(
        _fox_kernel,
        grid=(FOX_HEADS // FOX_GROUP, (t - HEAD_ROWS) // FOX_TILE),
        in_specs=[
            pl.BlockSpec((rows, width), lambda g, i: (0, g)),
            pl.BlockSpec((rows, width), lambda g, i: (0, k_blk + g)),
            pl.BlockSpec((rows, width), lambda g, i: (0, 2 * k_blk + g)),
            pl.BlockSpec((FOX_GROUP, 1, t), lambda g, i: (g, 0, 0)),
            pl.BlockSpec((1, FOX_HD), lambda g, i: (0, 0)),
        ],
        out_specs=pl.BlockSpec((FOX_TILE, width), lambda g, i: (i, g)),
        out_shape=jax.ShapeDtypeStruct((t - HEAD_ROWS, FOX_W), BF16),
        scratch_shapes=[pltpu.VMEM((FOX_TILE, width), BF16), pltpu.VMEM((rows, 2 * width), BF16)],
        compiler_params=_params(("arbitrary", "arbitrary"), 56),
    )(proj, proj, proj, negc3, norm_g)


ROUTE_E1, ROUTE_E2, ROUTE_R1, ROUTE_R2, ROUTE_W1, ROUTE_W2 = range(6)


def _out_proj_kernel(og_ref, of_ref, h_ref, wo_ref, g2_ref, wr_ref, br_ref,
                     h1_ref, u2_ref, route_ref, cnt_ref, tri_ref, run_ref):
    i = pl.program_id(0)
    half = ROW_TILE // 2

    @pl.when(i == 0)
    def _():
        r = lax.broadcasted_iota(jnp.int32, (half, half), 0)
        c = lax.broadcasted_iota(jnp.int32, (half, half), 1)
        tri_ref[...] = jnp.where(c < r, 1.0, 0.0).astype(BF16)
        run_ref[...] = jnp.zeros_like(run_ref)

    lane = lax.broadcasted_iota(jnp.int32, (half, LANES), 1).astype(F32)
    ninf = -jnp.inf

    def first_max(vals):
        top = jnp.max(vals, axis=-1, keepdims=True)
        idx = jnp.min(jnp.where(vals == top, lane, float(LANES)), axis=-1, keepdims=True)
        return top, idx

    halves = [slice(0, half), slice(half, ROW_TILE)]
    h1 = [h_ref[rows, :]
          + jnp.dot(og_ref[rows, :], wo_ref[0:GLA_DV_TOT, :], preferred_element_type=F32)
          + jnp.dot(of_ref[rows, :], wo_ref[GLA_DV_TOT:, :], preferred_element_type=F32) for rows in halves]
    running = run_ref[...]
    for rows, h1_half in zip(halves, h1):
        h1_ref[rows, :] = h1_half
        ms = jnp.mean(h1_half * h1_half, axis=-1, keepdims=True)
        u2 = (h1_half * lax.rsqrt(ms + EPS) * g2_ref[...]).astype(BF16)
        u2_ref[rows, :] = u2

        logits = jnp.dot(u2, wr_ref[...], preferred_element_type=F32) + br_ref[...]
        gl = jnp.where(lane < N_GROUPS, logits, ninf)
        g_top, g_idx = first_max(gl)
        p_g = 1.0 / jnp.sum(jnp.exp(gl - g_top), axis=-1, keepdims=True)
        e_lo = N_GROUPS + EXPERTS_PER_GROUP * g_idx
        el = jnp.where((lane >= e_lo) & (lane < e_lo + EXPERTS_PER_GROUP), logits, ninf)
        top1, i1 = first_max(el)
        top2, i2 = first_max(jnp.where(lane == i1, ninf, el))
        ratio = jnp.exp(top2 - top1)
        w1 = 1.0 / (1.0 + ratio)
        w2 = ratio * w1

        is1 = lane == i1
        is2 = lane == i2
        onehot = jnp.where(is1 | is2, 1.0, 0.0)
        before = jnp.dot(tri_ref[...], onehot.astype(BF16), preferred_element_type=F32) + running
        r1 = jnp.sum(jnp.where(is1, before, 0.0), axis=-1, keepdims=True)
        r2 = jnp.sum(jnp.where(is2, before, 0.0), axis=-1, keepdims=True)
        running = running + jnp.sum(onehot, axis=0, keepdims=True)

        rec = jnp.zeros_like(logits)
        for slot, val in ((ROUTE_E1, i1 - N_GROUPS), (ROUTE_E2, i2 - N_GROUPS), (ROUTE_R1, r1), (ROUTE_R2, r2),
                          (ROUTE_W1, p_g * w1), (ROUTE_W2, p_g * w2)):
            rec = jnp.where(lane == slot, val, rec)
        route_ref[rows, :] = rec
    run_ref[...] = running
    cnt_ref[...] = running


def _out_proj(o_gla, o_fox, h0, w_out, g2, w_router, b_router):
    t = h0.shape[0]
    row = lambda i: (i, 0)
    fixed = lambda i: (0, 0)
    return pl.pallas_call(
        _out_proj_kernel,
        grid=(t // ROW_TILE,),
        in_specs=[
            pl.BlockSpec((ROW_TILE, GLA_DV_TOT), row),
            pl.BlockSpec((ROW_TILE, FOX_W), row),
            pl.BlockSpec((ROW_TILE, D_MODEL), row),
            pl.BlockSpec((D_MODEL, D_MODEL), fixed, pipeline_mode=pl.Buffered(1)),
            pl.BlockSpec((1, D_MODEL), fixed),
            pl.BlockSpec((D_MODEL, LANES), fixed),
            pl.BlockSpec((1, LANES), fixed),
        ],
        out_specs=[
            pl.BlockSpec((ROW_TILE, D_MODEL), row),
            pl.BlockSpec((ROW_TILE, D_MODEL), row),
            pl.BlockSpec((ROW_TILE, LANES), row),
            pl.BlockSpec((1, LANES), fixed),
        ],
        out_shape=[
            jax.ShapeDtypeStruct((t, D_MODEL), F32),
            jax.ShapeDtypeStruct((t, D_MODEL), BF16),
            jax.ShapeDtypeStruct((t, LANES), F32),
            jax.ShapeDtypeStruct((1, LANES), F32),
        ],
        scratch_shapes=[pltpu.VMEM((ROW_TILE // 2, ROW_TILE // 2), BF16), pltpu.VMEM((1, LANES), F32)],
        compiler_params=_params(("arbitrary",), 48),
    )(o_gla, o_fox, h0, w_out, g2, w_router, b_router)


def _scatter_kernel(pos1_ref, pos2_ref, last_ref, has_ref, nv_ref, u2_ref, xs_ref, rows_ref, zero_ref, sems, zsem):
    i = pl.program_id(0)
    n_steps = pl.num_programs(0)
    tile_rows = EXPERT_TILE * ROW_SUBLANES
    n_tiles = xs_ref.shape[0] // tile_rows

    def zero_copy(start):
        start = pl.multiple_of(start * ROW_SUBLANES, tile_rows)
        return pltpu.make_async_copy(zero_ref, xs_ref.at[pl.ds(start, tile_rows), :], zsem)

    def for_each_zero_tile(action):
        for e in range(N_EXPERTS):
            @pl.when(has_ref[e] > 0)
            def _():
                action(zero_copy(last_ref[e]))

        def unused_tile(j, carry):
            action(zero_copy(j * EXPERT_TILE))
            return carry

        lax.fori_loop(nv_ref[0], n_tiles, unused_tile, 0)

    @pl.when(i == 0)
    def _():
        zero_ref[...] = jnp.zeros_like(zero_ref)
        for_each_zero_tile(lambda cp: cp.start())
        for_each_zero_tile(lambda cp: cp.wait())

    def row_copy(step, r, pos_ref):
        slot = step % 2
        src = pl.multiple_of(r * ROW_SUBLANES, ROW_SUBLANES)
        dst = pl.multiple_of(pos_ref[step * ROW_TILE + r] * ROW_SUBLANES, ROW_SUBLANES)
        return pltpu.make_async_copy(rows_ref.at[slot, pl.ds(src, ROW_SUBLANES), :],
                                     xs_ref.at[pl.ds(dst, ROW_SUBLANES), :], sems.at[slot])

    def for_each_row(step, action):
        def group(j, carry):
            for k in range(DMA_UNROLL):
                action(row_copy(step, j * DMA_UNROLL + k, pos1_ref))
                action(row_copy(step, j * DMA_UNROLL + k, pos2_ref))
            return carry

        lax.fori_loop(0, ROW_TILE // DMA_UNROLL, group, 0)

    _rows_store(rows_ref.at[i % 2], u2_ref[...].astype(F32), ROW_SUBLANES)
    for_each_row(i, lambda cp: cp.start())

    @pl.when(i > 0)
    def _():
        for_each_row(i - 1, lambda cp: cp.wait())

    @pl.when(i == n_steps - 1)
    def _():
        for_each_row(i, lambda cp: cp.wait())


def _scatter(pos1, pos2, last_row, has, n_valid, u2, n_rows):
    t = u2.shape[0]
    return pl.pallas_call(
        _scatter_kernel,
        grid_spec=pltpu.PrefetchScalarGridSpec(
            num_scalar_prefetch=5,
            grid=(t // ROW_TILE,),
            in_specs=[pl.BlockSpec((ROW_TILE, D_MODEL), lambda i, *_: (i, 0))],
            out_specs=pl.BlockSpec(memory_space=pl.ANY),
            scratch_shapes=[pltpu.VMEM((2, ROW_TILE * ROW_SUBLANES, LANES), F32),
                            pltpu.VMEM((EXPERT_TILE * ROW_SUBLANES, LANES), F32),
                            pltpu.SemaphoreType.DMA((2,)), pltpu.SemaphoreType.DMA],
        ),
        out_shape=jax.ShapeDtypeStruct((n_rows * ROW_SUBLANES, LANES), F32),
        compiler_params=_params(("arbitrary",), 40),
    )(pos1, pos2, last_row, has, n_valid, u2)


def _moe_kernel(te_ref, first_ref, next_ref, nv_ref, xs_ref, wg_hbm, wu_hbm, wd_hbm, y_ref,
                wg_stage, wu_stage, wd_stage, wgb_ref, wub_ref, wdb_ref, sems):
    i = pl.program_id(0)

    def weight_copies(expert):
        return (pltpu.make_async_copy(wg_hbm.at[expert], wg_stage, sems.at[0]),
                pltpu.make_async_copy(wu_hbm.at[expert], wu_stage, sems.at[1]),
                pltpu.make_async_copy(wd_hbm.at[expert], wd_stage, sems.at[2]))

    @pl.when(i >= nv_ref[0])
    def _():
        y_ref[...] = jnp.zeros_like(y_ref)

    @pl.when(i < nv_ref[0])
    def _():
        @pl.when(first_ref[i] == 1)
        def _():
            @pl.when(i == 0)
            def _():
                for cp in weight_copies(te_ref[i]):
                    cp.start()

            for cp in weight_copies(te_ref[i]):
                cp.wait()
            wgb_ref[...] = wg_stage[...].astype(BF16)
            wub_ref[...] = wu_stage[...].astype(BF16)
            wdb_ref[...] = wd_stage[...].astype(BF16)

            @pl.when(next_ref[i] >= 0)
            def _():
                for cp in weight_copies(next_ref[i]):
                    cp.start()

        x = _rows_load(xs_ref, EXPERT_TILE, ROW_SUBLANES).astype(BF16)
        hg = jnp.dot(x, wgb_ref[...], preferred_element_type=F32)
        hu = jnp.dot(x, wub_ref[...], preferred_element_type=F32)
        hm = (hg * _sigmoid(hg) * hu).astype(BF16)
        y = jnp.dot(hm, wdb_ref[...], preferred_element_type=F32)
        _rows_store(y_ref, y, ROW_SUBLANES)


def _moe(tile_expert, tile_first, tile_next, n_valid, xs, w_g, w_u, w_d):
    n_rows = xs.shape[0] // ROW_SUBLANES
    row = lambda i, *_: (i, 0)
    return pl.pallas_call(
        _moe_kernel,
        grid_spec=pltpu.PrefetchScalarGridSpec(
            num_scalar_prefetch=4,
            grid=(n_rows // EXPERT_TILE,),
            in_specs=[
                pl.BlockSpec((EXPERT_TILE * ROW_SUBLANES, LANES), row),
                pl.BlockSpec(memory_space=pl.ANY),
                pl.BlockSpec(memory_space=pl.ANY),
                pl.BlockSpec(memory_space=pl.ANY),
            ],
            out_specs=pl.BlockSpec((EXPERT_TILE * ROW_SUBLANES, LANES), row),
            scratch_shapes=[pltpu.VMEM((D_MODEL, D_EXPERT), F32), pltpu.VMEM((D_MODEL, D_EXPERT), F32),
                            pltpu.VMEM((D_EXPERT, D_MODEL), F32),
                            pltpu.VMEM((D_MODEL, D_EXPERT), BF16), pltpu.VMEM((D_MODEL, D_EXPERT), BF16),
                            pltpu.VMEM((D_EXPERT, D_MODEL), BF16),
                            pltpu.SemaphoreType.DMA((3,))],
        ),
        out_shape=jax.ShapeDtypeStruct((n_rows * ROW_SUBLANES, LANES), F32),
        compiler_params=_params(("arbitrary",), 48),
    )(tile_expert, tile_first, tile_next, n_valid, xs, w_g, w_u, w_d)


def _combine_kernel(pos1_ref, pos2_ref, h1_ref, route_ref, gf_ref, y_ref, out_ref, ya_ref, yb_ref, sems):
    i = pl.program_id(0)
    n_steps = pl.num_programs(0)

    def row_copy(tile, r, pos_ref, buf_ref):
        slot = tile % 2
        src = pl.multiple_of(pos_ref[tile * OUT_TILE + r] * ROW_SUBLANES, ROW_SUBLANES)
        dst = pl.multiple_of(r * ROW_SUBLANES, ROW_SUBLANES)
        return pltpu.make_async_copy(y_ref.at[pl.ds(src, ROW_SUBLANES), :],
                                     buf_ref.at[slot, pl.ds(dst, ROW_SUBLANES), :], sems.at[slot])

    def for_each_row(tile, action):
        def group(j, carry):
            for k in range(DMA_UNROLL):
                action(row_copy(tile, j * DMA_UNROLL + k, pos1_ref, ya_ref))
                action(row_copy(tile, j * DMA_UNROLL + k, pos2_ref, yb_ref))
            return carry

        lax.fori_loop(0, OUT_TILE // DMA_UNROLL, group, 0)

    @pl.when(i == 0)
    def _():
        for_each_row(i, lambda cp: cp.start())

    @pl.when(i + 1 < n_steps)
    def _():
        for_each_row(i + 1, lambda cp: cp.start())

    for_each_row(i, lambda cp: cp.wait())
    slot = i % 2
    rec = route_ref[...]
    ya = _rows_load(ya_ref.at[slot], OUT_TILE, ROW_SUBLANES)
    yb = _rows_load(yb_ref.at[slot], OUT_TILE, ROW_SUBLANES)
    hh = h1_ref[...] + rec[:, ROUTE_W1:ROUTE_W1 + 1] * ya + rec[:, ROUTE_W2:ROUTE_W2 + 1] * yb
    ms = jnp.mean(hh * hh, axis=-1, keepdims=True)
    out_ref[...] = hh * lax.rsqrt(ms + EPS) * gf_ref[...]


def _combine(pos1, pos2, h1, route, g_f, y):
    t = h1.shape[0]
    row = lambda i, *_: (i, 0)
    return pl.pallas_call(
        _combine_kernel,
        grid_spec=pltpu.PrefetchScalarGridSpec(
            num_scalar_prefetch=2,
            grid=(t // OUT_TILE,),
            in_specs=[
                pl.BlockSpec((OUT_TILE, D_MODEL), row),
                pl.BlockSpec((OUT_TILE, LANES), row),
                pl.BlockSpec((1, D_MODEL), lambda i, *_: (0, 0)),
                pl.BlockSpec(memory_space=pl.ANY),
            ],
            out_specs=pl.BlockSpec((OUT_TILE, D_MODEL), row),
            scratch_shapes=[pltpu.VMEM((2, OUT_TILE * ROW_SUBLANES, LANES), F32),
                            pltpu.VMEM((2, OUT_TILE * ROW_SUBLANES, LANES), F32),
                            pltpu.SemaphoreType.DMA((2,))],
        ),
        out_shape=jax.ShapeDtypeStruct((t, D_MODEL), F32),
        compiler_params=_params(("arbitrary",), 40),
    )(pos1, pos2, h1, route, g_f, y)


def _routing_tables(route, cnt, n_tiles):
    counts = cnt[0, N_GROUPS:N_GROUPS + N_EXPERTS].astype(jnp.int32)
    tiles = (counts + EXPERT_TILE - 1) // EXPERT_TILE
    tile_end = jnp.cumsum(tiles)
    row_start = (tile_end - tiles) * EXPERT_TILE
    n_valid = tile_end[-1]
    expert_ids = jnp.arange(N_EXPERTS, dtype=jnp.int32)

    def positions(expert_lane, rank_lane):
        expert = route[:, expert_lane].astype(jnp.int32)
        start = jnp.sum(jnp.where(expert[:, None] == expert_ids[None, :], row_start[None, :], 0), axis=1)
        return start + route[:, rank_lane].astype(jnp.int32)

    pos1 = positions(ROUTE_E1, ROUTE_R1)
    pos2 = positions(ROUTE_E2, ROUTE_R2)
    tile = jnp.minimum(jnp.arange(n_tiles, dtype=jnp.int32), n_valid - 1)
    tile_expert = jnp.sum(tile[:, None] >= tile_end[None, :], axis=-1).astype(jnp.int32)
    owner = tile_expert[:, None] == expert_ids[None, :]
    tile_first = (tile == jnp.sum(jnp.where(owner, (tile_end - tiles)[None, :], 0), axis=1)).astype(jnp.int32)
    later = (expert_ids[None, :] > expert_ids[:, None]) & (tiles[None, :] > 0)
    next_expert = jnp.min(jnp.where(later, expert_ids[None, :], N_EXPERTS), axis=1)
    next_expert = jnp.where(next_expert == N_EXPERTS, -1, next_expert)
    tile_next = jnp.sum(jnp.where(owner, next_expert[None, :], 0), axis=1).astype(jnp.int32)
    last_row = (tile_end - 1) * EXPERT_TILE
    return pos1, pos2, tile_expert, tile_first, tile_next, n_valid.reshape(1), last_row, tiles


def kernel(x, meta_tokens, norm1_g, w_in, b_fox_f, gla_w_gate2, gla_b_gate, gla_norm_g, fox_norm_g, w_out,
           norm2_g, w_router_group, b_router_group, w_router_expert, b_router_expert, w_exp_gate, w_exp_up,
           w_exp_down, norm_f_g):
    batch, seq, _ = x.shape
    assert batch == 1 and norm1_g.shape[0] == 1
    assert seq % FOX_TILE == 0 and seq % ROW_TILE == 0 and (HEAD_ROWS + seq) % BIAS_BLOCK == 0
    t = HEAD_ROWS + seq
    x2 = x[0]
    head = jnp.concatenate([jnp.zeros((PROJ_SKIP + PAD_FRONT, D_MODEL), F32), meta_tokens.astype(F32)], axis=0)

    assert w_in.shape == (1, D_MODEL, D_IN_PROJ) and PROJ_ALIGNED % PROJ_STAGE_COLS == 0
    proj, small = _in_proj(head, x2, norm1_g, w_in[0].T)

    negc = _fox_bias(small, b_fox_f[0].reshape(FOX_HEADS, 1))
    w2_pad = jnp.zeros((LANES, GLA_DK_TOT), F32).at[FOX_HEADS:FOX_HEADS + GLA_RANK].set(gla_w_gate2[0])
    o_gla = _gla(proj, small, w2_pad, gla_b_gate, gla_norm_g)
    o_fox = _fox(proj, negc.reshape(FOX_HEADS, 1, t), fox_norm_g)

    w_router = jnp.concatenate(
        [w_router_group[0], jnp.transpose(w_router_expert[0], (1, 0, 2)).reshape(D_MODEL, N_EXPERTS),
         jnp.zeros((D_MODEL, LANES - N_GROUPS - N_EXPERTS), F32)], axis=1).astype(BF16)
    b_router = jnp.concatenate([b_router_group[0], b_router_expert[0].reshape(-1),
                                jnp.zeros((LANES - N_GROUPS - N_EXPERTS,), F32)]).reshape(1, LANES)
    h1, u2, route, cnt = _out_proj(o_gla, o_fox, x2, w_out[0].astype(BF16), norm2_g, w_router, b_router)

    n_tiles = (2 * seq) // EXPERT_TILE + N_EXPERTS
    pos1, pos2, tile_expert, tile_first, tile_next, n_valid, last_row, tiles = _routing_tables(route, cnt, n_tiles)
    xs = _scatter(pos1, pos2, last_row, tiles, n_valid, u2, n_tiles * EXPERT_TILE)
    y = _moe(tile_expert, tile_first, tile_next, n_valid, xs,
             w_exp_gate[0].reshape(N_EXPERTS, D_MODEL, D_EXPERT),
             w_exp_up[0].reshape(N_EXPERTS, D_MODEL, D_EXPERT),
             w_exp_down[0].reshape(N_EXPERTS, D_EXPERT, D_MODEL))
    out = _combine(pos1, pos2, h1, route, norm_f_g.reshape(1, D_MODEL), y)
    return out.reshape(1, seq, D_MODEL)
```

```python
import jax
import jax.numpy as jnp
from jax import lax
from jax.experimental import pallas as pl
from jax.experimental.pallas import tpu as pltpu

D_MODEL = 2048
N_META = 16
GLA_HEADS = 4
GLA_DK = 128
GLA_DV = 256
GLA_DK_TOT = GLA_HEADS * GLA_DK
GLA_DV_TOT = GLA_HEADS * GLA_DV
GLA_RANK = 16
GLA_TAU = 16.0
GLA_CHUNK = 64
FOX_HEADS = 8
FOX_HD = 128
FOX_W = FOX_HEADS * FOX_HD
FOX_BLOCK = 128
PAD_FRONT = FOX_BLOCK - N_META
HEAD_ROWS = PAD_FRONT + N_META
N_GROUPS = 4
EXPERTS_PER_GROUP = 8
N_EXPERTS = N_GROUPS * EXPERTS_PER_GROUP
D_EXPERT = 512
EPS = 1e-6

LANES = 128
PROJ_ROWS = 2 * HEAD_ROWS
PROJ_SKIP = PROJ_ROWS - HEAD_ROWS
BIAS_BLOCK = 640
GLA_ROWS = 4 * GLA_CHUNK
FOX_TILE = 1024
FOX_ROWS = 128
FOX_KEYS = 1024
FOX_GROUP = 2
FOX_SKEW = 3
LOG2E = 1.4426950408889634
ROW_TILE = 512
EXPERT_TILE = 256
OUT_TILE = 256
MASK_VALUE = -1e30
PROJ_BIG = 3 * FOX_W + 2 * GLA_DK_TOT + 2 * GLA_DV_TOT
PROJ_FF = 3 * FOX_W
PROJ_GQ = PROJ_FF + FOX_HEADS
PROJ_GZ = PROJ_GQ + 2 * GLA_DK_TOT + 2 * GLA_DV_TOT
D_IN_PROJ = PROJ_GZ + GLA_RANK
PROJ_ALIGNED = (D_IN_PROJ // LANES) * LANES
PROJ_STAGE_COLS = 256
ROW_SUBLANES = D_MODEL // LANES
DMA_UNROLL = 8

F32 = jnp.float32
BF16 = jnp.bfloat16
NT_DIMS = (((1,), (1,)), ((), ()))
TN_DIMS = (((0,), (0,)), ((), ()))


def _log_sigmoid(x):
    return jnp.minimum(x, 0.0) - jnp.log(1.0 + jnp.exp(-jnp.abs(x)))


def _sigmoid(x):
    return 1.0 / (1.0 + jnp.exp(-x))


def _split3(x):
    hi = x.astype(BF16)
    rest = x - hi.astype(F32)
    mid = rest.astype(BF16)
    lo = (rest - mid.astype(F32)).astype(BF16)
    return hi, mid, lo


def _rows_load(ref, n_rows, n_chunks):
    return jnp.concatenate([ref[pl.ds(s, n_rows, stride=n_chunks), :] for s in range(n_chunks)], axis=1)


def _rows_store(ref, value, n_chunks):
    n_rows = value.shape[0]
    for s in range(n_chunks):
        ref[pl.ds(s, n_rows, stride=n_chunks), :] = value[:, s * LANES:(s + 1) * LANES]


def _start_row_copy(copy, queue):
    copy.start(priority=queue)


def _wait_row_copy(copy, queue):
    del queue
    copy.wait()


def _params(semantics, vmem_mb):
    return pltpu.CompilerParams(dimension_semantics=semantics, vmem_limit_bytes=vmem_mb * 1024 * 1024)


def _in_proj_kernel(head_ref, x_ref, g_ref, wt_hbm, proj_ref, small_ref, w_ref, stage_ref, tail_ref, sems):
    n_chunks = PROJ_ALIGNED // PROJ_STAGE_COLS
    n_tail = D_IN_PROJ - PROJ_ALIGNED

    def chunk_copy(c):
        return pltpu.make_async_copy(wt_hbm.at[pl.ds(c * PROJ_STAGE_COLS, PROJ_STAGE_COLS), :],
                                     stage_ref.at[c % 2], sems.at[c % 2])

    def tail_copy():
        return pltpu.make_async_copy(wt_hbm.at[pl.ds(PROJ_ALIGNED, n_tail), :],
                                     tail_ref.at[pl.ds(0, n_tail), :], sems.at[2])

    @pl.when(pl.program_id(0) == 0)
    def _():
        tail_ref[...] = jnp.zeros_like(tail_ref)
        tail_copy().start()
        chunk_copy(0).start()
        for c in range(n_chunks):
            if c + 1 < n_chunks:
                chunk_copy(c + 1).start()
            chunk_copy(c).wait()
            w_ref[:, c * PROJ_STAGE_COLS:(c + 1) * PROJ_STAGE_COLS] = stage_ref[c % 2].T.astype(BF16)
        tail_copy().wait()
        w_ref[:, PROJ_ALIGNED:] = tail_ref[...].T.astype(BF16)

    x = jnp.where(pl.program_id(0) == 0, head_ref[...], x_ref[...])
    ms = jnp.mean(x * x, axis=-1, keepdims=True)
    xn = (x * lax.rsqrt(ms + EPS) * g_ref[...]).astype(BF16)
    p = jnp.dot(xn, w_ref[...], preferred_element_type=F32)
    lane = lax.broadcasted_iota(jnp.int32, (PROJ_ROWS, LANES), 1)
    small_ref[...] = jnp.where(lane < FOX_HEADS, p[:, PROJ_FF:PROJ_FF + LANES], p[:, PROJ_ALIGNED:])
    proj_ref[...] = jnp.concatenate([p[:, :PROJ_FF], p[:, PROJ_GQ:PROJ_GZ]], axis=1).astype(BF16)


def _in_proj(head, x, g1, w_t):
    t = PROJ_ROWS + x.shape[0]
    fixed = lambda i: (0, 0)
    return pl.pallas_call(
        _in_proj_kernel,
        grid=(t // PROJ_ROWS,),
        in_specs=[
            pl.BlockSpec((PROJ_ROWS, D_MODEL), fixed),
            pl.BlockSpec((PROJ_ROWS, D_MODEL), lambda i: (jnp.maximum(i - 1, 0), 0)),
            pl.BlockSpec((1, D_MODEL), fixed),
            pl.BlockSpec(memory_space=pl.ANY),
        ],
        out_specs=[
            pl.BlockSpec((PROJ_ROWS, PROJ_BIG), lambda i: (i, 0)),
            pl.BlockSpec((PROJ_ROWS, LANES), lambda i: (i, 0)),
        ],
        out_shape=[
            jax.ShapeDtypeStruct((t, PROJ_BIG), BF16),
            jax.ShapeDtypeStruct((t, LANES), F32),
        ],
        scratch_shapes=[pltpu.VMEM((D_MODEL, PROJ_ALIGNED + LANES), BF16),
                        pltpu.VMEM((2, PROJ_STAGE_COLS, D_MODEL), F32),
                        pltpu.VMEM((LANES, D_MODEL), F32),
                        pltpu.SemaphoreType.DMA((3,))],
        compiler_params=_params(("arbitrary",), 56),
    )(head, x, g1, w_t)


def _fox_bias_kernel(small_ref, bf_ref, negc_ref):
    t = negc_ref.shape[1]
    r = lax.broadcasted_iota(jnp.int32, (BIAS_BLOCK, BIAS_BLOCK), 0)
    c = lax.broadcasted_iota(jnp.int32, (BIAS_BLOCK, BIAS_BLOCK), 1)
    upper = jnp.where(r <= c, 1.0, 0.0).astype(BF16)
    lane = lax.broadcasted_iota(jnp.int32, (FOX_HEADS, BIAS_BLOCK), 1)

    def body(b, carry):
        off = pl.multiple_of(b * BIAS_BLOCK, BIAS_BLOCK)
        valid = (off + lane) >= PAD_FRONT
        f_logit = small_ref[pl.ds(PROJ_SKIP + off, BIAS_BLOCK), :].T[0:FOX_HEADS, :]
        lf = jnp.where(valid, _log_sigmoid(f_logit + bf_ref[...]), 0.0)
        cum = sum(jnp.dot(piece, upper, preferred_element_type=F32) for piece in _split3(lf)) + carry
        negc_ref[:, pl.ds(off, BIAS_BLOCK)] = jnp.where(valid, -LOG2E * cum, MASK_VALUE)
        return cum[:, BIAS_BLOCK - 1:BIAS_BLOCK]

    lax.fori_loop(0, t // BIAS_BLOCK, body, jnp.zeros((FOX_HEADS, 1), F32))


def _fox_bias(small, b_f):
    return pl.pallas_call(
        _fox_bias_kernel,
        out_shape=jax.ShapeDtypeStruct((FOX_HEADS, small.shape[0] - PROJ_SKIP), F32),
    )(small, b_f)


def _gla_kernel(q_ref, k_ref, v_ref, r_ref, small_ref, w2_ref, bg_ref, ng_ref, o_ref, st_ref):
    i = pl.program_id(0)

    @pl.when(i == 0)
    def _():
        st_ref[...] = jnp.zeros_like(st_ref)

    z_hi, z_mid, _ = _split3(small_ref[...])
    w_hi, w_mid, _ = _split3(w2_ref[...])
    gate_logit = (jnp.dot(z_hi, w_hi, preferred_element_type=F32) + jnp.dot(z_hi, w_mid, preferred_element_type=F32)
                  + jnp.dot(z_mid, w_hi, preferred_element_type=F32) + bg_ref[...])
    g = _log_sigmoid(gate_logit) * (1.0 / GLA_TAU)
    rowid = i * GLA_ROWS + lax.broadcasted_iota(jnp.int32, (GLA_ROWS, 1), 0)
    g = jnp.where(rowid >= PROJ_SKIP + PAD_FRONT, g, 0.0)

    ci = lax.broadcasted_iota(jnp.int32, (GLA_CHUNK, GLA_CHUNK), 0)
    cj = lax.broadcasted_iota(jnp.int32, (GLA_CHUNK, GLA_CHUNK), 1)
    causal = cj <= ci
    lower = jnp.where(causal, 1.0, 0.0).astype(BF16)
    scale = GLA_DK ** -0.5
    mid = GLA_CHUNK // 2

    chunks = [slice(c * GLA_CHUNK, (c + 1) * GLA_CHUNK) for c in range(GLA_ROWS // GLA_CHUNK)]
    cums = [sum(jnp.dot(lower, piece, preferred_element_type=F32) for piece in _split3(g[rows])) for rows in chunks]

    heads = range(GLA_HEADS)
    ks = [slice(h * GLA_DK, (h + 1) * GLA_DK) for h in heads]
    vs = [slice(h * GLA_DV, (h + 1) * GLA_DV) for h in heads]

    prep = []
    for rows, b in zip(chunks, cums):
        b_mid = b[mid:mid + 1]
        b_last = b[GLA_CHUNK - 1:GLA_CHUNK]
        q = q_ref[rows, :].astype(F32) * scale
        k = k_ref[rows, :].astype(F32)
        q_intra = (q * jnp.exp(b - b_mid)).astype(BF16)
        k_intra = (k * jnp.exp(b_mid - b)).astype(BF16)
        q_inter = (q * jnp.exp(b)).astype(BF16)
        k_state = (k * jnp.exp(b_last - b)).astype(BF16)
        decay = jnp.exp(b_last)
        v = [v_ref[rows, vs[h]] for h in heads]
        a = [lax.dot_general(q_intra[:, ks[h]], k_intra[:, ks[h]], NT_DIMS, preferred_element_type=F32)
             for h in heads]
        u_t = [lax.dot_general(v[h], k_state[:, ks[h]], TN_DIMS, preferred_element_type=F32) for h in heads]
        prep.append((q_inter, decay, v, a, u_t))

    st = [st_ref[h] for h in heads]
    for rows, (q_inter, decay, v, a, u_t) in zip(chunks, prep):
        o_inter = [lax.dot_general(q_inter[:, ks[h]], st[h].astype(BF16), NT_DIMS, preferred_element_type=F32)
                   for h in heads]
        st = [decay[:, ks[h]] * st[h] + u_t[h] for h in heads]
        for h in heads:
            o = o_inter[h] + jnp.dot(jnp.where(causal, a[h], 0.0).astype(BF16), v[h], preferred_element_type=F32)
            ms = jnp.mean(o * o, axis=-1, keepdims=True)
            y = o * lax.rsqrt(ms + EPS) * ng_ref[...]
            r = r_ref[rows, vs[h]].astype(F32)
            o_ref[rows, vs[h]] = (y * (r * _sigmoid(r))).astype(BF16)
    for h in heads:
        st_ref[h] = st[h]


def _gla(proj, small, w2_pad, b_gate, norm_g):
    rows = proj.shape[0]
    assert GLA_ROWS == PROJ_SKIP + HEAD_ROWS
    q_blk = (3 * FOX_W) // GLA_DK_TOT
    v_blk = (3 * FOX_W + 2 * GLA_DK_TOT) // GLA_DV_TOT
    return pl.pallas_call(
        _gla_kernel,
        grid=(rows // GLA_ROWS,),
        in_specs=[
            pl.BlockSpec((GLA_ROWS, GLA_DK_TOT), lambda i: (i, q_blk)),
            pl.BlockSpec((GLA_ROWS, GLA_DK_TOT), lambda i: (i, q_blk + 1)),
            pl.BlockSpec((GLA_ROWS, GLA_DV_TOT), lambda i: (i, v_blk)),
            pl.BlockSpec((GLA_ROWS, GLA_DV_TOT), lambda i: (i, v_blk + 1)),
            pl.BlockSpec((GLA_ROWS, LANES), lambda i: (i, 0)),
            pl.BlockSpec((LANES, GLA_DK_TOT), lambda i: (0, 0)),
            pl.BlockSpec((1, GLA_DK_TOT), lambda i: (0, 0)),
            pl.BlockSpec((1, GLA_DV), lambda i: (0, 0)),
        ],
        out_specs=pl.BlockSpec((GLA_ROWS, GLA_DV_TOT), lambda i: (jnp.maximum(i - 1, 0), 0)),
        out_shape=jax.ShapeDtypeStruct((rows - GLA_ROWS, GLA_DV_TOT), BF16),
        scratch_shapes=[pltpu.VMEM((GLA_HEADS, GLA_DV, GLA_DK), F32)],
        compiler_params=_params(("arbitrary",), 32),
    )(proj, proj, proj, proj, small, w2_pad, b_gate, norm_g)


def _fox_kernel(q_ref, k_ref, v_ref, negc_ref, ng_ref, o_ref, qs_ref, va_ref):
    qi = pl.program_id(1)
    n_blocks = FOX_TILE // FOX_ROWS

    @pl.when(qi == 0)
    def _():
        lane = lax.broadcasted_iota(jnp.int32, (v_ref.shape[0], FOX_HD), 1)
        ones_col = jnp.where(lane == 0, 1.0, 0.0).astype(BF16)
        for hh in range(FOX_GROUP):
            va_ref[:, 2 * hh * FOX_HD:(2 * hh + 1) * FOX_HD] = v_ref[:, hh * FOX_HD:(hh + 1) * FOX_HD]
            va_ref[:, (2 * hh + 1) * FOX_HD:(2 * hh + 2) * FOX_HD] = ones_col

    units = [(hh, rb) for hh in range(FOX_GROUP) for rb in range(n_blocks)]
    q0 = pl.multiple_of(HEAD_ROWS + qi * FOX_TILE, FOX_ROWS)
    qs_ref[...] = (q_ref[pl.ds(PROJ_SKIP + q0, FOX_TILE), :].astype(F32) * (FOX_HD ** -0.5 * LOG2E)).astype(BF16)
    row = lax.broadcasted_iota(jnp.int32, (FOX_ROWS, FOX_ROWS), 0)
    col = lax.broadcasted_iota(jnp.int32, (FOX_ROWS, FOX_ROWS), 1)

    def run(state, steps):
        def scores(step):
            u, off, k0, k1, causal_tail = step
            hh, rb = units[u]
            rows = slice(rb * FOX_ROWS, (rb + 1) * FOX_ROWS)
            cols = slice(hh * FOX_HD, (hh + 1) * FOX_HD)
            s = lax.dot_general(qs_ref[rows, cols], k_ref[pl.ds(PROJ_SKIP + off + k0, k1 - k0), cols], NT_DIMS,
                                preferred_element_type=F32)
            s = s + negc_ref[hh, :, pl.ds(off + k0, k1 - k0)]
            if causal_tail:
                tail = jnp.where(col <= row, s[:, k1 - k0 - FOX_ROWS:], MASK_VALUE)
                s = tail if k1 - k0 == FOX_ROWS else jnp.concatenate([s[:, :k1 - k0 - FOX_ROWS], tail], axis=1)
            return s

        def update(step, s, state):
            u, off, k0, k1, _ = step
            hh, _ = units[u]
            m_prev, acc_prev = state[u]
            m_new = jnp.maximum(m_prev, jnp.max(s, axis=-1, keepdims=True))
            p = jnp.exp2(s - m_new).astype(BF16)
            acc_new = jnp.exp2(m_prev - m_new) * acc_prev + jnp.dot(
                p, va_ref[pl.ds(PROJ_SKIP + off + k0, k1 - k0), 2 * hh * FOX_HD:(2 * hh + 2) * FOX_HD],
                preferred_element_type=F32)
            state[u] = (m_new, acc_new)

        state = list(state)
        pending = [scores(st) for st in steps[:FOX_SKEW]]
        for j, st in enumerate(steps):
            if j + FOX_SKEW < len(steps):
                pending.append(scores(steps[j + FOX_SKEW]))
            update(st, pending[j], state)
            pending[j] = None
        return tuple(state)

    head_steps = [(u, 0, 0, HEAD_ROWS, False) for u in range(len(units))]

    def full_steps(off):
        return [(u, off, k0, k0 + FOX_KEYS, False)
                for k0 in range(0, FOX_TILE, FOX_KEYS) for u in range(len(units))]

    diag_steps = []
    for k0 in range(0, FOX_TILE, FOX_KEYS):
        for u, (_, rb) in enumerate(units):
            last = (rb + 1) * FOX_ROWS
            if last > k0:
                diag_steps.append((u, q0, k0, min(k0 + FOX_KEYS, last), last <= k0 + FOX_KEYS))

    state = tuple((jnp.full((FOX_ROWS, 1), MASK_VALUE, F32), jnp.zeros((FOX_ROWS, 2 * FOX_HD), F32))
                  for _ in units)
    state = run(state, head_steps)
    state = lax.fori_loop(
        0, qi, lambda kt, st: run(st, full_steps(pl.multiple_of(HEAD_ROWS + kt * FOX_TILE, FOX_ROWS))), state)
    state = run(state, diag_steps)
    for u, (hh, rb) in enumerate(units):
        _, acc = state[u]
        o = acc[:, :FOX_HD] / acc[:, FOX_HD:FOX_HD + 1]
        ms = jnp.mean(o * o, axis=-1, keepdims=True)
        o_ref[rb * FOX_ROWS:(rb + 1) * FOX_ROWS, hh * FOX_HD:(hh + 1) * FOX_HD] = (
            o * lax.rsqrt(ms + EPS) * ng_ref[...]).astype(BF16)


def _fox(proj, negc3, norm_g):
    rows = proj.shape[0]
    t = rows - PROJ_SKIP
    width = FOX_GROUP * FOX_HD
    k_blk = FOX_W // width
    return pl.pallas_call(
        _fox_kernel,
        grid=(FOX_HEADS // FOX_GROUP, (t - HEAD_ROWS) // FOX_TILE),
        in_specs=[
            pl.BlockSpec((rows, width), lambda g, i: (0, g)),
            pl.BlockSpec((rows, width), lambda g, i: (0, k_blk + g)),
            pl.BlockSpec((rows, width), lambda g, i: (0, 2 * k_blk + g)),
            pl.BlockSpec((FOX_GROUP, 1, t), lambda g, i: (g, 0, 0)),
            pl.BlockSpec((1, FOX_HD), lambda g, i: (0, 0)),
        ],
        out_specs=pl.BlockSpec((FOX_TILE, width), lambda g, i: (i, g)),
        out_shape=jax.ShapeDtypeStruct((t - HEAD_ROWS, FOX_W), BF16),
        scratch_shapes=[pltpu.VMEM((FOX_TILE, width), BF16), pltpu.VMEM((rows, 2 * width), BF16)],
        compiler_params=_params(("arbitrary", "arbitrary"), 56),
    )(proj, proj, proj, negc3, norm_g)


ROUTE_E1, ROUTE_E2, ROUTE_R1, ROUTE_R2, ROUTE_W1, ROUTE_W2 = range(6)


def _out_proj_kernel(og_ref, of_ref, h_ref, wo_ref, g2_ref, wr_ref, br_ref,
                     h1_ref, u2_ref, route_ref, cnt_ref, tri_ref, run_ref):
    i = pl.program_id(0)
    half = ROW_TILE // 2

    @pl.when(i == 0)
    def _():
        r = lax.broadcasted_iota(jnp.int32, (half, half), 0)
        c = lax.broadcasted_iota(jnp.int32, (half, half), 1)
        tri_ref[...] = jnp.where(c < r, 1.0, 0.0).astype(BF16)
        run_ref[...] = jnp.zeros_like(run_ref)

    lane = lax.broadcasted_iota(jnp.int32, (half, LANES), 1).astype(F32)
    ninf = -jnp.inf

    def first_max(vals):
        top = jnp.max(vals, axis=-1, keepdims=True)
        idx = jnp.min(jnp.where(vals == top, lane, float(LANES)), axis=-1, keepdims=True)
        return top, idx

    halves = [slice(0, half), slice(half, ROW_TILE)]
    h1 = [h_ref[rows, :]
          + jnp.dot(og_ref[rows, :], wo_ref[0:GLA_DV_TOT, :], preferred_element_type=F32)
          + jnp.dot(of_ref[rows, :], wo_ref[GLA_DV_TOT:, :], preferred_element_type=F32) for rows in halves]
    running = run_ref[...]
    for rows, h1_half in zip(halves, h1):
        h1_ref[rows, :] = h1_half
        ms = jnp.mean(h1_half * h1_half, axis=-1, keepdims=True)
        u2 = (h1_half * lax.rsqrt(ms + EPS) * g2_ref[...]).astype(BF16)
        u2_ref[rows, :] = u2

        logits = jnp.dot(u2, wr_ref[...], preferred_element_type=F32) + br_ref[...]
        gl = jnp.where(lane < N_GROUPS, logits, ninf)
        g_top, g_idx = first_max(gl)
        p_g = 1.0 / jnp.sum(jnp.exp(gl - g_top), axis=-1, keepdims=True)
        e_lo = N_GROUPS + EXPERTS_PER_GROUP * g_idx
        el = jnp.where((lane >= e_lo) & (lane < e_lo + EXPERTS_PER_GROUP), logits, ninf)
        top1, i1 = first_max(el)
        top2, i2 = first_max(jnp.where(lane == i1, ninf, el))
        ratio = jnp.exp(top2 - top1)
        w1 = 1.0 / (1.0 + ratio)
        w2 = ratio * w1

        is1 = lane == i1
        is2 = lane == i2
        onehot = jnp.where(is1 | is2, 1.0, 0.0)
        before = jnp.dot(tri_ref[...], onehot.astype(BF16), preferred_element_type=F32) + running
        r1 = jnp.sum(jnp.where(is1, before, 0.0), axis=-1, keepdims=True)
        r2 = jnp.sum(jnp.where(is2, before, 0.0), axis=-1, keepdims=True)
        running = running + jnp.sum(onehot, axis=0, keepdims=True)

        rec = jnp.zeros_like(logits)
        for slot, val in ((ROUTE_E1, i1 - N_GROUPS), (ROUTE_E2, i2 - N_GROUPS), (ROUTE_R1, r1), (ROUTE_R2, r2),
                          (ROUTE_W1, p_g * w1), (ROUTE_W2, p_g * w2)):
            rec = jnp.where(lane == slot, val, rec)
        route_ref[rows, :] = rec
    run_ref[...] = running
    cnt_ref[...] = running


def _out_proj(o_gla, o_fox, h0, w_out, g2, w_router, b_router):
    t = h0.shape[0]
    row = lambda i: (i, 0)
    fixed = lambda i: (0, 0)
    return pl.pallas_call(
        _out_proj_kernel,
        grid=(t // ROW_TILE,),
        in_specs=[
            pl.BlockSpec((ROW_TILE, GLA_DV_TOT), row),
            pl.BlockSpec((ROW_TILE, FOX_W), row),
            pl.BlockSpec((ROW_TILE, D_MODEL), row),
            pl.BlockSpec((D_MODEL, D_MODEL), fixed, pipeline_mode=pl.Buffered(1)),
            pl.BlockSpec((1, D_MODEL), fixed),
            pl.BlockSpec((D_MODEL, LANES), fixed),
            pl.BlockSpec((1, LANES), fixed),
        ],
        out_specs=[
            pl.BlockSpec((ROW_TILE, D_MODEL), row),
            pl.BlockSpec((ROW_TILE, D_MODEL), row),
            pl.BlockSpec((ROW_TILE, LANES), row),
            pl.BlockSpec((1, LANES), fixed),
        ],
        out_shape=[
            jax.ShapeDtypeStruct((t, D_MODEL), F32),
            jax.ShapeDtypeStruct((t, D_MODEL), BF16),
            jax.ShapeDtypeStruct((t, LANES), F32),
            jax.ShapeDtypeStruct((1, LANES), F32),
        ],
        scratch_shapes=[pltpu.VMEM((ROW_TILE // 2, ROW_TILE // 2), BF16), pltpu.VMEM((1, LANES), F32)],
        compiler_params=_params(("arbitrary",), 48),
    )(o_gla, o_fox, h0, w_out, g2, w_router, b_router)


def _scatter_kernel(pos1_ref, pos2_ref, last_ref, has_ref, nv_ref, u2_ref, xs_ref, rows_ref, zero_ref, sems, zsem):
    i = pl.program_id(0)
    n_steps = pl.num_programs(0)
    tile_rows = EXPERT_TILE * ROW_SUBLANES
    n_tiles = xs_ref.shape[0] // tile_rows

    def zero_copy(start):
        start = pl.multiple_of(start * ROW_SUBLANES, tile_rows)
        return pltpu.make_async_copy(zero_ref, xs_ref.at[pl.ds(start, tile_rows), :], zsem)

    def for_each_zero_tile(action):
        for e in range(N_EXPERTS):
            @pl.when(has_ref[e] > 0)
            def _():
                action(zero_copy(last_ref[e]))

        def unused_tile(j, carry):
            action(zero_copy(j * EXPERT_TILE))
            return carry

        lax.fori_loop(nv_ref[0], n_tiles, unused_tile, 0)

    @pl.when(i == 0)
    def _():
        zero_ref[...] = jnp.zeros_like(zero_ref)
        for_each_zero_tile(lambda cp: cp.start())
        for_each_zero_tile(lambda cp: cp.wait())

    def row_copy(step, r, pos_ref):
        slot = step % 2
        src = pl.multiple_of(r * ROW_SUBLANES, ROW_SUBLANES)
        dst = pl.multiple_of(pos_ref[step * ROW_TILE + r] * ROW_SUBLANES, ROW_SUBLANES)
        return pltpu.make_async_copy(rows_ref.at[slot, pl.ds(src, ROW_SUBLANES), :],
                                     xs_ref.at[pl.ds(dst, ROW_SUBLANES), :], sems.at[slot])

    def for_each_row(step, action):
        def group(j, carry):
            for k in range(DMA_UNROLL):
                action(row_copy(step, j * DMA_UNROLL + k, pos1_ref), 0)
                action(row_copy(step, j * DMA_UNROLL + k, pos2_ref), 1)
            return carry

        lax.fori_loop(0, ROW_TILE // DMA_UNROLL, group, 0)

    _rows_store(rows_ref.at[i % 2], u2_ref[...].astype(F32), ROW_SUBLANES)
    for_each_row(i, _start_row_copy)

    @pl.when(i > 0)
    def _():
        for_each_row(i - 1, _wait_row_copy)

    @pl.when(i == n_steps - 1)
    def _():
        for_each_row(i, _wait_row_copy)


def _scatter(pos1, pos2, last_row, has, n_valid, u2, n_rows):
    t = u2.shape[0]
    return pl.pallas_call(
        _scatter_kernel,
        grid_spec=pltpu.PrefetchScalarGridSpec(
            num_scalar_prefetch=5,
            grid=(t // ROW_TILE,),
            in_specs=[pl.BlockSpec((ROW_TILE, D_MODEL), lambda i, *_: (i, 0))],
            out_specs=pl.BlockSpec(memory_space=pl.ANY),
            scratch_shapes=[pltpu.VMEM((2, ROW_TILE * ROW_SUBLANES, LANES), F32),
                            pltpu.VMEM((EXPERT_TILE * ROW_SUBLANES, LANES), F32),
                            pltpu.SemaphoreType.DMA((2,)), pltpu.SemaphoreType.DMA],
        ),
        out_shape=jax.ShapeDtypeStruct((n_rows * ROW_SUBLANES, LANES), F32),
        compiler_params=_params(("arbitrary",), 40),
    )(pos1, pos2, last_row, has, n_valid, u2)


def _moe_kernel(te_ref, first_ref, next_ref, nv_ref, xs_ref, wg_hbm, wu_hbm, wd_hbm, y_ref,
                wg_stage, wu_stage, wd_stage, wgb_ref, wub_ref, wdb_ref, sems):
    i = pl.program_id(0)

    def weight_copies(expert):
        return (pltpu.make_async_copy(wg_hbm.at[expert], wg_stage, sems.at[0]),
                pltpu.make_async_copy(wu_hbm.at[expert], wu_stage, sems.at[1]),
                pltpu.make_async_copy(wd_hbm.at[expert], wd_stage, sems.at[2]))

    @pl.when(i >= nv_ref[0])
    def _():
        y_ref[...] = jnp.zeros_like(y_ref)

    @pl.when(i < nv_ref[0])
    def _():
        @pl.when(first_ref[i] == 1)
        def _():
            @pl.when(i == 0)
            def _():
                for cp in weight_copies(te_ref[i]):
                    cp.start()

            for cp in weight_copies(te_ref[i]):
                cp.wait()
            wgb_ref[...] = wg_stage[...].astype(BF16)
            wub_ref[...] = wu_stage[...].astype(BF16)
            wdb_ref[...] = wd_stage[...].astype(BF16)

            @pl.when(next_ref[i] >= 0)
            def _():
                for cp in weight_copies(next_ref[i]):
                    cp.start()

        x = _rows_load(xs_ref, EXPERT_TILE, ROW_SUBLANES).astype(BF16)
        hg = jnp.dot(x, wgb_ref[...], preferred_element_type=F32)
        hu = jnp.dot(x, wub_ref[...], preferred_element_type=F32)
        hm = (hg * _sigmoid(hg) * hu).astype(BF16)
        y = jnp.dot(hm, wdb_ref[...], preferred_element_type=F32)
        _rows_store(y_ref, y, ROW_SUBLANES)


def _moe(tile_expert, tile_first, tile_next, n_valid, xs, w_g, w_u, w_d):
    n_rows = xs.shape[0] // ROW_SUBLANES
    row = lambda i, *_: (i, 0)
    return pl.pallas_call(
        _moe_kernel,
        grid_spec=pltpu.PrefetchScalarGridSpec(
            num_scalar_prefetch=4,
            grid=(n_rows // EXPERT_TILE,),
            in_specs=[
                pl.BlockSpec((EXPERT_TILE * ROW_SUBLANES, LANES), row),
                pl.BlockSpec(memory_space=pl.ANY),
                pl.BlockSpec(memory_space=pl.ANY),
                pl.BlockSpec(memory_space=pl.ANY),
            ],
            out_specs=pl.BlockSpec((EXPERT_TILE * ROW_SUBLANES, LANES), row),
            scratch_shapes=[pltpu.VMEM((D_MODEL, D_EXPERT), F32), pltpu.VMEM((D_MODEL, D_EXPERT), F32),
                            pltpu.VMEM((D_EXPERT, D_MODEL), F32),
                            pltpu.VMEM((D_MODEL, D_EXPERT), BF16), pltpu.VMEM((D_MODEL, D_EXPERT), BF16),
                            pltpu.VMEM((D_EXPERT, D_MODEL), BF16),
                            pltpu.SemaphoreType.DMA((3,))],
        ),
        out_shape=jax.ShapeDtypeStruct((n_rows * ROW_SUBLANES, LANES), F32),
        compiler_params=_params(("arbitrary",), 48),
    )(tile_expert, tile_first, tile_next, n_valid, xs, w_g, w_u, w_d)


def _combine_kernel(pos1_ref, pos2_ref, h1_ref, route_ref, gf_ref, y_ref, out_ref, ya_ref, yb_ref, sems):
    i = pl.program_id(0)
    n_steps = pl.num_programs(0)

    def row_copy(tile, r, pos_ref, buf_ref):
        slot = tile % 2
        src = pl.multiple_of(pos_ref[tile * OUT_TILE + r] * ROW_SUBLANES, ROW_SUBLANES)
        dst = pl.multiple_of(r * ROW_SUBLANES, ROW_SUBLANES)
        return pltpu.make_async_copy(y_ref.at[pl.ds(src, ROW_SUBLANES), :],
                                     buf_ref.at[slot, pl.ds(dst, ROW_SUBLANES), :], sems.at[slot])

    def for_each_row(tile, action):
        def group(j, carry):
            for k in range(DMA_UNROLL):
                action(row_copy(tile, j * DMA_UNROLL + k, pos1_ref, ya_ref), 0)
                action(row_copy(tile, j * DMA_UNROLL + k, pos2_ref, yb_ref), 1)
            return carry

        lax.fori_loop(0, OUT_TILE // DMA_UNROLL, group, 0)

    @pl.when(i == 0)
    def _():
        for_each_row(i, _start_row_copy)

    @pl.when(i + 1 < n_steps)
    def _():
        for_each_row(i + 1, _start_row_copy)

    for_each_row(i, _wait_row_copy)
    slot = i % 2
    rec = route_ref[...]
    ya = _rows_load(ya_ref.at[slot], OUT_TILE, ROW_SUBLANES)
    yb = _rows_load(yb_ref.at[slot], OUT_TILE, ROW_SUBLANES)
    hh = h1_ref[...] + rec[:, ROUTE_W1:ROUTE_W1 + 1] * ya + rec[:, ROUTE_W2:ROUTE_W2 + 1] * yb
    ms = jnp.mean(hh * hh, axis=-1, keepdims=True)
    out_ref[...] = hh * lax.rsqrt(ms + EPS) * gf_ref[...]


def _combine(pos1, pos2, h1, route, g_f, y):
    t = h1.shape[0]
    row = lambda i, *_: (i, 0)
    return pl.pallas_call(
        _combine_kernel,
        grid_spec=pltpu.PrefetchScalarGridSpec(
            num_scalar_prefetch=2,
            grid=(t // OUT_TILE,),
            in_specs=[
                pl.BlockSpec((OUT_TILE, D_MODEL), row),
                pl.BlockSpec((OUT_TILE, LANES), row),
                pl.BlockSpec((1, D_MODEL), lambda i, *_: (0, 0)),
                pl.BlockSpec(memory_space=pl.ANY),
            ],
            out_specs=pl.BlockSpec((OUT_TILE, D_MODEL), row),
            scratch_shapes=[pltpu.VMEM((2, OUT_TILE * ROW_SUBLANES, LANES), F32),
                            pltpu.VMEM((2, OUT_TILE * ROW_SUBLANES, LANES), F32),
                            pltpu.SemaphoreType.DMA((2,))],
        ),
        out_shape=jax.ShapeDtypeStruct((t, D_MODEL), F32),
        compiler_params=_params(("arbitrary",), 40),
    )(pos1, pos2, h1, route, g_f, y)


def _routing_tables(route, cnt, n_tiles):
    counts = cnt[0, N_GROUPS:N_GROUPS + N_EXPERTS].astype(jnp.int32)
    tiles = (counts + EXPERT_TILE - 1) // EXPERT_TILE
    tile_end = jnp.cumsum(tiles)
    row_start = (tile_end - tiles) * EXPERT_TILE
    n_valid = tile_end[-1]
    expert_ids = jnp.arange(N_EXPERTS, dtype=jnp.int32)

    def positions(expert_lane, rank_lane):
        expert = route[:, expert_lane].astype(jnp.int32)
        start = jnp.sum(jnp.where(expert[:, None] == expert_ids[None, :], row_start[None, :], 0), axis=1)
        return start + route[:, rank_lane].astype(jnp.int32)

    pos1 = positions(ROUTE_E1, ROUTE_R1)
    pos2 = positions(ROUTE_E2, ROUTE_R2)
    tile = jnp.minimum(jnp.arange(n_tiles, dtype=jnp.int32), n_valid - 1)
    tile_expert = jnp.sum(tile[:, None] >= tile_end[None, :], axis=-1).astype(jnp.int32)
    owner = tile_expert[:, None] == expert_ids[None, :]
    tile_first = (tile == jnp.sum(jnp.where(owner, (tile_end - tiles)[None, :], 0), axis=1)).astype(jnp.int32)
    later = (expert_ids[None, :] > expert_ids[:, None]) & (tiles[None, :] > 0)
    next_expert = jnp.min(jnp.where(later, expert_ids[None, :], N_EXPERTS), axis=1)
    next_expert = jnp.where(next_expert == N_EXPERTS, -1, next_expert)
    tile_next = jnp.sum(jnp.where(owner, next_expert[None, :], 0), axis=1).astype(jnp.int32)
    last_row = (tile_end - 1) * EXPERT_TILE
    return pos1, pos2, tile_expert, tile_first, tile_next, n_valid.reshape(1), last_row, tiles


def kernel(x, meta_tokens, norm1_g, w_in, b_fox_f, gla_w_gate2, gla_b_gate, gla_norm_g, fox_norm_g, w_out,
           norm2_g, w_router_group, b_router_group, w_router_expert, b_router_expert, w_exp_gate, w_exp_up,
           w_exp_down, norm_f_g):
    batch, seq, _ = x.shape
    assert batch == 1 and norm1_g.shape[0] == 1
    assert seq % FOX_TILE == 0 and seq % ROW_TILE == 0 and (HEAD_ROWS + seq) % BIAS_BLOCK == 0
    t = HEAD_ROWS + seq
    x2 = x[0]
    head = jnp.concatenate([jnp.zeros((PROJ_SKIP + PAD_FRONT, D_MODEL), F32), meta_tokens.astype(F32)], axis=0)

    assert w_in.shape == (1, D_MODEL, D_IN_PROJ) and PROJ_ALIGNED % PROJ_STAGE_COLS == 0
    proj, small = _in_proj(head, x2, norm1_g, w_in[0].T)

    negc = _fox_bias(small, b_fox_f[0].reshape(FOX_HEADS, 1))
    w2_pad = jnp.zeros((LANES, GLA_DK_TOT), F32).at[FOX_HEADS:FOX_HEADS + GLA_RANK].set(gla_w_gate2[0])
    o_gla = _gla(proj, small, w2_pad, gla_b_gate, gla_norm_g)
    o_fox = _fox(proj, negc.reshape(FOX_HEADS, 1, t), fox_norm_g)

    w_router = jnp.concatenate(
        [w_router_group[0], jnp.transpose(w_router_expert[0], (1, 0, 2)).reshape(D_MODEL, N_EXPERTS),
         jnp.zeros((D_MODEL, LANES - N_GROUPS - N_EXPERTS), F32)], axis=1).astype(BF16)
    b_router = jnp.concatenate([b_router_group[0], b_router_expert[0].reshape(-1),
                                jnp.zeros((LANES - N_GROUPS - N_EXPERTS,), F32)]).reshape(1, LANES)
    h1, u2, route, cnt = _out_proj(o_gla, o_fox, x2, w_out[0].astype(BF16), norm2_g, w_router, b_router)

    n_tiles = (2 * seq) // EXPERT_TILE + N_EXPERTS
    pos1, pos2, tile_expert, tile_first, tile_next, n_valid, last_row, tiles = _routing_tables(route, cnt, n_tiles)
    xs = _scatter(pos1, pos2, last_row, tiles, n_valid, u2, n_tiles * EXPERT_TILE)
    y = _moe(tile_expert, tile_first, tile_next, n_valid, xs,
             w_exp_gate[0].reshape(N_EXPERTS, D_MODEL, D_EXPERT),
             w_exp_up[0].reshape(N_EXPERTS, D_MODEL, D_EXPERT),
             w_exp_down[0].reshape(N_EXPERTS, D_EXPERT, D_MODEL))
    out = _combine(pos1, pos2, h1, route, norm_f_g.reshape(1, D_MODEL), y)
    return out.reshape(1, seq, D_MODEL)
```

```python
import jax
import jax.numpy as jnp
from jax import lax
from jax.experimental import pallas as pl
from jax.experimental.pallas import tpu as pltpu

D_MODEL = 2048
N_META = 16
GLA_HEADS = 4
GLA_DK = 128
GLA_DV = 256
GLA_DK_TOT = GLA_HEADS * GLA_DK
GLA_DV_TOT = GLA_HEADS * GLA_DV
GLA_RANK = 16
GLA_TAU = 16.0
GLA_CHUNK = 64
FOX_HEADS = 8
FOX_HD = 128
FOX_W = FOX_HEADS * FOX_HD
FOX_BLOCK = 128
PAD_FRONT = FOX_BLOCK - N_META
HEAD_ROWS = PAD_FRONT + N_META
N_GROUPS = 4
EXPERTS_PER_GROUP = 8
N_EXPERTS = N_GROUPS * EXPERTS_PER_GROUP
D_EXPERT = 512
EPS = 1e-6

LANES = 128
PROJ_ROWS = 2 * HEAD_ROWS
PROJ_SKIP = PROJ_ROWS - HEAD_ROWS
BIAS_BLOCK = 640
GLA_ROWS = 4 * GLA_CHUNK
FOX_TILE = 1024
FOX_ROWS = 128
FOX_KEYS = 1024
FOX_GROUP = 2
FOX_SKEW = 3
LOG2E = 1.4426950408889634
ROW_TILE = 512
EXPERT_TILE = 256
OUT_TILE = 256
MASK_VALUE = -1e30
PROJ_BIG = 3 * FOX_W + 2 * GLA_DK_TOT + 2 * GLA_DV_TOT
PROJ_FF = 3 * FOX_W
PROJ_GQ = PROJ_FF + FOX_HEADS
PROJ_GZ = PROJ_GQ + 2 * GLA_DK_TOT + 2 * GLA_DV_TOT
D_IN_PROJ = PROJ_GZ + GLA_RANK
PROJ_ALIGNED = (D_IN_PROJ // LANES) * LANES
PROJ_STAGE_COLS = 256
ROW_SUBLANES = D_MODEL // LANES
DMA_UNROLL = 8

F32 = jnp.float32
BF16 = jnp.bfloat16
NT_DIMS = (((1,), (1,)), ((), ()))
TN_DIMS = (((0,), (0,)), ((), ()))


def _log_sigmoid(x):
    return jnp.minimum(x, 0.0) - jnp.log(1.0 + jnp.exp(-jnp.abs(x)))


def _sigmoid(x):
    return 1.0 / (1.0 + jnp.exp(-x))


def _split3(x):
    hi = x.astype(BF16)
    rest = x - hi.astype(F32)
    mid = rest.astype(BF16)
    lo = (rest - mid.astype(F32)).astype(BF16)
    return hi, mid, lo


def _rows_load(ref, n_rows, n_chunks):
    return jnp.concatenate([ref[pl.ds(s, n_rows, stride=n_chunks), :] for s in range(n_chunks)], axis=1)


def _rows_store(ref, value, n_chunks):
    n_rows = value.shape[0]
    for s in range(n_chunks):
        ref[pl.ds(s, n_rows, stride=n_chunks), :] = value[:, s * LANES:(s + 1) * LANES]


def _params(semantics, vmem_mb):
    return pltpu.CompilerParams(dimension_semantics=semantics, vmem_limit_bytes=vmem_mb * 1024 * 1024)


def _in_proj_kernel(head_ref, x_ref, g_ref, wt_hbm, proj_ref, small_ref, w_ref, stage_ref, tail_ref, sems):
    n_chunks = PROJ_ALIGNED // PROJ_STAGE_COLS
    n_tail = D_IN_PROJ - PROJ_ALIGNED

    def chunk_copy(c):
        return pltpu.make_async_copy(wt_hbm.at[pl.ds(c * PROJ_STAGE_COLS, PROJ_STAGE_COLS), :],
                                     stage_ref.at[c % 2], sems.at[c % 2])

    def tail_copy():
        return pltpu.make_async_copy(wt_hbm.at[pl.ds(PROJ_ALIGNED, n_tail), :],
                                     tail_ref.at[pl.ds(0, n_tail), :], sems.at[2])

    @pl.when(pl.program_id(0) == 0)
    def _():
        tail_ref[...] = jnp.zeros_like(tail_ref)
        tail_copy().start()
        chunk_copy(0).start()
        for c in range(n_chunks):
            if c + 1 < n_chunks:
                chunk_copy(c + 1).start()
            chunk_copy(c).wait()
            w_ref[:, c * PROJ_STAGE_COLS:(c + 1) * PROJ_STAGE_COLS] = stage_ref[c % 2].T.astype(BF16)
        tail_copy().wait()
        w_ref[:, PROJ_ALIGNED:] = tail_ref[...].T.astype(BF16)

    x = jnp.where(pl.program_id(0) == 0, head_ref[...], x_ref[...])
    ms = jnp.mean(x * x, axis=-1, keepdims=True)
    xn = (x * lax.rsqrt(ms + EPS) * g_ref[...]).astype(BF16)
    p = jnp.dot(xn, w_ref[...], preferred_element_type=F32)
    lane = lax.broadcasted_iota(jnp.int32, (PROJ_ROWS, LANES), 1)
    small_ref[...] = jnp.where(lane < FOX_HEADS, p[:, PROJ_FF:PROJ_FF + LANES], p[:, PROJ_ALIGNED:])
    proj_ref[...] = jnp.concatenate([p[:, :PROJ_FF], p[:, PROJ_GQ:PROJ_GZ]], axis=1).astype(BF16)


def _in_proj(head, x, g1, w_t):
    t = PROJ_ROWS + x.shape[0]
    fixed = lambda i: (0, 0)
    return pl.pallas_call(
        _in_proj_kernel,
        grid=(t // PROJ_ROWS,),
        in_specs=[
            pl.BlockSpec((PROJ_ROWS, D_MODEL), fixed),
            pl.BlockSpec((PROJ_ROWS, D_MODEL), lambda i: (jnp.maximum(i - 1, 0), 0)),
            pl.BlockSpec((1, D_MODEL), fixed),
            pl.BlockSpec(memory_space=pl.ANY),
        ],
        out_specs=[
            pl.BlockSpec((PROJ_ROWS, PROJ_BIG), lambda i: (i, 0)),
            pl.BlockSpec((PROJ_ROWS, LANES), lambda i: (i, 0)),
        ],
        out_shape=[
            jax.ShapeDtypeStruct((t, PROJ_BIG), BF16),
            jax.ShapeDtypeStruct((t, LANES), F32),
        ],
        scratch_shapes=[pltpu.VMEM((D_MODEL, PROJ_ALIGNED + LANES), BF16),
                        pltpu.VMEM((2, PROJ_STAGE_COLS, D_MODEL), F32),
                        pltpu.VMEM((LANES, D_MODEL), F32),
                        pltpu.SemaphoreType.DMA((3,))],
        compiler_params=_params(("arbitrary",), 56),
    )(head, x, g1, w_t)


def _fox_bias_kernel(small_ref, bf_ref, negc_ref):
    t = negc_ref.shape[1]
    r = lax.broadcasted_iota(jnp.int32, (BIAS_BLOCK, BIAS_BLOCK), 0)
    c = lax.broadcasted_iota(jnp.int32, (BIAS_BLOCK, BIAS_BLOCK), 1)
    upper = jnp.where(r <= c, 1.0, 0.0).astype(BF16)
    lane = lax.broadcasted_iota(jnp.int32, (FOX_HEADS, BIAS_BLOCK), 1)

    def body(b, carry):
        off = pl.multiple_of(b * BIAS_BLOCK, BIAS_BLOCK)
        valid = (off + lane) >= PAD_FRONT
        f_logit = small_ref[pl.ds(PROJ_SKIP + off, BIAS_BLOCK), :].T[0:FOX_HEADS, :]
        lf = jnp.where(valid, _log_sigmoid(f_logit + bf_ref[...]), 0.0)
        cum = sum(jnp.dot(piece, upper, preferred_element_type=F32) for piece in _split3(lf)) + carry
        negc_ref[:, pl.ds(off, BIAS_BLOCK)] = jnp.where(valid, -LOG2E * cum, MASK_VALUE)
        return cum[:, BIAS_BLOCK - 1:BIAS_BLOCK]

    lax.fori_loop(0, t // BIAS_BLOCK, body, jnp.zeros((FOX_HEADS, 1), F32))


def _fox_bias(small, b_f):
    return pl.pallas_call(
        _fox_bias_kernel,
        out_shape=jax.ShapeDtypeStruct((FOX_HEADS, small.shape[0] - PROJ_SKIP), F32),
    )(small, b_f)


def _gla_kernel(q_ref, k_ref, v_ref, r_ref, small_ref, w2_ref, bg_ref, ng_ref, o_ref, st_ref):
    i = pl.program_id(0)

    @pl.when(i == 0)
    def _():
        st_ref[...] = jnp.zeros_like(st_ref)

    z_hi, z_mid, _ = _split3(small_ref[...])
    w_hi, w_mid, _ = _split3(w2_ref[...])
    gate_logit = (jnp.dot(z_hi, w_hi, preferred_element_type=F32) + jnp.dot(z_hi, w_mid, preferred_element_type=F32)
                  + jnp.dot(z_mid, w_hi, preferred_element_type=F32) + bg_ref[...])
    g = _log_sigmoid(gate_logit) * (1.0 / GLA_TAU)
    rowid = i * GLA_ROWS + lax.broadcasted_iota(jnp.int32, (GLA_ROWS, 1), 0)
    g = jnp.where(rowid >= PROJ_SKIP + PAD_FRONT, g, 0.0)

    ci = lax.broadcasted_iota(jnp.int32, (GLA_CHUNK, GLA_CHUNK), 0)
    cj = lax.broadcasted_iota(jnp.int32, (GLA_CHUNK, GLA_CHUNK), 1)
    causal = cj <= ci
    lower = jnp.where(causal, 1.0, 0.0).astype(BF16)
    scale = GLA_DK ** -0.5
    mid = GLA_CHUNK // 2

    chunks = [slice(c * GLA_CHUNK, (c + 1) * GLA_CHUNK) for c in range(GLA_ROWS // GLA_CHUNK)]
    cums = [sum(jnp.dot(lower, piece, preferred_element_type=F32) for piece in _split3(g[rows])) for rows in chunks]

    heads = range(GLA_HEADS)
    ks = [slice(h * GLA_DK, (h + 1) * GLA_DK) for h in heads]
    vs = [slice(h * GLA_DV, (h + 1) * GLA_DV) for h in heads]

    prep = []
    for rows, b in zip(chunks, cums):
        b_mid = b[mid:mid + 1]
        b_last = b[GLA_CHUNK - 1:GLA_CHUNK]
        q = q_ref[rows, :].astype(F32) * scale
        k = k_ref[rows, :].astype(F32)
        q_intra = (q * jnp.exp(b - b_mid)).astype(BF16)
        k_intra = (k * jnp.exp(b_mid - b)).astype(BF16)
        q_inter = (q * jnp.exp(b)).astype(BF16)
        k_state = (k * jnp.exp(b_last - b)).astype(BF16)
        decay = jnp.exp(b_last)
        v = [v_ref[rows, vs[h]] for h in heads]
        a = [lax.dot_general(q_intra[:, ks[h]], k_intra[:, ks[h]], NT_DIMS, preferred_element_type=F32)
             for h in heads]
        u_t = [lax.dot_general(v[h], k_state[:, ks[h]], TN_DIMS, preferred_element_type=F32) for h in heads]
        prep.append((q_inter, decay, v, a, u_t))

    st = [st_ref[h] for h in heads]
    for rows, (q_inter, decay, v, a, u_t) in zip(chunks, prep):
        o_inter = [lax.dot_general(q_inter[:, ks[h]], st[h].astype(BF16), NT_DIMS, preferred_element_type=F32)
                   for h in heads]
        st = [decay[:, ks[h]] * st[h] + u_t[h] for h in heads]
        for h in heads:
            o = o_inter[h] + jnp.dot(jnp.where(causal, a[h], 0.0).astype(BF16), v[h], preferred_element_type=F32)
            ms = jnp.mean(o * o, axis=-1, keepdims=True)
            y = o * lax.rsqrt(ms + EPS) * ng_ref[...]
            r = r_ref[rows, vs[h]].astype(F32)
            o_ref[rows, vs[h]] = (y * (r * _sigmoid(r))).astype(BF16)
    for h in heads:
        st_ref[h] = st[h]


def _gla(proj, small, w2_pad, b_gate, norm_g):
    rows = proj.shape[0]
    assert GLA_ROWS == PROJ_SKIP + HEAD_ROWS
    q_blk = (3 * FOX_W) // GLA_DK_TOT
    v_blk = (3 * FOX_W + 2 * GLA_DK_TOT) // GLA_DV_TOT
    return pl.pallas_call(
        _gla_kernel,
        grid=(rows // GLA_ROWS,),
        in_specs=[
            pl.BlockSpec((GLA_ROWS, GLA_DK_TOT), lambda i: (i, q_blk)),
            pl.BlockSpec((GLA_ROWS, GLA_DK_TOT), lambda i: (i, q_blk + 1)),
            pl.BlockSpec((GLA_ROWS, GLA_DV_TOT), lambda i: (i, v_blk)),
            pl.BlockSpec((GLA_ROWS, GLA_DV_TOT), lambda i: (i, v_blk + 1)),
            pl.BlockSpec((GLA_ROWS, LANES), lambda i: (i, 0)),
            pl.BlockSpec((LANES, GLA_DK_TOT), lambda i: (0, 0)),
            pl.BlockSpec((1, GLA_DK_TOT), lambda i: (0, 0)),
            pl.BlockSpec((1, GLA_DV), lambda i: (0, 0)),
        ],
        out_specs=pl.BlockSpec((GLA_ROWS, GLA_DV_TOT), lambda i: (jnp.maximum(i - 1, 0), 0)),
        out_shape=jax.ShapeDtypeStruct((rows - GLA_ROWS, GLA_DV_TOT), BF16),
        scratch_shapes=[pltpu.VMEM((GLA_HEADS, GLA_DV, GLA_DK), F32)],
        compiler_params=_params(("arbitrary",), 32),
    )(proj, proj, proj, proj, small, w2_pad, b_gate, norm_g)


def _fox_kernel(q_ref, k_ref, v_ref, negc_ref, ng_ref, o_ref, qs_ref, va_ref):
    qi = pl.program_id(1)
    n_blocks = FOX_TILE // FOX_ROWS

    @pl.when(qi == 0)
    def _():
        lane = lax.broadcasted_iota(jnp.int32, (v_ref.shape[0], FOX_HD), 1)
        ones_col = jnp.where(lane == 0, 1.0, 0.0).astype(BF16)
        for hh in range(FOX_GROUP):
            va_ref[:, 2 * hh * FOX_HD:(2 * hh + 1) * FOX_HD] = v_ref[:, hh * FOX_HD:(hh + 1) * FOX_HD]
            va_ref[:, (2 * hh + 1) * FOX_HD:(2 * hh + 2) * FOX_HD] = ones_col

    units = [(hh, rb) for hh in range(FOX_GROUP) for rb in range(n_blocks)]
    q0 = pl.multiple_of(HEAD_ROWS + qi * FOX_TILE, FOX_ROWS)
    qs_ref[...] = (q_ref[pl.ds(PROJ_SKIP + q0, FOX_TILE), :].astype(F32) * (FOX_HD ** -0.5 * LOG2E)).astype(BF16)
    row = lax.broadcasted_iota(jnp.int32, (FOX_ROWS, FOX_ROWS), 0)
    col = lax.broadcasted_iota(jnp.int32, (FOX_ROWS, FOX_ROWS), 1)

    def run(state, steps):
        def scores(step):
            u, off, k0, k1, causal_tail = step
            hh, rb = units[u]
            rows = slice(rb * FOX_ROWS, (rb + 1) * FOX_ROWS)
            cols = slice(hh * FOX_HD, (hh + 1) * FOX_HD)
            s = lax.dot_general(qs_ref[rows, cols], k_ref[pl.ds(PROJ_SKIP + off + k0, k1 - k0), cols], NT_DIMS,
                                preferred_element_type=F32)
            s = s + negc_ref[hh, :, pl.ds(off + k0, k1 - k0)]
            if causal_tail:
                tail = jnp.where(col <= row, s[:, k1 - k0 - FOX_ROWS:], MASK_VALUE)
                s = tail if k1 - k0 == FOX_ROWS else jnp.concatenate([s[:, :k1 - k0 - FOX_ROWS], tail], axis=1)
            return s

        def update(step, s, state):
            u, off, k0, k1, _ = step
            hh, _ = units[u]
            m_prev, acc_prev = state[u]
            m_new = jnp.maximum(m_prev, jnp.max(s, axis=-1, keepdims=True))
            p = jnp.exp2(s - m_new).astype(BF16)
            acc_new = jnp.exp2(m_prev - m_new) * acc_prev + jnp.dot(
                p, va_ref[pl.ds(PROJ_SKIP + off + k0, k1 - k0), 2 * hh * FOX_HD:(2 * hh + 2) * FOX_HD],
                preferred_element_type=F32)
            state[u] = (m_new, acc_new)

        state = list(state)
        pending = [scores(st) for st in steps[:FOX_SKEW]]
        for j, st in enumerate(steps):
            if j + FOX_SKEW < len(steps):
                pending.append(scores(steps[j + FOX_SKEW]))
            update(st, pending[j], state)
            pending[j] = None
        return tuple(state)

    head_steps = [(u, 0, 0, HEAD_ROWS, False) for u in range(len(units))]

    def full_steps(off):
        return [(u, off, k0, k0 + FOX_KEYS, False)
                for k0 in range(0, FOX_TILE, FOX_KEYS) for u in range(len(units))]

    diag_steps = []
    for k0 in range(0, FOX_TILE, FOX_KEYS):
        for u, (_, rb) in enumerate(units):
            last = (rb + 1) * FOX_ROWS
            if last > k0:
                diag_steps.append((u, q0, k0, min(k0 + FOX_KEYS, last), last <= k0 + FOX_KEYS))

    state = tuple((jnp.full((FOX_ROWS, 1), MASK_VALUE, F32), jnp.zeros((FOX_ROWS, 2 * FOX_HD), F32))
                  for _ in units)
    state = run(state, head_steps)
    state = lax.fori_loop(
        0, qi, lambda kt, st: run(st, full_steps(pl.multiple_of(HEAD_ROWS + kt * FOX_TILE, FOX_ROWS))), state)
    state = run(state, diag_steps)
    for u, (hh, rb) in enumerate(units):
        _, acc = state[u]
        o = acc[:, :FOX_HD] / acc[:, FOX_HD:FOX_HD + 1]
        ms = jnp.mean(o * o, axis=-1, keepdims=True)
        o_ref[rb * FOX_ROWS:(rb + 1) * FOX_ROWS, hh * FOX_HD:(hh + 1) * FOX_HD] = (
            o * lax.rsqrt(ms + EPS) * ng_ref[...]).astype(BF16)


def _fox(proj, negc3, norm_g):
    rows = proj.shape[0]
    t = rows - PROJ_SKIP
    width = FOX_GROUP * FOX_HD
    k_blk = FOX_W // width
    return pl.pallas_call(
        _fox_kernel,
        grid=(FOX_HEADS // FOX_GROUP, (t - HEAD_ROWS) // FOX_TILE),
        in_specs=[
            pl.BlockSpec((rows, width), lambda g, i: (0, g)),
            pl.BlockSpec((rows, width), lambda g, i: (0, k_blk + g)),
            pl.BlockSpec((rows, width), lambda g, i: (0, 2 * k_blk + g)),
            pl.BlockSpec((FOX_GROUP, 1, t), lambda g, i: (g, 0, 0)),
            pl.BlockSpec((1, FOX_HD), lambda g, i: (0, 0)),
        ],
        out_specs=pl.BlockSpec((FOX_TILE, width), lambda g, i: (i, g)),
        out_shape=jax.ShapeDtypeStruct((t - HEAD_ROWS, FOX_W), BF16),
        scratch_shapes=[pltpu.VMEM((FOX_TILE, width), BF16), pltpu.VMEM((rows, 2 * width), BF16)],
        compiler_params=_params(("arbitrary", "arbitrary"), 56),
    )(proj, proj, proj, negc3, norm_g)


ROUTE_E1, ROUTE_E2, ROUTE_R1, ROUTE_R2, ROUTE_W1, ROUTE_W2 = range(6)


def _out_proj_kernel(og_ref, of_ref, h_ref, wo_ref, g2_ref, wr_ref, br_ref,
                     h1_ref, u2_ref, route_ref, cnt_ref, tri_ref, run_ref):
    i = pl.program_id(0)
    half = ROW_TILE // 2

    @pl.when(i == 0)
    def _():
        r = lax.broadcasted_iota(jnp.int32, (half, half), 0)
        c = lax.broadcasted_iota(jnp.int32, (half, half), 1)
        tri_ref[...] = jnp.where(c < r, 1.0, 0.0).astype(BF16)
        run_ref[...] = jnp.zeros_like(run_ref)

    lane = lax.broadcasted_iota(jnp.int32, (half, LANES), 1).astype(F32)
    ninf = -jnp.inf

    def first_max(vals):
        top = jnp.max(vals, axis=-1, keepdims=True)
        idx = jnp.min(jnp.where(vals == top, lane, float(LANES)), axis=-1, keepdims=True)
        return top, idx

    halves = [slice(0, half), slice(half, ROW_TILE)]
    h1 = [h_ref[rows, :]
          + jnp.dot(og_ref[rows, :], wo_ref[0:GLA_DV_TOT, :], preferred_element_type=F32)
          + jnp.dot(of_ref[rows, :], wo_ref[GLA_DV_TOT:, :], preferred_element_type=F32) for rows in halves]
    running = run_ref[...]
    for rows, h1_half in zip(halves, h1):
        h1_ref[rows, :] = h1_half
        ms = jnp.mean(h1_half * h1_half, axis=-1, keepdims=True)
        u2 = (h1_half * lax.rsqrt(ms + EPS) * g2_ref[...]).astype(BF16)
        u2_ref[rows, :] = u2

        logits = jnp.dot(u2, wr_ref[...], preferred_element_type=F32) + br_ref[...]
        gl = jnp.where(lane < N_GROUPS, logits, ninf)
        g_top, g_idx = first_max(gl)
        p_g = 1.0 / jnp.sum(jnp.exp(gl - g_top), axis=-1, keepdims=True)
        e_lo = N_GROUPS + EXPERTS_PER_GROUP * g_idx
        el = jnp.where((lane >= e_lo) & (lane < e_lo + EXPERTS_PER_GROUP), logits, ninf)
        top1, i1 = first_max(el)
        top2, i2 = first_max(jnp.where(lane == i1, ninf, el))
        ratio = jnp.exp(top2 - top1)
        w1 = 1.0 / (1.0 + ratio)
        w2 = ratio * w1

        is1 = lane == i1
        is2 = lane == i2
        onehot = jnp.where(is1 | is2, 1.0, 0.0)
        before = jnp.dot(tri_ref[...], onehot.astype(BF16), preferred_element_type=F32) + running
        r1 = jnp.sum(jnp.where(is1, before, 0.0), axis=-1, keepdims=True)
        r2 = jnp.sum(jnp.where(is2, before, 0.0), axis=-1, keepdims=True)
        running = running + jnp.sum(onehot, axis=0, keepdims=True)

        rec = jnp.zeros_like(logits)
        for slot, val in ((ROUTE_E1, i1 - N_GROUPS), (ROUTE_E2, i2 - N_GROUPS), (ROUTE_R1, r1), (ROUTE_R2, r2),
                          (ROUTE_W1, p_g * w1), (ROUTE_W2, p_g * w2)):
            rec = jnp.where(lane == slot, val, rec)
        route_ref[rows, :] = rec
    run_ref[...] = running
    cnt_ref[...] = running


def _out_proj(o_gla, o_fox, h0, w_out, g2, w_router, b_router):
    t = h0.shape[0]
    row = lambda i: (i, 0)
    fixed = lambda i: (0, 0)
    return pl.pallas_call(
        _out_proj_kernel,
        grid=(t // ROW_TILE,),
        in_specs=[
            pl.BlockSpec((ROW_TILE, GLA_DV_TOT), row),
            pl.BlockSpec((ROW_TILE, FOX_W), row),
            pl.BlockSpec((ROW_TILE, D_MODEL), row),
            pl.BlockSpec((D_MODEL, D_MODEL), fixed, pipeline_mode=pl.Buffered(1)),
            pl.BlockSpec((1, D_MODEL), fixed),
            pl.BlockSpec((D_MODEL, LANES), fixed),
            pl.BlockSpec((1, LANES), fixed),
        ],
        out_specs=[
            pl.BlockSpec((ROW_TILE, D_MODEL), row),
            pl.BlockSpec((ROW_TILE, D_MODEL), row),
            pl.BlockSpec((ROW_TILE, LANES), row),
            pl.BlockSpec((1, LANES), fixed),
        ],
        out_shape=[
            jax.ShapeDtypeStruct((t, D_MODEL), F32),
            jax.ShapeDtypeStruct((t, D_MODEL), BF16),
            jax.ShapeDtypeStruct((t, LANES), F32),
            jax.ShapeDtypeStruct((1, LANES), F32),
        ],
        scratch_shapes=[pltpu.VMEM((ROW_TILE // 2, ROW_TILE // 2), BF16), pltpu.VMEM((1, LANES), F32)],
        compiler_params=_params(("arbitrary",), 48),
    )(o_gla, o_fox, h0, w_out, g2, w_router, b_router)


def _scatter_kernel(pos1_ref, pos2_ref, pad_start_ref, pad_len_ref, nv_ref, u2_ref, xs_ref,
                    rows_ref, zero_ref, sems, zsem):
    i = pl.program_id(0)
    n_steps = pl.num_programs(0)
    tile_rows = EXPERT_TILE * ROW_SUBLANES
    n_tiles = xs_ref.shape[0] // tile_rows

    def zero_copy(start, n_rows):
        start = pl.multiple_of(start * ROW_SUBLANES, ROW_SUBLANES)
        return pltpu.make_async_copy(zero_ref.at[pl.ds(0, n_rows * ROW_SUBLANES), :],
                                     xs_ref.at[pl.ds(start, n_rows * ROW_SUBLANES), :], zsem)

    def for_each_zero_block(action):
        def expert(e, carry):
            start = pad_start_ref[e]
            length = pad_len_ref[e]
            for bit in reversed(range(EXPERT_TILE.bit_length() - 1)):
                n_rows = 1 << bit

                @pl.when((length & n_rows) != 0)
                def _():
                    action(zero_copy(start, n_rows))

                start = start + (length & n_rows)
            return carry

        def unused_tile(j, carry):
            action(zero_copy(j * EXPERT_TILE, EXPERT_TILE))
            return carry

        lax.fori_loop(0, N_EXPERTS, expert, 0)
        lax.fori_loop(nv_ref[0], n_tiles, unused_tile, 0)

    @pl.when(i == 0)
    def _():
        zero_ref[...] = jnp.zeros_like(zero_ref)
        for_each_zero_block(lambda cp: cp.start())

    def row_copy(step, r, pos_ref):
        slot = step % 2
        src = pl.multiple_of(r * ROW_SUBLANES, ROW_SUBLANES)
        dst = pl.multiple_of(pos_ref[step * ROW_TILE + r] * ROW_SUBLANES, ROW_SUBLANES)
        return pltpu.make_async_copy(rows_ref.at[slot, pl.ds(src, ROW_SUBLANES), :],
                                     xs_ref.at[pl.ds(dst, ROW_SUBLANES), :], sems.at[slot])

    def for_each_row(step, action):
        def group(j, carry):
            for k in range(DMA_UNROLL):
                action(row_copy(step, j * DMA_UNROLL + k, pos1_ref))
                action(row_copy(step, j * DMA_UNROLL + k, pos2_ref))
            return carry

        lax.fori_loop(0, ROW_TILE // DMA_UNROLL, group, 0)

    _rows_store(rows_ref.at[i % 2], u2_ref[...].astype(F32), ROW_SUBLANES)
    for_each_row(i, lambda cp: cp.start())

    @pl.when(i > 0)
    def _():
        for_each_row(i - 1, lambda cp: cp.wait())

    @pl.when(i == n_steps - 1)
    def _():
        for_each_row(i, lambda cp: cp.wait())
        for_each_zero_block(lambda cp: cp.wait())


def _scatter(pos1, pos2, pad_start, pad_len, n_valid, u2, n_rows):
    t = u2.shape[0]
    return pl.pallas_call(
        _scatter_kernel,
        grid_spec=pltpu.PrefetchScalarGridSpec(
            num_scalar_prefetch=5,
            grid=(t // ROW_TILE,),
            in_specs=[pl.BlockSpec((ROW_TILE, D_MODEL), lambda i, *_: (i, 0))],
            out_specs=pl.BlockSpec(memory_space=pl.ANY),
            scratch_shapes=[pltpu.VMEM((2, ROW_TILE * ROW_SUBLANES, LANES), F32),
                            pltpu.VMEM((EXPERT_TILE * ROW_SUBLANES, LANES), F32),
                            pltpu.SemaphoreType.DMA((2,)), pltpu.SemaphoreType.DMA],
        ),
        out_shape=jax.ShapeDtypeStruct((n_rows * ROW_SUBLANES, LANES), F32),
        compiler_params=_params(("arbitrary",), 40),
    )(pos1, pos2, pad_start, pad_len, n_valid, u2)


def _moe_kernel(te_ref, first_ref, next_ref, nv_ref, xs_ref, wg_hbm, wu_hbm, wd_hbm, y_ref,
                wg_stage, wu_stage, wd_stage, wgb_ref, wub_ref, wdb_ref, sems):
    i = pl.program_id(0)

    def weight_copies(expert):
        return (pltpu.make_async_copy(wg_hbm.at[expert], wg_stage, sems.at[0]),
                pltpu.make_async_copy(wu_hbm.at[expert], wu_stage, sems.at[1]),
                pltpu.make_async_copy(wd_hbm.at[expert], wd_stage, sems.at[2]))

    @pl.when(i >= nv_ref[0])
    def _():
        y_ref[...] = jnp.zeros_like(y_ref)

    @pl.when(i < nv_ref[0])
    def _():
        @pl.when(first_ref[i] == 1)
        def _():
            @pl.when(i == 0)
            def _():
                for cp in weight_copies(te_ref[i]):
                    cp.start()

            for cp in weight_copies(te_ref[i]):
                cp.wait()
            wgb_ref[...] = wg_stage[...].astype(BF16)
            wub_ref[...] = wu_stage[...].astype(BF16)
            wdb_ref[...] = wd_stage[...].astype(BF16)

            @pl.when(next_ref[i] >= 0)
            def _():
                for cp in weight_copies(next_ref[i]):
                    cp.start()

        x = _rows_load(xs_ref, EXPERT_TILE, ROW_SUBLANES).astype(BF16)
        hg = jnp.dot(x, wgb_ref[...], preferred_element_type=F32)
        hu = jnp.dot(x, wub_ref[...], preferred_element_type=F32)
        hm = (hg * _sigmoid(hg) * hu).astype(BF16)
        y = jnp.dot(hm, wdb_ref[...], preferred_element_type=F32)
        _rows_store(y_ref, y, ROW_SUBLANES)


def _moe(tile_expert, tile_first, tile_next, n_valid, xs, w_g, w_u, w_d):
    n_rows = xs.shape[0] // ROW_SUBLANES
    row = lambda i, *_: (i, 0)
    return pl.pallas_call(
        _moe_kernel,
        grid_spec=pltpu.PrefetchScalarGridSpec(
            num_scalar_prefetch=4,
            grid=(n_rows // EXPERT_TILE,),
            in_specs=[
                pl.BlockSpec((EXPERT_TILE * ROW_SUBLANES, LANES),
                             lambda i, te, first, nxt, nv: (jnp.minimum(i, nv[0] - 1), 0)),
                pl.BlockSpec(memory_space=pl.ANY),
                pl.BlockSpec(memory_space=pl.ANY),
                pl.BlockSpec(memory_space=pl.ANY),
            ],
            out_specs=pl.BlockSpec((EXPERT_TILE * ROW_SUBLANES, LANES), row),
            scratch_shapes=[pltpu.VMEM((D_MODEL, D_EXPERT), F32), pltpu.VMEM((D_MODEL, D_EXPERT), F32),
                            pltpu.VMEM((D_EXPERT, D_MODEL), F32),
                            pltpu.VMEM((D_MODEL, D_EXPERT), BF16), pltpu.VMEM((D_MODEL, D_EXPERT), BF16),
                            pltpu.VMEM((D_EXPERT, D_MODEL), BF16),
                            pltpu.SemaphoreType.DMA((3,))],
        ),
        out_shape=jax.ShapeDtypeStruct((n_rows * ROW_SUBLANES, LANES), F32),
        compiler_params=_params(("arbitrary",), 48),
    )(tile_expert, tile_first, tile_next, n_valid, xs, w_g, w_u, w_d)


def _combine_kernel(pos1_ref, pos2_ref, h1_ref, route_ref, gf_ref, y_ref, out_ref, ya_ref, yb_ref, sems):
    i = pl.program_id(0)
    n_steps = pl.num_programs(0)

    def row_copy(tile, r, pos_ref, buf_ref):
        slot = tile % 2
        src = pl.multiple_of(pos_ref[tile * OUT_TILE + r] * ROW_SUBLANES, ROW_SUBLANES)
        dst = pl.multiple_of(r * ROW_SUBLANES, ROW_SUBLANES)
        return pltpu.make_async_copy(y_ref.at[pl.ds(src, ROW_SUBLANES), :],
                                     buf_ref.at[slot, pl.ds(dst, ROW_SUBLANES), :], sems.at[slot])

    def for_each_row(tile, action):
        def group(j, carry):
            for k in range(DMA_UNROLL):
                action(row_copy(tile, j * DMA_UNROLL + k, pos1_ref, ya_ref))
                action(row_copy(tile, j * DMA_UNROLL + k, pos2_ref, yb_ref))
            return carry

        lax.fori_loop(0, OUT_TILE // DMA_UNROLL, group, 0)

    @pl.when(i == 0)
    def _():
        for_each_row(i, lambda cp: cp.start())

    @pl.when(i + 1 < n_steps)
    def _():
        for_each_row(i + 1, lambda cp: cp.start())

    for_each_row(i, lambda cp: cp.wait())
    slot = i % 2
    rec = route_ref[...]
    ya = _rows_load(ya_ref.at[slot], OUT_TILE, ROW_SUBLANES)
    yb = _rows_load(yb_ref.at[slot], OUT_TILE, ROW_SUBLANES)
    hh = h1_ref[...] + rec[:, ROUTE_W1:ROUTE_W1 + 1] * ya + rec[:, ROUTE_W2:ROUTE_W2 + 1] * yb
    ms = jnp.mean(hh * hh, axis=-1, keepdims=True)
    out_ref[...] = hh * lax.rsqrt(ms + EPS) * gf_ref[...]


def _combine(pos1, pos2, h1, route, g_f, y):
    t = h1.shape[0]
    row = lambda i, *_: (i, 0)
    return pl.pallas_call(
        _combine_kernel,
        grid_spec=pltpu.PrefetchScalarGridSpec(
            num_scalar_prefetch=2,
            grid=(t // OUT_TILE,),
            in_specs=[
                pl.BlockSpec((OUT_TILE, D_MODEL), row),
                pl.BlockSpec((OUT_TILE, LANES), row),
                pl.BlockSpec((1, D_MODEL), lambda i, *_: (0, 0)),
                pl.BlockSpec(memory_space=pl.ANY),
            ],
            out_specs=pl.BlockSpec((OUT_TILE, D_MODEL), row),
            scratch_shapes=[pltpu.VMEM((2, OUT_TILE * ROW_SUBLANES, LANES), F32),
                            pltpu.VMEM((2, OUT_TILE * ROW_SUBLANES, LANES), F32),
                            pltpu.SemaphoreType.DMA((2,))],
        ),
        out_shape=jax.ShapeDtypeStruct((t, D_MODEL), F32),
        compiler_params=_params(("arbitrary",), 40),
    )(pos1, pos2, h1, route, g_f, y)


def _routing_tables(route, cnt, n_tiles):
    counts = cnt[0, N_GROUPS:N_GROUPS + N_EXPERTS].astype(jnp.int32)
    tiles = (counts + EXPERT_TILE - 1) // EXPERT_TILE
    tile_end = jnp.cumsum(tiles)
    row_start = (tile_end - tiles) * EXPERT_TILE
    n_valid = tile_end[-1]
    expert_ids = jnp.arange(N_EXPERTS, dtype=jnp.int32)

    def positions(expert_lane, rank_lane):
        expert = route[:, expert_lane].astype(jnp.int32)
        start = jnp.sum(jnp.where(expert[:, None] == expert_ids[None, :], row_start[None, :], 0), axis=1)
        return start + route[:, rank_lane].astype(jnp.int32)

    pos1 = positions(ROUTE_E1, ROUTE_R1)
    pos2 = positions(ROUTE_E2, ROUTE_R2)
    tile = jnp.minimum(jnp.arange(n_tiles, dtype=jnp.int32), n_valid - 1)
    tile_expert = jnp.sum(tile[:, None] >= tile_end[None, :], axis=-1).astype(jnp.int32)
    owner = tile_expert[:, None] == expert_ids[None, :]
    tile_first = (tile == jnp.sum(jnp.where(owner, (tile_end - tiles)[None, :], 0), axis=1)).astype(jnp.int32)
    later = (expert_ids[None, :] > expert_ids[:, None]) & (tiles[None, :] > 0)
    next_expert = jnp.min(jnp.where(later, expert_ids[None, :], N_EXPERTS), axis=1)
    next_expert = jnp.where(next_expert == N_EXPERTS, -1, next_expert)
    tile_next = jnp.sum(jnp.where(owner, next_expert[None, :], 0), axis=1).astype(jnp.int32)
    pad_start = row_start + counts
    pad_len = tiles * EXPERT_TILE - counts
    return pos1, pos2, tile_expert, tile_first, tile_next, n_valid.reshape(1), pad_start, pad_len


def kernel(x, meta_tokens, norm1_g, w_in, b_fox_f, gla_w_gate2, gla_b_gate, gla_norm_g, fox_norm_g, w_out,
           norm2_g, w_router_group, b_router_group, w_router_expert, b_router_expert, w_exp_gate, w_exp_up,
           w_exp_down, norm_f_g):
    batch, seq, _ = x.shape
    assert batch == 1 and norm1_g.shape[0] == 1
    assert seq % FOX_TILE == 0 and seq % ROW_TILE == 0 and (HEAD_ROWS + seq) % BIAS_BLOCK == 0
    t = HEAD_ROWS + seq
    x2 = x[0]
    head = jnp.concatenate([jnp.zeros((PROJ_SKIP + PAD_FRONT, D_MODEL), F32), meta_tokens.astype(F32)], axis=0)

    assert w_in.shape == (1, D_MODEL, D_IN_PROJ) and PROJ_ALIGNED % PROJ_STAGE_COLS == 0
    proj, small = _in_proj(head, x2, norm1_g, w_in[0].T)

    negc = _fox_bias(small, b_fox_f[0].reshape(FOX_HEADS, 1))
    w2_pad = jnp.zeros((LANES, GLA_DK_TOT), F32).at[FOX_HEADS:FOX_HEADS + GLA_RANK].set(gla_w_gate2[0])
    o_gla = _gla(proj, small, w2_pad, gla_b_gate, gla_norm_g)
    o_fox = _fox(proj, negc.reshape(FOX_HEADS, 1, t), fox_norm_g)

    w_router = jnp.concatenate(
        [w_router_group[0], jnp.transpose(w_router_expert[0], (1, 0, 2)).reshape(D_MODEL, N_EXPERTS),
         jnp.zeros((D_MODEL, LANES - N_GROUPS - N_EXPERTS), F32)], axis=1).astype(BF16)
    b_router = jnp.concatenate([b_router_group[0], b_router_expert[0].reshape(-1),
                                jnp.zeros((LANES - N_GROUPS - N_EXPERTS,), F32)]).reshape(1, LANES)
    h1, u2, route, cnt = _out_proj(o_gla, o_fox, x2, w_out[0].astype(BF16), norm2_g, w_router, b_router)

    n_tiles = (2 * seq) // EXPERT_TILE + N_EXPERTS
    pos1, pos2, tile_expert, tile_first, tile_next, n_valid, pad_start, pad_len = _routing_tables(route, cnt, n_tiles)
    xs = _scatter(pos1, pos2, pad_start, pad_len, n_valid, u2, n_tiles * EXPERT_TILE)
    y = _moe(tile_expert, tile_first, tile_next, n_valid, xs,
             w_exp_gate[0].reshape(N_EXPERTS, D_MODEL, D_EXPERT),
             w_exp_up[0].reshape(N_EXPERTS, D_MODEL, D_EXPERT),
             w_exp_down[0].reshape(N_EXPERTS, D_EXPERT, D_MODEL))
    out = _combine(pos1, pos2, h1, route, norm_f_g.reshape(1, D_MODEL), y)
    return out.reshape(1, seq, D_MODEL)
```

```python
import jax
import jax.numpy as jnp
from jax import lax
from jax.experimental import pallas as pl
from jax.experimental.pallas import tpu as pltpu

D_MODEL = 2048
N_META = 16
GLA_HEADS = 4
GLA_DK = 128
GLA_DV = 256
GLA_DK_TOT = GLA_HEADS * GLA_DK
GLA_DV_TOT = GLA_HEADS * GLA_DV
GLA_RANK = 16
GLA_TAU = 16.0
GLA_CHUNK = 64
FOX_HEADS = 8
FOX_HD = 128
FOX_W = FOX_HEADS * FOX_HD
FOX_BLOCK = 128
PAD_FRONT = FOX_BLOCK - N_META
HEAD_ROWS = PAD_FRONT + N_META
N_GROUPS = 4
EXPERTS_PER_GROUP = 8
N_EXPERTS = N_GROUPS * EXPERTS_PER_GROUP
D_EXPERT = 512
EPS = 1e-6

LANES = 128
PROJ_ROWS = 2 * HEAD_ROWS
PROJ_SKIP = PROJ_ROWS - HEAD_ROWS
BIAS_BLOCK = 640
GLA_ROWS = 4 * GLA_CHUNK
FOX_TILE = 1024
FOX_ROWS = 128
FOX_KEYS = 1024
FOX_GROUP = 2
FOX_SKEW = 3
LOG2E = 1.4426950408889634
ROW_TILE = 512
EXPERT_TILE = 256
OUT_TILE = 256
MASK_VALUE = -1e30
PROJ_BIG = 3 * FOX_W + 2 * GLA_DK_TOT + 2 * GLA_DV_TOT
PROJ_FF = 3 * FOX_W
PROJ_GQ = PROJ_FF + FOX_HEADS
PROJ_GZ = PROJ_GQ + 2 * GLA_DK_TOT + 2 * GLA_DV_TOT
D_IN_PROJ = PROJ_GZ + GLA_RANK
PROJ_ALIGNED = (D_IN_PROJ // LANES) * LANES
PROJ_STAGE_COLS = 256
ROW_SUBLANES = D_MODEL // LANES
DMA_UNROLL = 8

F32 = jnp.float32
BF16 = jnp.bfloat16
NT_DIMS = (((1,), (1,)), ((), ()))
TN_DIMS = (((0,), (0,)), ((), ()))


def _log_sigmoid(x):
    return jnp.minimum(x, 0.0) - jnp.log(1.0 + jnp.exp(-jnp.abs(x)))


def _sigmoid(x):
    return 1.0 / (1.0 + jnp.exp(-x))


def _split3(x):
    hi = x.astype(BF16)
    rest = x - hi.astype(F32)
    mid = rest.astype(BF16)
    lo = (rest - mid.astype(F32)).astype(BF16)
    return hi, mid, lo


def _rows_load(ref, n_rows, n_chunks):
    return jnp.concatenate([ref[pl.ds(s, n_rows, stride=n_chunks), :] for s in range(n_chunks)], axis=1)


def _rows_store(ref, value, n_chunks):
    n_rows = value.shape[0]
    for s in range(n_chunks):
        ref[pl.ds(s, n_rows, stride=n_chunks), :] = value[:, s * LANES:(s + 1) * LANES]


def _params(semantics, vmem_mb):
    return pltpu.CompilerParams(dimension_semantics=semantics, vmem_limit_bytes=vmem_mb * 1024 * 1024)


def _in_proj_kernel(head_ref, x_ref, g_ref, wt_hbm, proj_ref, small_ref, w_ref, stage_ref, tail_ref, sems):
    n_chunks = PROJ_ALIGNED // PROJ_STAGE_COLS
    n_tail = D_IN_PROJ - PROJ_ALIGNED

    def chunk_copy(c):
        return pltpu.make_async_copy(wt_hbm.at[pl.ds(c * PROJ_STAGE_COLS, PROJ_STAGE_COLS), :],
                                     stage_ref.at[c % 2], sems.at[c % 2])

    def tail_copy():
        return pltpu.make_async_copy(wt_hbm.at[pl.ds(PROJ_ALIGNED, n_tail), :],
                                     tail_ref.at[pl.ds(0, n_tail), :], sems.at[2])

    @pl.when(pl.program_id(0) == 0)
    def _():
        tail_ref[...] = jnp.zeros_like(tail_ref)
        tail_copy().start()
        chunk_copy(0).start()
        for c in range(n_chunks):
            if c + 1 < n_chunks:
                chunk_copy(c + 1).start()
            chunk_copy(c).wait()
            w_ref[:, c * PROJ_STAGE_COLS:(c + 1) * PROJ_STAGE_COLS] = stage_ref[c % 2].T.astype(BF16)
        tail_copy().wait()
        w_ref[:, PROJ_ALIGNED:] = tail_ref[...].T.astype(BF16)

    x = jnp.where(pl.program_id(0) == 0, head_ref[...], x_ref[...])
    ms = jnp.mean(x * x, axis=-1, keepdims=True)
    xn = (x * lax.rsqrt(ms + EPS) * g_ref[...]).astype(BF16)
    p = jnp.dot(xn, w_ref[...], preferred_element_type=F32)
    lane = lax.broadcasted_iota(jnp.int32, (PROJ_ROWS, LANES), 1)
    small_ref[...] = jnp.where(lane < FOX_HEADS, p[:, PROJ_FF:PROJ_FF + LANES], p[:, PROJ_ALIGNED:])
    proj_ref[...] = jnp.concatenate([p[:, :PROJ_FF], p[:, PROJ_GQ:PROJ_GZ]], axis=1).astype(BF16)


def _in_proj(head, x, g1, w_t):
    t = PROJ_ROWS + x.shape[0]
    fixed = lambda i: (0, 0)
    return pl.pallas_call(
        _in_proj_kernel,
        grid=(t // PROJ_ROWS,),
        in_specs=[
            pl.BlockSpec((PROJ_ROWS, D_MODEL), fixed),
            pl.BlockSpec((PROJ_ROWS, D_MODEL), lambda i: (jnp.maximum(i - 1, 0), 0)),
            pl.BlockSpec((1, D_MODEL), fixed),
            pl.BlockSpec(memory_space=pl.ANY),
        ],
        out_specs=[
            pl.BlockSpec((PROJ_ROWS, PROJ_BIG), lambda i: (i, 0)),
            pl.BlockSpec((PROJ_ROWS, LANES), lambda i: (i, 0)),
        ],
        out_shape=[
            jax.ShapeDtypeStruct((t, PROJ_BIG), BF16),
            jax.ShapeDtypeStruct((t, LANES), F32),
        ],
        scratch_shapes=[pltpu.VMEM((D_MODEL, PROJ_ALIGNED + LANES), BF16),
                        pltpu.VMEM((2, PROJ_STAGE_COLS, D_MODEL), F32),
                        pltpu.VMEM((LANES, D_MODEL), F32),
                        pltpu.SemaphoreType.DMA((3,))],
        compiler_params=_params(("arbitrary",), 56),
    )(head, x, g1, w_t)


def _fox_bias_kernel(small_ref, bf_ref, negc_ref):
    t = negc_ref.shape[1]
    r = lax.broadcasted_iota(jnp.int32, (BIAS_BLOCK, BIAS_BLOCK), 0)
    c = lax.broadcasted_iota(jnp.int32, (BIAS_BLOCK, BIAS_BLOCK), 1)
    upper = jnp.where(r <= c, 1.0, 0.0).astype(BF16)
    lane = lax.broadcasted_iota(jnp.int32, (FOX_HEADS, BIAS_BLOCK), 1)

    def body(b, carry):
        off = pl.multiple_of(b * BIAS_BLOCK, BIAS_BLOCK)
        valid = (off + lane) >= PAD_FRONT
        f_logit = small_ref[pl.ds(PROJ_SKIP + off, BIAS_BLOCK), :].T[0:FOX_HEADS, :]
        lf = jnp.where(valid, _log_sigmoid(f_logit + bf_ref[...]), 0.0)
        cum = sum(jnp.dot(piece, upper, preferred_element_type=F32) for piece in _split3(lf)) + carry
        negc_ref[:, pl.ds(off, BIAS_BLOCK)] = jnp.where(valid, -LOG2E * cum, MASK_VALUE)
        return cum[:, BIAS_BLOCK - 1:BIAS_BLOCK]

    lax.fori_loop(0, t // BIAS_BLOCK, body, jnp.zeros((FOX_HEADS, 1), F32))


def _fox_bias(small, b_f):
    return pl.pallas_call(
        _fox_bias_kernel,
        out_shape=jax.ShapeDtypeStruct((FOX_HEADS, small.shape[0] - PROJ_SKIP), F32),
    )(small, b_f)


def _gla_kernel(q_ref, k_ref, v_ref, r_ref, small_ref, w2_ref, bg_ref, ng_ref, o_ref, st_ref):
    i = pl.program_id(0)

    @pl.when(i == 0)
    def _():
        st_ref[...] = jnp.zeros_like(st_ref)

    z_hi, z_mid, _ = _split3(small_ref[...])
    w_hi, w_mid, _ = _split3(w2_ref[...])
    gate_logit = (jnp.dot(z_hi, w_hi, preferred_element_type=F32) + jnp.dot(z_hi, w_mid, preferred_element_type=F32)
                  + jnp.dot(z_mid, w_hi, preferred_element_type=F32) + bg_ref[...])
    g = _log_sigmoid(gate_logit) * (1.0 / GLA_TAU)
    rowid = i * GLA_ROWS + lax.broadcasted_iota(jnp.int32, (GLA_ROWS, 1), 0)
    g = jnp.where(rowid >= PROJ_SKIP + PAD_FRONT, g, 0.0)

    ci = lax.broadcasted_iota(jnp.int32, (GLA_CHUNK, GLA_CHUNK), 0)
    cj = lax.broadcasted_iota(jnp.int32, (GLA_CHUNK, GLA_CHUNK), 1)
    causal = cj <= ci
    lower = jnp.where(causal, 1.0, 0.0).astype(BF16)
    scale = GLA_DK ** -0.5
    mid = GLA_CHUNK // 2

    chunks = [slice(c * GLA_CHUNK, (c + 1) * GLA_CHUNK) for c in range(GLA_ROWS // GLA_CHUNK)]
    cums = [sum(jnp.dot(lower, piece, preferred_element_type=F32) for piece in _split3(g[rows])) for rows in chunks]

    heads = range(GLA_HEADS)
    ks = [slice(h * GLA_DK, (h + 1) * GLA_DK) for h in heads]
    vs = [slice(h * GLA_DV, (h + 1) * GLA_DV) for h in heads]

    prep = []
    for rows, b in zip(chunks, cums):
        b_mid = b[mid:mid + 1]
        b_last = b[GLA_CHUNK - 1:GLA_CHUNK]
        q = q_ref[rows, :].astype(F32) * scale
        k = k_ref[rows, :].astype(F32)
        q_intra = (q * jnp.exp(b - b_mid)).astype(BF16)
        k_intra = (k * jnp.exp(b_mid - b)).astype(BF16)
        q_inter = (q * jnp.exp(b)).astype(BF16)
        k_state = (k * jnp.exp(b_last - b)).astype(BF16)
        decay = jnp.exp(b_last)
        v = [v_ref[rows, vs[h]] for h in heads]
        a = [lax.dot_general(q_intra[:, ks[h]], k_intra[:, ks[h]], NT_DIMS, preferred_element_type=F32)
             for h in heads]
        u_t = [lax.dot_general(v[h], k_state[:, ks[h]], TN_DIMS, preferred_element_type=F32) for h in heads]
        prep.append((q_inter, decay, v, a, u_t))

    st = [st_ref[h] for h in heads]
    for rows, (q_inter, decay, v, a, u_t) in zip(chunks, prep):
        o_inter = [lax.dot_general(q_inter[:, ks[h]], st[h].astype(BF16), NT_DIMS, preferred_element_type=F32)
                   for h in heads]
        st = [decay[:, ks[h]] * st[h] + u_t[h] for h in heads]
        for h in heads:
            o = o_inter[h] + jnp.dot(jnp.where(causal, a[h], 0.0).astype(BF16), v[h], preferred_element_type=F32)
            ms = jnp.mean(o * o, axis=-1, keepdims=True)
            y = o * lax.rsqrt(ms + EPS) * ng_ref[...]
            r = r_ref[rows, vs[h]].astype(F32)
            o_ref[rows, vs[h]] = (y * (r * _sigmoid(r))).astype(BF16)
    for h in heads:
        st_ref[h] = st[h]


def _gla(proj, small, w2_pad, b_gate, norm_g):
    rows = proj.shape[0]
    assert GLA_ROWS == PROJ_SKIP + HEAD_ROWS
    q_blk = (3 * FOX_W) // GLA_DK_TOT
    v_blk = (3 * FOX_W + 2 * GLA_DK_TOT) // GLA_DV_TOT
    return pl.pallas_call(
        _gla_kernel,
        grid=(rows // GLA_ROWS,),
        in_specs=[
            pl.BlockSpec((GLA_ROWS, GLA_DK_TOT), lambda i: (i, q_blk)),
            pl.BlockSpec((GLA_ROWS, GLA_DK_TOT), lambda i: (i, q_blk + 1)),
            pl.BlockSpec((GLA_ROWS, GLA_DV_TOT), lambda i: (i, v_blk)),
            pl.BlockSpec((GLA_ROWS, GLA_DV_TOT), lambda i: (i, v_blk + 1)),
            pl.BlockSpec((GLA_ROWS, LANES), lambda i: (i, 0)),
            pl.BlockSpec((LANES, GLA_DK_TOT), lambda i: (0, 0)),
            pl.BlockSpec((1, GLA_DK_TOT), lambda i: (0, 0)),
            pl.BlockSpec((1, GLA_DV), lambda i: (0, 0)),
        ],
        out_specs=pl.BlockSpec((GLA_ROWS, GLA_DV_TOT), lambda i: (jnp.maximum(i - 1, 0), 0)),
        out_shape=jax.ShapeDtypeStruct((rows - GLA_ROWS, GLA_DV_TOT), BF16),
        scratch_shapes=[pltpu.VMEM((GLA_HEADS, GLA_DV, GLA_DK), F32)],
        compiler_params=_params(("arbitrary",), 32),
    )(proj, proj, proj, proj, small, w2_pad, b_gate, norm_g)


def _fox_kernel(q_ref, k_ref, v_ref, negc_ref, ng_ref, o_ref, qs_ref, va_ref):
    qi = pl.program_id(1)
    n_blocks = FOX_TILE // FOX_ROWS

    @pl.when(qi == 0)
    def _():
        lane = lax.broadcasted_iota(jnp.int32, (v_ref.shape[0], FOX_HD), 1)
        ones_col = jnp.where(lane == 0, 1.0, 0.0).astype(BF16)
        for hh in range(FOX_GROUP):
            va_ref[:, 2 * hh * FOX_HD:(2 * hh + 1) * FOX_HD] = v_ref[:, hh * FOX_HD:(hh + 1) * FOX_HD]
            va_ref[:, (2 * hh + 1) * FOX_HD:(2 * hh + 2) * FOX_HD] = ones_col

    units = [(hh, rb) for hh in range(FOX_GROUP) for rb in range(n_blocks)]
    q0 = pl.multiple_of(HEAD_ROWS + qi * FOX_TILE, FOX_ROWS)
    qs_ref[...] = (q_ref[pl.ds(PROJ_SKIP + q0, FOX_TILE), :].astype(F32) * (FOX_HD ** -0.5 * LOG2E)).astype(BF16)
    row = lax.broadcasted_iota(jnp.int32, (FOX_ROWS, FOX_ROWS), 0)
    col = lax.broadcasted_iota(jnp.int32, (FOX_ROWS, FOX_ROWS), 1)

    def run(state, steps):
        def scores(step):
            u, off, k0, k1, causal_tail = step
            hh, rb = units[u]
            rows = slice(rb * FOX_ROWS, (rb + 1) * FOX_ROWS)
            cols = slice(hh * FOX_HD, (hh + 1) * FOX_HD)
            s = lax.dot_general(qs_ref[rows, cols], k_ref[pl.ds(PROJ_SKIP + off + k0, k1 - k0), cols], NT_DIMS,
                                preferred_element_type=F32)
            s = s + negc_ref[hh, :, pl.ds(off + k0, k1 - k0)]
            if causal_tail:
                tail = jnp.where(col <= row, s[:, k1 - k0 - FOX_ROWS:], MASK_VALUE)
                s = tail if k1 - k0 == FOX_ROWS else jnp.concatenate([s[:, :k1 - k0 - FOX_ROWS], tail], axis=1)
            return s

        def update(step, s, state):
            u, off, k0, k1, _ = step
            hh, _ = units[u]
            m_prev, acc_prev = state[u]
            m_new = jnp.maximum(m_prev, jnp.max(s, axis=-1, keepdims=True))
            p = jnp.exp2(s - m_new).astype(BF16)
            acc_new = jnp.exp2(m_prev - m_new) * acc_prev + jnp.dot(
                p, va_ref[pl.ds(PROJ_SKIP + off + k0, k1 - k0), 2 * hh * FOX_HD:(2 * hh + 2) * FOX_HD],
                preferred_element_type=F32)
            state[u] = (m_new, acc_new)

        state = list(state)
        pending = [scores(st) for st in steps[:FOX_SKEW]]
        for j, st in enumerate(steps):
            if j + FOX_SKEW < len(steps):
                pending.append(scores(steps[j + FOX_SKEW]))
            update(st, pending[j], state)
            pending[j] = None
        return tuple(state)

    head_steps = [(u, 0, 0, HEAD_ROWS, False) for u in range(len(units))]

    def full_steps(off):
        return [(u, off, k0, k0 + FOX_KEYS, False)
                for k0 in range(0, FOX_TILE, FOX_KEYS) for u in range(len(units))]

    diag_steps = []
    for k0 in range(0, FOX_TILE, FOX_KEYS):
        for u, (_, rb) in enumerate(units):
            last = (rb + 1) * FOX_ROWS
            if last > k0:
                diag_steps.append((u, q0, k0, min(k0 + FOX_KEYS, last), last <= k0 + FOX_KEYS))

    state = tuple((jnp.full((FOX_ROWS, 1), MASK_VALUE, F32), jnp.zeros((FOX_ROWS, 2 * FOX_HD), F32))
                  for _ in units)
    state = run(state, head_steps)
    state = lax.fori_loop(
        0, qi, lambda kt, st: run(st, full_steps(pl.multiple_of(HEAD_ROWS + kt * FOX_TILE, FOX_ROWS))), state)
    state = run(state, diag_steps)
    for u, (hh, rb) in enumerate(units):
        _, acc = state[u]
        o = acc[:, :FOX_HD] / acc[:, FOX_HD:FOX_HD + 1]
        ms = jnp.mean(o * o, axis=-1, keepdims=True)
        o_ref[rb * FOX_ROWS:(rb + 1) * FOX_ROWS, hh * FOX_HD:(hh + 1) * FOX_HD] = (
            o * lax.rsqrt(ms + EPS) * ng_ref[...]).astype(BF16)


def _fox(proj, negc3, norm_g):
    rows = proj.shape[0]
    t = rows - PROJ_SKIP
    width = FOX_GROUP * FOX_HD
    k_blk = FOX_W // width
    return pl.pallas_call(
        _fox_kernel,
        grid=(FOX_HEADS // FOX_GROUP, (t - HEAD_ROWS) // FOX_TILE),
        in_specs=[
            pl.BlockSpec((rows, width), lambda g, i: (0, g)),
            pl.BlockSpec((rows, width), lambda g, i: (0, k_blk + g)),
            pl.BlockSpec((rows, width), lambda g, i: (0, 2 * k_blk + g)),
            pl.BlockSpec((FOX_GROUP, 1, t), lambda g, i: (g, 0, 0)),
            pl.BlockSpec((1, FOX_HD), lambda g, i: (0, 0)),
        ],
        out_specs=pl.BlockSpec((FOX_TILE, width), lambda g, i: (i, g)),
        out_shape=jax.ShapeDtypeStruct((t - HEAD_ROWS, FOX_W), BF16),
        scratch_shapes=[pltpu.VMEM((FOX_TILE, width), BF16), pltpu.VMEM((rows, 2 * width), BF16)],
        compiler_params=_params(("arbitrary", "arbitrary"), 56),
    )(proj, proj, proj, negc3, norm_g)


ROUTE_E1, ROUTE_E2, ROUTE_R1, ROUTE_R2, ROUTE_W1, ROUTE_W2 = range(6)


def _out_proj_kernel(og_ref, of_ref, h_ref, wo_ref, g2_ref, wr_ref, br_ref,
                     h1_ref, u2_ref, route_ref, cnt_ref, tri_ref, run_ref):
    i = pl.program_id(0)
    half = ROW_TILE // 2

    @pl.when(i == 0)
    def _():
        r = lax.broadcasted_iota(jnp.int32, (half, half), 0)
        c = lax.broadcasted_iota(jnp.int32, (half, half), 1)
        tri_ref[...] = jnp.where(c < r, 1.0, 0.0).astype(BF16)
        run_ref[...] = jnp.zeros_like(run_ref)

    lane = lax.broadcasted_iota(jnp.int32, (half, LANES), 1).astype(F32)
    ninf = -jnp.inf

    def first_max(vals):
        top = jnp.max(vals, axis=-1, keepdims=True)
        idx = jnp.min(jnp.where(vals == top, lane, float(LANES)), axis=-1, keepdims=True)
        return top, idx

    halves = [slice(0, half), slice(half, ROW_TILE)]
    h1 = [h_ref[rows, :]
          + jnp.dot(og_ref[rows, :], wo_ref[0:GLA_DV_TOT, :], preferred_element_type=F32)
          + jnp.dot(of_ref[rows, :], wo_ref[GLA_DV_TOT:, :], preferred_element_type=F32) for rows in halves]
    running = run_ref[...]
    for rows, h1_half in zip(halves, h1):
        h1_ref[rows, :] = h1_half
        ms = jnp.mean(h1_half * h1_half, axis=-1, keepdims=True)
        u2 = (h1_half * lax.rsqrt(ms + EPS) * g2_ref[...]).astype(BF16)
        u2_ref[rows, :] = u2

        logits = jnp.dot(u2, wr_ref[...], preferred_element_type=F32) + br_ref[...]
        gl = jnp.where(lane < N_GROUPS, logits, ninf)
        g_top, g_idx = first_max(gl)
        p_g = 1.0 / jnp.sum(jnp.exp(gl - g_top), axis=-1, keepdims=True)
        e_lo = N_GROUPS + EXPERTS_PER_GROUP * g_idx
        el = jnp.where((lane >= e_lo) & (lane < e_lo + EXPERTS_PER_GROUP), logits, ninf)
        top1, i1 = first_max(el)
        top2, i2 = first_max(jnp.where(lane == i1, ninf, el))
        ratio = jnp.exp(top2 - top1)
        w1 = 1.0 / (1.0 + ratio)
        w2 = ratio * w1

        is1 = lane == i1
        is2 = lane == i2
        onehot = jnp.where(is1 | is2, 1.0, 0.0)
        before = jnp.dot(tri_ref[...], onehot.astype(BF16), preferred_element_type=F32) + running
        r1 = jnp.sum(jnp.where(is1, before, 0.0), axis=-1, keepdims=True)
        r2 = jnp.sum(jnp.where(is2, before, 0.0), axis=-1, keepdims=True)
        running = running + jnp.sum(onehot, axis=0, keepdims=True)

        rec = jnp.zeros_like(logits)
        for slot, val in ((ROUTE_E1, i1 - N_GROUPS), (ROUTE_E2, i2 - N_GROUPS), (ROUTE_R1, r1), (ROUTE_R2, r2),
                          (ROUTE_W1, p_g * w1), (ROUTE_W2, p_g * w2)):
            rec = jnp.where(lane == slot, val, rec)
        route_ref[rows, :] = rec
    run_ref[...] = running
    cnt_ref[...] = running


def _out_proj(o_gla, o_fox, h0, w_out, g2, w_router, b_router):
    t = h0.shape[0]
    row = lambda i: (i, 0)
    fixed = lambda i: (0, 0)
    return pl.pallas_call(
        _out_proj_kernel,
        grid=(t // ROW_TILE,),
        in_specs=[
            pl.BlockSpec((ROW_TILE, GLA_DV_TOT), row),
            pl.BlockSpec((ROW_TILE, FOX_W), row),
            pl.BlockSpec((ROW_TILE, D_MODEL), row),
            pl.BlockSpec((D_MODEL, D_MODEL), fixed, pipeline_mode=pl.Buffered(1)),
            pl.BlockSpec((1, D_MODEL), fixed),
            pl.BlockSpec((D_MODEL, LANES), fixed),
            pl.BlockSpec((1, LANES), fixed),
        ],
        out_specs=[
            pl.BlockSpec((ROW_TILE, D_MODEL), row),
            pl.BlockSpec((ROW_TILE, D_MODEL), row),
            pl.BlockSpec((ROW_TILE, LANES), row),
            pl.BlockSpec((1, LANES), fixed),
        ],
        out_shape=[
            jax.ShapeDtypeStruct((t, D_MODEL), F32),
            jax.ShapeDtypeStruct((t, D_MODEL), BF16),
            jax.ShapeDtypeStruct((t, LANES), F32),
            jax.ShapeDtypeStruct((1, LANES), F32),
        ],
        scratch_shapes=[pltpu.VMEM((ROW_TILE // 2, ROW_TILE // 2), BF16), pltpu.VMEM((1, LANES), F32)],
        compiler_params=_params(("arbitrary",), 48),
    )(o_gla, o_fox, h0, w_out, g2, w_router, b_router)


def _scatter_kernel(pos1_ref, pos2_ref, pad_start_ref, pad_len_ref, nv_ref, u2_ref, xs_ref,
                    rows_ref, zero_ref, sems, zsem):
    i = pl.program_id(0)
    n_steps = pl.num_programs(0)
    tile_rows = EXPERT_TILE * ROW_SUBLANES
    n_tiles = xs_ref.shape[0] // tile_rows

    def zero_copy(start, n_rows):
        start = pl.multiple_of(start * ROW_SUBLANES, ROW_SUBLANES)
        return pltpu.make_async_copy(zero_ref.at[pl.ds(0, n_rows * ROW_SUBLANES), :],
                                     xs_ref.at[pl.ds(start, n_rows * ROW_SUBLANES), :], zsem)

    def for_each_zero_block(action):
        def expert(e, carry):
            start = pad_start_ref[e]
            length = pad_len_ref[e]
            for bit in reversed(range(EXPERT_TILE.bit_length() - 1)):
                n_rows = 1 << bit

                @pl.when((length & n_rows) != 0)
                def _():
                    action(zero_copy(start, n_rows))

                start = start + (length & n_rows)
            return carry

        def unused_tile(j, carry):
            action(zero_copy(j * EXPERT_TILE, EXPERT_TILE))
            return carry

        lax.fori_loop(0, N_EXPERTS, expert, 0)
        lax.fori_loop(nv_ref[0], n_tiles, unused_tile, 0)

    @pl.when(i == 0)
    def _():
        zero_ref[...] = jnp.zeros_like(zero_ref)
        for_each_zero_block(lambda cp: cp.start())

    def row_copy(step, r, pos_ref):
        slot = step % 2
        src = pl.multiple_of(r * ROW_SUBLANES, ROW_SUBLANES)
        dst = pl.multiple_of(pos_ref[step * ROW_TILE + r] * ROW_SUBLANES, ROW_SUBLANES)
        return pltpu.make_async_copy(rows_ref.at[slot, pl.ds(src, ROW_SUBLANES), :],
                                     xs_ref.at[pl.ds(dst, ROW_SUBLANES), :], sems.at[slot])

    def for_each_row(step, action):
        def group(j, carry):
            for k in range(DMA_UNROLL):
                action(row_copy(step, j * DMA_UNROLL + k, pos1_ref))
                action(row_copy(step, j * DMA_UNROLL + k, pos2_ref))
            return carry

        lax.fori_loop(0, ROW_TILE // DMA_UNROLL, group, 0)

    _rows_store(rows_ref.at[i % 2], u2_ref[...].astype(F32), ROW_SUBLANES)
    for_each_row(i, lambda cp: cp.start())

    @pl.when(i > 0)
    def _():
        for_each_row(i - 1, lambda cp: cp.wait())

    @pl.when(i == n_steps - 1)
    def _():
        for_each_row(i, lambda cp: cp.wait())
        for_each_zero_block(lambda cp: cp.wait())


def _scatter(pos1, pos2, pad_start, pad_len, n_valid, u2, n_rows):
    t = u2.shape[0]
    return pl.pallas_call(
        _scatter_kernel,
        grid_spec=pltpu.PrefetchScalarGridSpec(
            num_scalar_prefetch=5,
            grid=(t // ROW_TILE,),
            in_specs=[pl.BlockSpec((ROW_TILE, D_MODEL), lambda i, *_: (i, 0))],
            out_specs=pl.BlockSpec(memory_space=pl.ANY),
            scratch_shapes=[pltpu.VMEM((2, ROW_TILE * ROW_SUBLANES, LANES), F32),
                            pltpu.VMEM((EXPERT_TILE * ROW_SUBLANES, LANES), F32),
                            pltpu.SemaphoreType.DMA((2,)), pltpu.SemaphoreType.DMA],
        ),
        out_shape=jax.ShapeDtypeStruct((n_rows * ROW_SUBLANES, LANES), F32),
        compiler_params=_params(("arbitrary",), 40),
    )(pos1, pos2, pad_start, pad_len, n_valid, u2)


def _moe_kernel(te_ref, first_ref, next_ref, nv_ref, xs_ref, wg_hbm, wu_hbm, wd_hbm, y_ref,
                wg_stage, wu_stage, wd_stage, wgb_ref, wub_ref, wdb_ref, sems):
    i = pl.program_id(0)

    def weight_copies(expert):
        return (pltpu.make_async_copy(wg_hbm.at[expert], wg_stage, sems.at[0]),
                pltpu.make_async_copy(wu_hbm.at[expert], wu_stage, sems.at[1]),
                pltpu.make_async_copy(wd_hbm.at[expert], wd_stage, sems.at[2]))

    @pl.when(i >= nv_ref[0])
    def _():
        y_ref[...] = jnp.zeros_like(y_ref)

    @pl.when(i < nv_ref[0])
    def _():
        @pl.when(first_ref[i] == 1)
        def _():
            @pl.when(i == 0)
            def _():
                for cp in weight_copies(te_ref[i]):
                    cp.start()

            for cp in weight_copies(te_ref[i]):
                cp.wait()
            wgb_ref[...] = wg_stage[...].astype(BF16)
            wub_ref[...] = wu_stage[...].astype(BF16)
            wdb_ref[...] = wd_stage[...].astype(BF16)

            @pl.when(next_ref[i] >= 0)
            def _():
                for cp in weight_copies(next_ref[i]):
                    cp.start(priority=1)

        x = _rows_load(xs_ref, EXPERT_TILE, ROW_SUBLANES).astype(BF16)
        hg = jnp.dot(x, wgb_ref[...], preferred_element_type=F32)
        hu = jnp.dot(x, wub_ref[...], preferred_element_type=F32)
        hm = (hg * _sigmoid(hg) * hu).astype(BF16)
        y = jnp.dot(hm, wdb_ref[...], preferred_element_type=F32)
        _rows_store(y_ref, y, ROW_SUBLANES)


def _moe(tile_expert, tile_first, tile_next, n_valid, xs, w_g, w_u, w_d):
    n_rows = xs.shape[0] // ROW_SUBLANES
    row = lambda i, *_: (i, 0)
    return pl.pallas_call(
        _moe_kernel,
        grid_spec=pltpu.PrefetchScalarGridSpec(
            num_scalar_prefetch=4,
            grid=(n_rows // EXPERT_TILE,),
            in_specs=[
                pl.BlockSpec((EXPERT_TILE * ROW_SUBLANES, LANES),
                             lambda i, te, first, nxt, nv: (jnp.minimum(i, nv[0] - 1), 0)),
                pl.BlockSpec(memory_space=pl.ANY),
                pl.BlockSpec(memory_space=pl.ANY),
                pl.BlockSpec(memory_space=pl.ANY),
            ],
            out_specs=pl.BlockSpec((EXPERT_TILE * ROW_SUBLANES, LANES), row),
            scratch_shapes=[pltpu.VMEM((D_MODEL, D_EXPERT), F32), pltpu.VMEM((D_MODEL, D_EXPERT), F32),
                            pltpu.VMEM((D_EXPERT, D_MODEL), F32),
                            pltpu.VMEM((D_MODEL, D_EXPERT), BF16), pltpu.VMEM((D_MODEL, D_EXPERT), BF16),
                            pltpu.VMEM((D_EXPERT, D_MODEL), BF16),
                            pltpu.SemaphoreType.DMA((3,))],
        ),
        out_shape=jax.ShapeDtypeStruct((n_rows * ROW_SUBLANES, LANES), F32),
        compiler_params=_params(("arbitrary",), 48),
    )(tile_expert, tile_first, tile_next, n_valid, xs, w_g, w_u, w_d)


def _combine_kernel(pos1_ref, pos2_ref, h1_ref, route_ref, gf_ref, y_ref, out_ref, ya_ref, yb_ref, sems):
    i = pl.program_id(0)
    n_steps = pl.num_programs(0)

    def row_copy(tile, r, pos_ref, buf_ref):
        slot = tile % 2
        src = pl.multiple_of(pos_ref[tile * OUT_TILE + r] * ROW_SUBLANES, ROW_SUBLANES)
        dst = pl.multiple_of(r * ROW_SUBLANES, ROW_SUBLANES)
        return pltpu.make_async_copy(y_ref.at[pl.ds(src, ROW_SUBLANES), :],
                                     buf_ref.at[slot, pl.ds(dst, ROW_SUBLANES), :], sems.at[slot])

    def for_each_row(tile, action):
        def group(j, carry):
            for k in range(DMA_UNROLL):
                action(row_copy(tile, j * DMA_UNROLL + k, pos1_ref, ya_ref))
                action(row_copy(tile, j * DMA_UNROLL + k, pos2_ref, yb_ref))
            return carry

        lax.fori_loop(0, OUT_TILE // DMA_UNROLL, group, 0)

    @pl.when(i == 0)
    def _():
        for_each_row(i, lambda cp: cp.start())

    @pl.when(i + 1 < n_steps)
    def _():
        for_each_row(i + 1, lambda cp: cp.start())

    for_each_row(i, lambda cp: cp.wait())
    slot = i % 2
    rec = route_ref[...]
    ya = _rows_load(ya_ref.at[slot], OUT_TILE, ROW_SUBLANES)
    yb = _rows_load(yb_ref.at[slot], OUT_TILE, ROW_SUBLANES)
    hh = h1_ref[...] + rec[:, ROUTE_W1:ROUTE_W1 + 1] * ya + rec[:, ROUTE_W2:ROUTE_W2 + 1] * yb
    ms = jnp.mean(hh * hh, axis=-1, keepdims=True)
    out_ref[...] = hh * lax.rsqrt(ms + EPS) * gf_ref[...]


def _combine(pos1, pos2, h1, route, g_f, y):
    t = h1.shape[0]
    row = lambda i, *_: (i, 0)
    return pl.pallas_call(
        _combine_kernel,
        grid_spec=pltpu.PrefetchScalarGridSpec(
            num_scalar_prefetch=2,
            grid=(t // OUT_TILE,),
            in_specs=[
                pl.BlockSpec((OUT_TILE, D_MODEL), row),
                pl.BlockSpec((OUT_TILE, LANES), row),
                pl.BlockSpec((1, D_MODEL), lambda i, *_: (0, 0)),
                pl.BlockSpec(memory_space=pl.ANY),
            ],
            out_specs=pl.BlockSpec((OUT_TILE, D_MODEL), row),
            scratch_shapes=[pltpu.VMEM((2, OUT_TILE * ROW_SUBLANES, LANES), F32),
                            pltpu.VMEM((2, OUT_TILE * ROW_SUBLANES, LANES), F32),
                            pltpu.SemaphoreType.DMA((2,))],
        ),
        out_shape=jax.ShapeDtypeStruct((t, D_MODEL), F32),
        compiler_params=_params(("arbitrary",), 40),
    )(pos1, pos2, h1, route, g_f, y)


def _routing_tables(route, cnt, n_tiles):
    counts = cnt[0, N_GROUPS:N_GROUPS + N_EXPERTS].astype(jnp.int32)
    tiles = (counts + EXPERT_TILE - 1) // EXPERT_TILE
    tile_end = jnp.cumsum(tiles)
    row_start = (tile_end - tiles) * EXPERT_TILE
    n_valid = tile_end[-1]
    expert_ids = jnp.arange(N_EXPERTS, dtype=jnp.int32)

    def positions(expert_lane, rank_lane):
        expert = route[:, expert_lane].astype(jnp.int32)
        start = jnp.sum(jnp.where(expert[:, None] == expert_ids[None, :], row_start[None, :], 0), axis=1)
        return start + route[:, rank_lane].astype(jnp.int32)

    pos1 = positions(ROUTE_E1, ROUTE_R1)
    pos2 = positions(ROUTE_E2, ROUTE_R2)
    tile = jnp.minimum(jnp.arange(n_tiles, dtype=jnp.int32), n_valid - 1)
    tile_expert = jnp.sum(tile[:, None] >= tile_end[None, :], axis=-1).astype(jnp.int32)
    owner = tile_expert[:, None] == expert_ids[None, :]
    tile_first = (tile == jnp.sum(jnp.where(owner, (tile_end - tiles)[None, :], 0), axis=1)).astype(jnp.int32)
    later = (expert_ids[None, :] > expert_ids[:, None]) & (tiles[None, :] > 0)
    next_expert = jnp.min(jnp.where(later, expert_ids[None, :], N_EXPERTS), axis=1)
    next_expert = jnp.where(next_expert == N_EXPERTS, -1, next_expert)
    tile_next = jnp.sum(jnp.where(owner, next_expert[None, :], 0), axis=1).astype(jnp.int32)
    pad_start = row_start + counts
    pad_len = tiles * EXPERT_TILE - counts
    return pos1, pos2, tile_expert, tile_first, tile_next, n_valid.reshape(1), pad_start, pad_len


def kernel(x, meta_tokens, norm1_g, w_in, b_fox_f, gla_w_gate2, gla_b_gate, gla_norm_g, fox_norm_g, w_out,
           norm2_g, w_router_group, b_router_group, w_router_expert, b_router_expert, w_exp_gate, w_exp_up,
           w_exp_down, norm_f_g):
    batch, seq, _ = x.shape
    assert batch == 1 and norm1_g.shape[0] == 1
    assert seq % FOX_TILE == 0 and seq % ROW_TILE == 0 and (HEAD_ROWS + seq) % BIAS_BLOCK == 0
    t = HEAD_ROWS + seq
    x2 = x[0]
    head = jnp.concatenate([jnp.zeros((PROJ_SKIP + PAD_FRONT, D_MODEL), F32), meta_tokens.astype(F32)], axis=0)

    assert w_in.shape == (1, D_MODEL, D_IN_PROJ) and PROJ_ALIGNED % PROJ_STAGE_COLS == 0
    proj, small = _in_proj(head, x2, norm1_g, w_in[0].T)

    negc = _fox_bias(small, b_fox_f[0].reshape(FOX_HEADS, 1))
    w2_pad = jnp.zeros((LANES, GLA_DK_TOT), F32).at[FOX_HEADS:FOX_HEADS + GLA_RANK].set(gla_w_gate2[0])
    o_gla = _gla(proj, small, w2_pad, gla_b_gate, gla_norm_g)
    o_fox = _fox(proj, negc.reshape(FOX_HEADS, 1, t), fox_norm_g)

    w_router = jnp.concatenate(
        [w_router_group[0], jnp.transpose(w_router_expert[0], (1, 0, 2)).reshape(D_MODEL, N_EXPERTS),
         jnp.zeros((D_MODEL, LANES - N_GROUPS - N_EXPERTS), F32)], axis=1).astype(BF16)
    b_router = jnp.concatenate([b_router_group[0], b_router_expert[0].reshape(-1),
                                jnp.zeros((LANES - N_GROUPS - N_EXPERTS,), F32)]).reshape(1, LANES)
    h1, u2, route, cnt = _out_proj(o_gla, o_fox, x2, w_out[0].astype(BF16), norm2_g, w_router, b_router)

    n_tiles = (2 * seq) // EXPERT_TILE + N_EXPERTS
    pos1, pos2, tile_expert, tile_first, tile_next, n_valid, pad_start, pad_len = _routing_tables(route, cnt, n_tiles)
    xs = _scatter(pos1, pos2, pad_start, pad_len, n_valid, u2, n_tiles * EXPERT_TILE)
    y = _moe(tile_expert, tile_first, tile_next, n_valid, xs,
             w_exp_gate[0].reshape(N_EXPERTS, D_MODEL, D_EXPERT),
             w_exp_up[0].reshape(N_EXPERTS, D_MODEL, D_EXPERT),
             w_exp_down[0].reshape(N_EXPERTS, D_EXPERT, D_MODEL))
    out = _combine(pos1, pos2, h1, route, norm_f_g.reshape(1, D_MODEL), y)
    return out.reshape(1, seq, D_MODEL)
```

```python
import jax
import jax.numpy as jnp
from jax import lax
from jax.experimental import pallas as pl
from jax.experimental.pallas import tpu as pltpu

D_MODEL = 2048
N_META = 16
GLA_HEADS = 4
GLA_DK = 128
GLA_DV = 256
GLA_DK_TOT = GLA_HEADS * GLA_DK
GLA_DV_TOT = GLA_HEADS * GLA_DV
GLA_RANK = 16
GLA_TAU = 16.0
GLA_CHUNK = 64
FOX_HEADS = 8
FOX_HD = 128
FOX_W = FOX_HEADS * FOX_HD
FOX_BLOCK = 128
PAD_FRONT = FOX_BLOCK - N_META
HEAD_ROWS = PAD_FRONT + N_META
N_GROUPS = 4
EXPERTS_PER_GROUP = 8
N_EXPERTS = N_GROUPS * EXPERTS_PER_GROUP
D_EXPERT = 512
EPS = 1e-6

LANES = 128
PROJ_ROWS = 2 * HEAD_ROWS
PROJ_SKIP = PROJ_ROWS - HEAD_ROWS
BIAS_BLOCK = 640
GLA_ROWS = 4 * GLA_CHUNK
FOX_TILE = 1024
FOX_ROWS = 128
FOX_KEYS = 1024
FOX_GROUP = 2
FOX_SKEW = 3
LOG2E = 1.4426950408889634
ROW_TILE = 512
EXPERT_TILE = 256
OUT_TILE = 256
MASK_VALUE = -1e30
PROJ_BIG = 3 * FOX_W + 2 * GLA_DK_TOT + 2 * GLA_DV_TOT
PROJ_FF = 3 * FOX_W
PROJ_GQ = PROJ_FF + FOX_HEADS
PROJ_GZ = PROJ_GQ + 2 * GLA_DK_TOT + 2 * GLA_DV_TOT
D_IN_PROJ = PROJ_GZ + GLA_RANK
PROJ_ALIGNED = (D_IN_PROJ // LANES) * LANES
PROJ_STAGE_COLS = 256
ROW_SUBLANES = D_MODEL // LANES
DMA_UNROLL = 8

F32 = jnp.float32
BF16 = jnp.bfloat16
NT_DIMS = (((1,), (1,)), ((), ()))
TN_DIMS = (((0,), (0,)), ((), ()))


def _log_sigmoid(x):
    return jnp.minimum(x, 0.0) - jnp.log(1.0 + jnp.exp(-jnp.abs(x)))


def _sigmoid(x):
    return 1.0 / (1.0 + jnp.exp(-x))


def _split3(x):
    hi = x.astype(BF16)
    rest = x - hi.astype(F32)
    mid = rest.astype(BF16)
    lo = (rest - mid.astype(F32)).astype(BF16)
    return hi, mid, lo


def _rows_load(ref, n_rows, n_chunks):
    return jnp.concatenate([ref[pl.ds(s, n_rows, stride=n_chunks), :] for s in range(n_chunks)], axis=1)


def _rows_store(ref, value, n_chunks):
    n_rows = value.shape[0]
    for s in range(n_chunks):
        ref[pl.ds(s, n_rows, stride=n_chunks), :] = value[:, s * LANES:(s + 1) * LANES]


def _params(semantics, vmem_mb):
    return pltpu.CompilerParams(dimension_semantics=semantics, vmem_limit_bytes=vmem_mb * 1024 * 1024)


def _in_proj_kernel(head_ref, x_ref, g_ref, wt_hbm, proj_ref, small_ref, w_ref, stage_ref, tail_ref, sems):
    n_chunks = PROJ_ALIGNED // PROJ_STAGE_COLS
    n_tail = D_IN_PROJ - PROJ_ALIGNED

    def chunk_copy(c):
        return pltpu.make_async_copy(wt_hbm.at[pl.ds(c * PROJ_STAGE_COLS, PROJ_STAGE_COLS), :],
                                     stage_ref.at[c % 2], sems.at[c % 2])

    def tail_copy():
        return pltpu.make_async_copy(wt_hbm.at[pl.ds(PROJ_ALIGNED, n_tail), :],
                                     tail_ref.at[pl.ds(0, n_tail), :], sems.at[2])

    @pl.when(pl.program_id(0) == 0)
    def _():
        tail_ref[...] = jnp.zeros_like(tail_ref)
        tail_copy().start()
        chunk_copy(0).start()
        for c in range(n_chunks):
            if c + 1 < n_chunks:
                chunk_copy(c + 1).start()
            chunk_copy(c).wait()
            w_ref[:, c * PROJ_STAGE_COLS:(c + 1) * PROJ_STAGE_COLS] = stage_ref[c % 2].T.astype(BF16)
        tail_copy().wait()
        w_ref[:, PROJ_ALIGNED:] = tail_ref[...].T.astype(BF16)

    x = jnp.where(pl.program_id(0) == 0, head_ref[...], x_ref[...])
    ms = jnp.mean(x * x, axis=-1, keepdims=True)
    xn = (x * lax.rsqrt(ms + EPS) * g_ref[...]).astype(BF16)
    p = jnp.dot(xn, w_ref[...], preferred_element_type=F32)
    lane = lax.broadcasted_iota(jnp.int32, (PROJ_ROWS, LANES), 1)
    small_ref[...] = jnp.where(lane < FOX_HEADS, p[:, PROJ_FF:PROJ_FF + LANES], p[:, PROJ_ALIGNED:])
    proj_ref[...] = jnp.concatenate([p[:, :PROJ_FF], p[:, PROJ_GQ:PROJ_GZ]], axis=1).astype(BF16)


def _in_proj(head, x, g1, w_t):
    t = PROJ_ROWS + x.shape[0]
    fixed = lambda i: (0, 0)
    return pl.pallas_call(
        _in_proj_kernel,
        grid=(t // PROJ_ROWS,),
        in_specs=[
            pl.BlockSpec((PROJ_ROWS, D_MODEL), fixed),
            pl.BlockSpec((PROJ_ROWS, D_MODEL), lambda i: (jnp.maximum(i - 1, 0), 0)),
            pl.BlockSpec((1, D_MODEL), fixed),
            pl.BlockSpec(memory_space=pl.ANY),
        ],
        out_specs=[
            pl.BlockSpec((PROJ_ROWS, PROJ_BIG), lambda i: (i, 0)),
            pl.BlockSpec((PROJ_ROWS, LANES), lambda i: (i, 0)),
        ],
        out_shape=[
            jax.ShapeDtypeStruct((t, PROJ_BIG), BF16),
            jax.ShapeDtypeStruct((t, LANES), F32),
        ],
        scratch_shapes=[pltpu.VMEM((D_MODEL, PROJ_ALIGNED + LANES), BF16),
                        pltpu.VMEM((2, PROJ_STAGE_COLS, D_MODEL), F32),
                        pltpu.VMEM((LANES, D_MODEL), F32),
                        pltpu.SemaphoreType.DMA((3,))],
        compiler_params=_params(("arbitrary",), 56),
    )(head, x, g1, w_t)


def _fox_bias_kernel(small_ref, bf_ref, negc_ref):
    t = negc_ref.shape[1]
    r = lax.broadcasted_iota(jnp.int32, (BIAS_BLOCK, BIAS_BLOCK), 0)
    c = lax.broadcasted_iota(jnp.int32, (BIAS_BLOCK, BIAS_BLOCK), 1)
    upper = jnp.where(r <= c, 1.0, 0.0).astype(BF16)
    lane = lax.broadcasted_iota(jnp.int32, (FOX_HEADS, BIAS_BLOCK), 1)

    def body(b, carry):
        off = pl.multiple_of(b * BIAS_BLOCK, BIAS_BLOCK)
        valid = (off + lane) >= PAD_FRONT
        f_logit = small_ref[pl.ds(PROJ_SKIP + off, BIAS_BLOCK), :].T[0:FOX_HEADS, :]
        lf = jnp.where(valid, _log_sigmoid(f_logit + bf_ref[...]), 0.0)
        cum = sum(jnp.dot(piece, upper, preferred_element_type=F32) for piece in _split3(lf)) + carry
        negc_ref[:, pl.ds(off, BIAS_BLOCK)] = jnp.where(valid, -LOG2E * cum, MASK_VALUE)
        return cum[:, BIAS_BLOCK - 1:BIAS_BLOCK]

    lax.fori_loop(0, t // BIAS_BLOCK, body, jnp.zeros((FOX_HEADS, 1), F32))


def _fox_bias(small, b_f):
    return pl.pallas_call(
        _fox_bias_kernel,
        out_shape=jax.ShapeDtypeStruct((FOX_HEADS, small.shape[0] - PROJ_SKIP), F32),
    )(small, b_f)


def _gla_kernel(q_ref, k_ref, v_ref, r_ref, small_ref, w2_ref, bg_ref, ng_ref, o_ref, st_ref):
    i = pl.program_id(0)

    @pl.when(i == 0)
    def _():
        st_ref[...] = jnp.zeros_like(st_ref)

    z_hi, z_mid, _ = _split3(small_ref[...])
    w_hi, w_mid, _ = _split3(w2_ref[...])
    gate_logit = (jnp.dot(z_hi, w_hi, preferred_element_type=F32) + jnp.dot(z_hi, w_mid, preferred_element_type=F32)
                  + jnp.dot(z_mid, w_hi, preferred_element_type=F32) + bg_ref[...])
    g = _log_sigmoid(gate_logit) * (1.0 / GLA_TAU)
    rowid = i * GLA_ROWS + lax.broadcasted_iota(jnp.int32, (GLA_ROWS, 1), 0)
    g = jnp.where(rowid >= PROJ_SKIP + PAD_FRONT, g, 0.0)

    ci = lax.broadcasted_iota(jnp.int32, (GLA_CHUNK, GLA_CHUNK), 0)
    cj = lax.broadcasted_iota(jnp.int32, (GLA_CHUNK, GLA_CHUNK), 1)
    causal = cj <= ci
    lower = jnp.where(causal, 1.0, 0.0).astype(BF16)
    scale = GLA_DK ** -0.5
    mid = GLA_CHUNK // 2

    chunks = [slice(c * GLA_CHUNK, (c + 1) * GLA_CHUNK) for c in range(GLA_ROWS // GLA_CHUNK)]
    cums = [sum(jnp.dot(lower, piece, preferred_element_type=F32) for piece in _split3(g[rows])) for rows in chunks]

    heads = range(GLA_HEADS)
    ks = [slice(h * GLA_DK, (h + 1) * GLA_DK) for h in heads]
    vs = [slice(h * GLA_DV, (h + 1) * GLA_DV) for h in heads]

    prep = []
    for rows, b in zip(chunks, cums):
        b_mid = b[mid:mid + 1]
        b_last = b[GLA_CHUNK - 1:GLA_CHUNK]
        q = q_ref[rows, :].astype(F32) * scale
        k = k_ref[rows, :].astype(F32)
        q_intra = (q * jnp.exp(b - b_mid)).astype(BF16)
        k_intra = (k * jnp.exp(b_mid - b)).astype(BF16)
        q_inter = (q * jnp.exp(b)).astype(BF16)
        k_state = (k * jnp.exp(b_last - b)).astype(BF16)
        decay = jnp.exp(b_last)
        v = [v_ref[rows, vs[h]] for h in heads]
        a = [lax.dot_general(q_intra[:, ks[h]], k_intra[:, ks[h]], NT_DIMS, preferred_element_type=F32)
             for h in heads]
        u_t = [lax.dot_general(v[h], k_state[:, ks[h]], TN_DIMS, preferred_element_type=F32) for h in heads]
        prep.append((q_inter, decay, v, a, u_t))

    st = [st_ref[h] for h in heads]
    for rows, (q_inter, decay, v, a, u_t) in zip(chunks, prep):
        o_inter = [lax.dot_general(q_inter[:, ks[h]], st[h].astype(BF16), NT_DIMS, preferred_element_type=F32)
                   for h in heads]
        st = [decay[:, ks[h]] * st[h] + u_t[h] for h in heads]
        for h in heads:
            o = o_inter[h] + jnp.dot(jnp.where(causal, a[h], 0.0).astype(BF16), v[h], preferred_element_type=F32)
            ms = jnp.mean(o * o, axis=-1, keepdims=True)
            y = o * lax.rsqrt(ms + EPS) * ng_ref[...]
            r = r_ref[rows, vs[h]].astype(F32)
            o_ref[rows, vs[h]] = (y * (r * _sigmoid(r))).astype(BF16)
    for h in heads:
        st_ref[h] = st[h]


def _gla(proj, small, w2_pad, b_gate, norm_g):
    rows = proj.shape[0]
    assert GLA_ROWS == PROJ_SKIP + HEAD_ROWS
    q_blk = (3 * FOX_W) // GLA_DK_TOT
    v_blk = (3 * FOX_W + 2 * GLA_DK_TOT) // GLA_DV_TOT
    return pl.pallas_call(
        _gla_kernel,
        grid=(rows // GLA_ROWS,),
        in_specs=[
            pl.BlockSpec((GLA_ROWS, GLA_DK_TOT), lambda i: (i, q_blk)),
            pl.BlockSpec((GLA_ROWS, GLA_DK_TOT), lambda i: (i, q_blk + 1)),
            pl.BlockSpec((GLA_ROWS, GLA_DV_TOT), lambda i: (i, v_blk)),
            pl.BlockSpec((GLA_ROWS, GLA_DV_TOT), lambda i: (i, v_blk + 1)),
            pl.BlockSpec((GLA_ROWS, LANES), lambda i: (i, 0)),
            pl.BlockSpec((LANES, GLA_DK_TOT), lambda i: (0, 0)),
            pl.BlockSpec((1, GLA_DK_TOT), lambda i: (0, 0)),
            pl.BlockSpec((1, GLA_DV), lambda i: (0, 0)),
        ],
        out_specs=pl.BlockSpec((GLA_ROWS, GLA_DV_TOT), lambda i: (jnp.maximum(i - 1, 0), 0)),
        out_shape=jax.ShapeDtypeStruct((rows - GLA_ROWS, GLA_DV_TOT), BF16),
        scratch_shapes=[pltpu.VMEM((GLA_HEADS, GLA_DV, GLA_DK), F32)],
        compiler_params=_params(("arbitrary",), 32),
    )(proj, proj, proj, proj, small, w2_pad, b_gate, norm_g)


def _fox_kernel(q_ref, k_ref, v_ref, negc_ref, ng_ref, o_ref, qs_ref, va_ref):
    qi = pl.program_id(1)
    n_blocks = FOX_TILE // FOX_ROWS

    @pl.when(qi == 0)
    def _():
        lane = lax.broadcasted_iota(jnp.int32, (v_ref.shape[0], FOX_HD), 1)
        ones_col = jnp.where(lane == 0, 1.0, 0.0).astype(BF16)
        for hh in range(FOX_GROUP):
            va_ref[:, 2 * hh * FOX_HD:(2 * hh + 1) * FOX_HD] = v_ref[:, hh * FOX_HD:(hh + 1) * FOX_HD]
            va_ref[:, (2 * hh + 1) * FOX_HD:(2 * hh + 2) * FOX_HD] = ones_col

    units = [(hh, rb) for hh in range(FOX_GROUP) for rb in range(n_blocks)]
    q0 = pl.multiple_of(HEAD_ROWS + qi * FOX_TILE, FOX_ROWS)
    qs_ref[...] = (q_ref[pl.ds(PROJ_SKIP + q0, FOX_TILE), :].astype(F32) * (FOX_HD ** -0.5 * LOG2E)).astype(BF16)
    row = lax.broadcasted_iota(jnp.int32, (FOX_ROWS, FOX_ROWS), 0)
    col = lax.broadcasted_iota(jnp.int32, (FOX_ROWS, FOX_ROWS), 1)

    def run(state, steps):
        def scores(step):
            u, off, k0, k1, causal_tail = step
            hh, rb = units[u]
            rows = slice(rb * FOX_ROWS, (rb + 1) * FOX_ROWS)
            cols = slice(hh * FOX_HD, (hh + 1) * FOX_HD)
            s = lax.dot_general(qs_ref[rows, cols], k_ref[pl.ds(PROJ_SKIP + off + k0, k1 - k0), cols], NT_DIMS,
                                preferred_element_type=F32)
            s = s + negc_ref[hh, :, pl.ds(off + k0, k1 - k0)]
            if causal_tail:
                tail = jnp.where(col <= row, s[:, k1 - k0 - FOX_ROWS:], MASK_VALUE)
                s = tail if k1 - k0 == FOX_ROWS else jnp.concatenate([s[:, :k1 - k0 - FOX_ROWS], tail], axis=1)
            return s

        def update(step, s, state):
            u, off, k0, k1, _ = step
            hh, _ = units[u]
            m_prev, acc_prev = state[u]
            m_new = jnp.maximum(m_prev, jnp.max(s, axis=-1, keepdims=True))
            p = jnp.exp2(s - m_new).astype(BF16)
            acc_new = jnp.exp2(m_prev - m_new) * acc_prev + jnp.dot(
                p, va_ref[pl.ds(PROJ_SKIP + off + k0, k1 - k0), 2 * hh * FOX_HD:(2 * hh + 2) * FOX_HD],
                preferred_element_type=F32)
            state[u] = (m_new, acc_new)

        state = list(state)
        pending = [scores(st) for st in steps[:FOX_SKEW]]
        for j, st in enumerate(steps):
            if j + FOX_SKEW < len(steps):
                pending.append(scores(steps[j + FOX_SKEW]))
            update(st, pending[j], state)
            pending[j] = None
        return tuple(state)

    head_steps = [(u, 0, 0, HEAD_ROWS, False) for u in range(len(units))]

    def full_steps(off):
        return [(u, off, k0, k0 + FOX_KEYS, False)
                for k0 in range(0, FOX_TILE, FOX_KEYS) for u in range(len(units))]

    diag_steps = []
    for k0 in range(0, FOX_TILE, FOX_KEYS):
        for u, (_, rb) in enumerate(units):
            last = (rb + 1) * FOX_ROWS
            if last > k0:
                diag_steps.append((u, q0, k0, min(k0 + FOX_KEYS, last), last <= k0 + FOX_KEYS))

    state = tuple((jnp.full((FOX_ROWS, 1), MASK_VALUE, F32), jnp.zeros((FOX_ROWS, 2 * FOX_HD), F32))
                  for _ in units)
    state = run(state, head_steps)
    state = lax.fori_loop(
        0, qi, lambda kt, st: run(st, full_steps(pl.multiple_of(HEAD_ROWS + kt * FOX_TILE, FOX_ROWS))), state)
    state = run(state, diag_steps)
    for u, (hh, rb) in enumerate(units):
        _, acc = state[u]
        o = acc[:, :FOX_HD] / acc[:, FOX_HD:FOX_HD + 1]
        ms = jnp.mean(o * o, axis=-1, keepdims=True)
        o_ref[rb * FOX_ROWS:(rb + 1) * FOX_ROWS, hh * FOX_HD:(hh + 1) * FOX_HD] = (
            o * lax.rsqrt(ms + EPS) * ng_ref[...]).astype(BF16)


def _fox(proj, negc3, norm_g):
    rows = proj.shape[0]
    t = rows - PROJ_SKIP
    width = FOX_GROUP * FOX_HD
    k_blk = FOX_W // width
    return pl.pallas_call(
        _fox_kernel,
        grid=(FOX_HEADS // FOX_GROUP, (t - HEAD_ROWS) // FOX_TILE),
        in_specs=[
            pl.BlockSpec((rows, width), lambda g, i: (0, g)),
            pl.BlockSpec((rows, width), lambda g, i: (0, k_blk + g)),
            pl.BlockSpec((rows, width), lambda g, i: (0, 2 * k_blk + g)),
            pl.BlockSpec((FOX_GROUP, 1, t), lambda g, i: (g, 0, 0)),
            pl.BlockSpec((1, FOX_HD), lambda g, i: (0, 0)),
        ],
        out_specs=pl.BlockSpec((FOX_TILE, width), lambda g, i: (i, g)),
        out_shape=jax.ShapeDtypeStruct((t - HEAD_ROWS, FOX_W), BF16),
        scratch_shapes=[pltpu.VMEM((FOX_TILE, width), BF16), pltpu.VMEM((rows, 2 * width), BF16)],
        compiler_params=_params(("arbitrary", "arbitrary"), 56),
    )(proj, proj, proj, negc3, norm_g)


ROUTE_E1, ROUTE_E2, ROUTE_R1, ROUTE_R2, ROUTE_W1, ROUTE_W2 = range(6)


def _out_proj_kernel(og_ref, of_ref, h_ref, wo_ref, g2_ref, wr_ref, br_ref,
                     h1_ref, u2_ref, route_ref, cnt_ref, tri_ref, run_ref):
    i = pl.program_id(0)
    half = ROW_TILE // 2

    @pl.when(i == 0)
    def _():
        r = lax.broadcasted_iota(jnp.int32, (half, half), 0)
        c = lax.broadcasted_iota(jnp.int32, (half, half), 1)
        tri_ref[...] = jnp.where(c < r, 1.0, 0.0).astype(BF16)
        run_ref[...] = jnp.zeros_like(run_ref)

    lane = lax.broadcasted_iota(jnp.int32, (half, LANES), 1).astype(F32)
    ninf = -jnp.inf

    def first_max(vals):
        top = jnp.max(vals, axis=-1, keepdims=True)
        idx = jnp.min(jnp.where(vals == top, lane, float(LANES)), axis=-1, keepdims=True)
        return top, idx

    halves = [slice(0, half), slice(half, ROW_TILE)]
    h1 = [h_ref[rows, :]
          + jnp.dot(og_ref[rows, :], wo_ref[0:GLA_DV_TOT, :], preferred_element_type=F32)
          + jnp.dot(of_ref[rows, :], wo_ref[GLA_DV_TOT:, :], preferred_element_type=F32) for rows in halves]
    running = run_ref[...]
    for rows, h1_half in zip(halves, h1):
        h1_ref[rows, :] = h1_half
        ms = jnp.mean(h1_half * h1_half, axis=-1, keepdims=True)
        u2 = (h1_half * lax.rsqrt(ms + EPS) * g2_ref[...]).astype(BF16)
        u2_ref[rows, :] = u2

        logits = jnp.dot(u2, wr_ref[...], preferred_element_type=F32) + br_ref[...]
        gl = jnp.where(lane < N_GROUPS, logits, ninf)
        g_top, g_idx = first_max(gl)
        p_g = 1.0 / jnp.sum(jnp.exp(gl - g_top), axis=-1, keepdims=True)
        e_lo = N_GROUPS + EXPERTS_PER_GROUP * g_idx
        el = jnp.where((lane >= e_lo) & (lane < e_lo + EXPERTS_PER_GROUP), logits, ninf)
        top1, i1 = first_max(el)
        top2, i2 = first_max(jnp.where(lane == i1, ninf, el))
        ratio = jnp.exp(top2 - top1)
        w1 = 1.0 / (1.0 + ratio)
        w2 = ratio * w1

        is1 = lane == i1
        is2 = lane == i2
        onehot = jnp.where(is1 | is2, 1.0, 0.0)
        before = jnp.dot(tri_ref[...], onehot.astype(BF16), preferred_element_type=F32) + running
        r1 = jnp.sum(jnp.where(is1, before, 0.0), axis=-1, keepdims=True)
        r2 = jnp.sum(jnp.where(is2, before, 0.0), axis=-1, keepdims=True)
        running = running + jnp.sum(onehot, axis=0, keepdims=True)

        rec = jnp.zeros_like(logits)
        for slot, val in ((ROUTE_E1, i1 - N_GROUPS), (ROUTE_E2, i2 - N_GROUPS), (ROUTE_R1, r1), (ROUTE_R2, r2),
                          (ROUTE_W1, p_g * w1), (ROUTE_W2, p_g * w2)):
            rec = jnp.where(lane == slot, val, rec)
        route_ref[rows, :] = rec
    run_ref[...] = running
    cnt_ref[...] = running


def _out_proj(o_gla, o_fox, h0, w_out, g2, w_router, b_router):
    t = h0.shape[0]
    row = lambda i: (i, 0)
    fixed = lambda i: (0, 0)
    return pl.pallas_call(
        _out_proj_kernel,
        grid=(t // ROW_TILE,),
        in_specs=[
            pl.BlockSpec((ROW_TILE, GLA_DV_TOT), row),
            pl.BlockSpec((ROW_TILE, FOX_W), row),
            pl.BlockSpec((ROW_TILE, D_MODEL), row),
            pl.BlockSpec((D_MODEL, D_MODEL), fixed, pipeline_mode=pl.Buffered(1)),
            pl.BlockSpec((1, D_MODEL), fixed),
            pl.BlockSpec((D_MODEL, LANES), fixed),
            pl.BlockSpec((1, LANES), fixed),
        ],
        out_specs=[
            pl.BlockSpec((ROW_TILE, D_MODEL), row),
            pl.BlockSpec((ROW_TILE, D_MODEL), row),
            pl.BlockSpec((ROW_TILE, LANES), row),
            pl.BlockSpec((1, LANES), fixed),
        ],
        out_shape=[
            jax.ShapeDtypeStruct((t, D_MODEL), F32),
            jax.ShapeDtypeStruct((t, D_MODEL), BF16),
            jax.ShapeDtypeStruct((t, LANES), F32),
            jax.ShapeDtypeStruct((1, LANES), F32),
        ],
        scratch_shapes=[pltpu.VMEM((ROW_TILE // 2, ROW_TILE // 2), BF16), pltpu.VMEM((1, LANES), F32)],
        compiler_params=_params(("arbitrary",), 48),
    )(o_gla, o_fox, h0, w_out, g2, w_router, b_router)


def _scatter_kernel(pos1_ref, pos2_ref, pad_start_ref, pad_len_ref, nv_ref, u2_ref, xs_ref,
                    rows_ref, zero_ref, sems, zsem):
    i = pl.program_id(0)
    n_steps = pl.num_programs(0)
    tile_rows = EXPERT_TILE * ROW_SUBLANES
    n_tiles = xs_ref.shape[0] // tile_rows

    def zero_copy(start, n_rows):
        start = pl.multiple_of(start * ROW_SUBLANES, ROW_SUBLANES)
        return pltpu.make_async_copy(zero_ref.at[pl.ds(0, n_rows * ROW_SUBLANES), :],
                                     xs_ref.at[pl.ds(start, n_rows * ROW_SUBLANES), :], zsem)

    def for_each_zero_block(action):
        def expert(e, carry):
            start = pad_start_ref[e]
            length = pad_len_ref[e]
            for bit in reversed(range(EXPERT_TILE.bit_length() - 1)):
                n_rows = 1 << bit

                @pl.when((length & n_rows) != 0)
                def _():
                    action(zero_copy(start, n_rows))

                start = start + (length & n_rows)
            return carry

        def unused_tile(j, carry):
            action(zero_copy(j * EXPERT_TILE, EXPERT_TILE))
            return carry

        lax.fori_loop(0, N_EXPERTS, expert, 0)
        lax.fori_loop(nv_ref[0], n_tiles, unused_tile, 0)

    @pl.when(i == 0)
    def _():
        zero_ref[...] = jnp.zeros_like(zero_ref)
        for_each_zero_block(lambda cp: cp.start(priority=1))

    def row_copy(step, r, pos_ref):
        slot = step % 2
        src = pl.multiple_of(r * ROW_SUBLANES, ROW_SUBLANES)
        dst = pl.multiple_of(pos_ref[step * ROW_TILE + r] * ROW_SUBLANES, ROW_SUBLANES)
        return pltpu.make_async_copy(rows_ref.at[slot, pl.ds(src, ROW_SUBLANES), :],
                                     xs_ref.at[pl.ds(dst, ROW_SUBLANES), :], sems.at[slot])

    def for_each_row(step, action):
        def group(j, carry):
            for k in range(DMA_UNROLL):
                action(row_copy(step, j * DMA_UNROLL + k, pos1_ref))
                action(row_copy(step, j * DMA_UNROLL + k, pos2_ref))
            return carry

        lax.fori_loop(0, ROW_TILE // DMA_UNROLL, group, 0)

    _rows_store(rows_ref.at[i % 2], u2_ref[...].astype(F32), ROW_SUBLANES)
    for_each_row(i, lambda cp: cp.start())

    @pl.when(i > 0)
    def _():
        for_each_row(i - 1, lambda cp: cp.wait())

    @pl.when(i == n_steps - 1)
    def _():
        for_each_row(i, lambda cp: cp.wait())
        for_each_zero_block(lambda cp: cp.wait())


def _scatter(pos1, pos2, pad_start, pad_len, n_valid, u2, n_rows):
    t = u2.shape[0]
    return pl.pallas_call(
        _scatter_kernel,
        grid_spec=pltpu.PrefetchScalarGridSpec(
            num_scalar_prefetch=5,
            grid=(t // ROW_TILE,),
            in_specs=[pl.BlockSpec((ROW_TILE, D_MODEL), lambda i, *_: (i, 0))],
            out_specs=pl.BlockSpec(memory_space=pl.ANY),
            scratch_shapes=[pltpu.VMEM((2, ROW_TILE * ROW_SUBLANES, LANES), F32),
                            pltpu.VMEM((EXPERT_TILE * ROW_SUBLANES, LANES), F32),
                            pltpu.SemaphoreType.DMA((2,)), pltpu.SemaphoreType.DMA],
        ),
        out_shape=jax.ShapeDtypeStruct((n_rows * ROW_SUBLANES, LANES), F32),
        compiler_params=_params(("arbitrary",), 40),
    )(pos1, pos2, pad_start, pad_len, n_valid, u2)


def _moe_kernel(te_ref, first_ref, next_ref, nv_ref, xs_ref, wg_hbm, wu_hbm, wd_hbm, y_ref,
                wg_stage, wu_stage, wd_stage, wgb_ref, wub_ref, wdb_ref, sems):
    i = pl.program_id(0)

    def weight_copies(expert):
        return (pltpu.make_async_copy(wg_hbm.at[expert], wg_stage, sems.at[0]),
                pltpu.make_async_copy(wu_hbm.at[expert], wu_stage, sems.at[1]),
                pltpu.make_async_copy(wd_hbm.at[expert], wd_stage, sems.at[2]))

    @pl.when(i >= nv_ref[0])
    def _():
        y_ref[...] = jnp.zeros_like(y_ref)

    @pl.when(i < nv_ref[0])
    def _():
        @pl.when(first_ref[i] == 1)
        def _():
            @pl.when(i == 0)
            def _():
                for cp in weight_copies(te_ref[i]):
                    cp.start()

            for cp in weight_copies(te_ref[i]):
                cp.wait()
            wgb_ref[...] = wg_stage[...].astype(BF16)
            wub_ref[...] = wu_stage[...].astype(BF16)
            wdb_ref[...] = wd_stage[...].astype(BF16)

            @pl.when(next_ref[i] >= 0)
            def _():
                for cp in weight_copies(next_ref[i]):
                    cp.start(priority=1)

        x = _rows_load(xs_ref, EXPERT_TILE, ROW_SUBLANES).astype(BF16)
        hg = jnp.dot(x, wgb_ref[...], preferred_element_type=F32)
        hu = jnp.dot(x, wub_ref[...], preferred_element_type=F32)
        hm = (hg * _sigmoid(hg) * hu).astype(BF16)
        y = jnp.dot(hm, wdb_ref[...], preferred_element_type=F32)
        _rows_store(y_ref, y, ROW_SUBLANES)


def _moe(tile_expert, tile_first, tile_next, n_valid, xs, w_g, w_u, w_d):
    n_rows = xs.shape[0] // ROW_SUBLANES
    row = lambda i, *_: (i, 0)
    return pl.pallas_call(
        _moe_kernel,
        grid_spec=pltpu.PrefetchScalarGridSpec(
            num_scalar_prefetch=4,
            grid=(n_rows // EXPERT_TILE,),
            in_specs=[
                pl.BlockSpec((EXPERT_TILE * ROW_SUBLANES, LANES),
                             lambda i, te, first, nxt, nv: (jnp.minimum(i, nv[0] - 1), 0)),
                pl.BlockSpec(memory_space=pl.ANY),
                pl.BlockSpec(memory_space=pl.ANY),
                pl.BlockSpec(memory_space=pl.ANY),
            ],
            out_specs=pl.BlockSpec((EXPERT_TILE * ROW_SUBLANES, LANES), row),
            scratch_shapes=[pltpu.VMEM((D_MODEL, D_EXPERT), F32), pltpu.VMEM((D_MODEL, D_EXPERT), F32),
                            pltpu.VMEM((D_EXPERT, D_MODEL), F32),
                            pltpu.VMEM((D_MODEL, D_EXPERT), BF16), pltpu.VMEM((D_MODEL, D_EXPERT), BF16),
                            pltpu.VMEM((D_EXPERT, D_MODEL), BF16),
                            pltpu.SemaphoreType.DMA((3,))],
        ),
        out_shape=jax.ShapeDtypeStruct((n_rows * ROW_SUBLANES, LANES), F32),
        compiler_params=_params(("arbitrary",), 48),
    )(tile_expert, tile_first, tile_next, n_valid, xs, w_g, w_u, w_d)


def _combine_kernel(pos1_ref, pos2_ref, h1_ref, route_ref, gf_ref, y_ref, out_ref, ya_ref, yb_ref, sems):
    i = pl.program_id(0)
    n_steps = pl.num_programs(0)

    def row_copy(tile, r, pos_ref, buf_ref):
        slot = tile % 2
        src = pl.multiple_of(pos_ref[tile * OUT_TILE + r] * ROW_SUBLANES, ROW_SUBLANES)
        dst = pl.multiple_of(r * ROW_SUBLANES, ROW_SUBLANES)
        return pltpu.make_async_copy(y_ref.at[pl.ds(src, ROW_SUBLANES), :],
                                     buf_ref.at[slot, pl.ds(dst, ROW_SUBLANES), :], sems.at[slot])

    def for_each_row(tile, action):
        def group(j, carry):
            for k in range(DMA_UNROLL):
                action(row_copy(tile, j * DMA_UNROLL + k, pos1_ref, ya_ref))
                action(row_copy(tile, j * DMA_UNROLL + k, pos2_ref, yb_ref))
            return carry

        lax.fori_loop(0, OUT_TILE // DMA_UNROLL, group, 0)

    @pl.when(i == 0)
    def _():
        for_each_row(i, lambda cp: cp.start())

    @pl.when(i + 1 < n_steps)
    def _():
        for_each_row(i + 1, lambda cp: cp.start())

    for_each_row(i, lambda cp: cp.wait())
    slot = i % 2
    rec = route_ref[...]
    ya = _rows_load(ya_ref.at[slot], OUT_TILE, ROW_SUBLANES)
    yb = _rows_load(yb_ref.at[slot], OUT_TILE, ROW_SUBLANES)
    hh = h1_ref[...] + rec[:, ROUTE_W1:ROUTE_W1 + 1] * ya + rec[:, ROUTE_W2:ROUTE_W2 + 1] * yb
    ms = jnp.mean(hh * hh, axis=-1, keepdims=True)
    out_ref[...] = hh * lax.rsqrt(ms + EPS) * gf_ref[...]


def _combine(pos1, pos2, h1, route, g_f, y):
    t = h1.shape[0]
    row = lambda i, *_: (i, 0)
    return pl.pallas_call(
        _combine_kernel,
        grid_spec=pltpu.PrefetchScalarGridSpec(
            num_scalar_prefetch=2,
            grid=(t // OUT_TILE,),
            in_specs=[
                pl.BlockSpec((OUT_TILE, D_MODEL), row),
                pl.BlockSpec((OUT_TILE, LANES), row),
                pl.BlockSpec((1, D_MODEL), lambda i, *_: (0, 0)),
                pl.BlockSpec(memory_space=pl.ANY),
            ],
            out_specs=pl.BlockSpec((OUT_TILE, D_MODEL), row),
            scratch_shapes=[pltpu.VMEM((2, OUT_TILE * ROW_SUBLANES, LANES), F32),
                            pltpu.VMEM((2, OUT_TILE * ROW_SUBLANES, LANES), F32),
                            pltpu.SemaphoreType.DMA((2,))],
        ),
        out_shape=jax.ShapeDtypeStruct((t, D_MODEL), F32),
        compiler_params=_params(("arbitrary",), 40),
    )(pos1, pos2, h1, route, g_f, y)


def _routing_tables(route, cnt, n_tiles):
    counts = cnt[0, N_GROUPS:N_GROUPS + N_EXPERTS].astype(jnp.int32)
    tiles = (counts + EXPERT_TILE - 1) // EXPERT_TILE
    tile_end = jnp.cumsum(tiles)
    row_start = (tile_end - tiles) * EXPERT_TILE
    n_valid = tile_end[-1]
    expert_ids = jnp.arange(N_EXPERTS, dtype=jnp.int32)

    def positions(expert_lane, rank_lane):
        expert = route[:, expert_lane].astype(jnp.int32)
        start = jnp.sum(jnp.where(expert[:, None] == expert_ids[None, :], row_start[None, :], 0), axis=1)
        return start + route[:, rank_lane].astype(jnp.int32)

    pos1 = positions(ROUTE_E1, ROUTE_R1)
    pos2 = positions(ROUTE_E2, ROUTE_R2)
    tile = jnp.minimum(jnp.arange(n_tiles, dtype=jnp.int32), n_valid - 1)
    tile_expert = jnp.sum(tile[:, None] >= tile_end[None, :], axis=-1).astype(jnp.int32)
    owner = tile_expert[:, None] == expert_ids[None, :]
    tile_first = (tile == jnp.sum(jnp.where(owner, (tile_end - tiles)[None, :], 0), axis=1)).astype(jnp.int32)
    later = (expert_ids[None, :] > expert_ids[:, None]) & (tiles[None, :] > 0)
    next_expert = jnp.min(jnp.where(later, expert_ids[None, :], N_EXPERTS), axis=1)
    next_expert = jnp.where(next_expert == N_EXPERTS, -1, next_expert)
    tile_next = jnp.sum(jnp.where(owner, next_expert[None, :], 0), axis=1).astype(jnp.int32)
    pad_start = row_start + counts
    pad_len = tiles * EXPERT_TILE - counts
    return pos1, pos2, tile_expert, tile_first, tile_next, n_valid.reshape(1), pad_start, pad_len


def kernel(x, meta_tokens, norm1_g, w_in, b_fox_f, gla_w_gate2, gla_b_gate, gla_norm_g, fox_norm_g, w_out,
           norm2_g, w_router_group, b_router_group, w_router_expert, b_router_expert, w_exp_gate, w_exp_up,
           w_exp_down, norm_f_g):
    batch, seq, _ = x.shape
    assert batch == 1 and norm1_g.shape[0] == 1
    assert seq % FOX_TILE == 0 and seq % ROW_TILE == 0 and (HEAD_ROWS + seq) % BIAS_BLOCK == 0
    t = HEAD_ROWS + seq
    x2 = x[0]
    head = jnp.concatenate([jnp.zeros((PROJ_SKIP + PAD_FRONT, D_MODEL), F32), meta_tokens.astype(F32)], axis=0)

    assert w_in.shape == (1, D_MODEL, D_IN_PROJ) and PROJ_ALIGNED % PROJ_STAGE_COLS == 0
    proj, small = _in_proj(head, x2, norm1_g, w_in[0].T)

    negc = _fox_bias(small, b_fox_f[0].reshape(FOX_HEADS, 1))
    w2_pad = jnp.zeros((LANES, GLA_DK_TOT), F32).at[FOX_HEADS:FOX_HEADS + GLA_RANK].set(gla_w_gate2[0])
    o_gla = _gla(proj, small, w2_pad, gla_b_gate, gla_norm_g)
    o_fox = _fox(proj, negc.reshape(FOX_HEADS, 1, t), fox_norm_g)

    w_router = jnp.concatenate(
        [w_router_group[0], jnp.transpose(w_router_expert[0], (1, 0, 2)).reshape(D_MODEL, N_EXPERTS),
         jnp.zeros((D_MODEL, LANES - N_GROUPS - N_EXPERTS), F32)], axis=1).astype(BF16)
    b_router = jnp.concatenate([b_router_group[0], b_router_expert[0].reshape(-1),
                                jnp.zeros((LANES - N_GROUPS - N_EXPERTS,), F32)]).reshape(1, LANES)
    h1, u2, route, cnt = _out_proj(o_gla, o_fox, x2, w_out[0].astype(BF16), norm2_g, w_router, b_router)

    n_tiles = (2 * seq) // EXPERT_TILE + N_EXPERTS
    pos1, pos2, tile_expert, tile_first, tile_next, n_valid, pad_start, pad_len = _routing_tables(route, cnt, n_tiles)
    xs = _scatter(pos1, pos2, pad_start, pad_len, n_valid, u2, n_tiles * EXPERT_TILE)
    y = _moe(tile_expert, tile_first, tile_next, n_valid, xs,
             w_exp_gate[0].reshape(N_EXPERTS, D_MODEL, D_EXPERT),
             w_exp_up[0].reshape(N_EXPERTS, D_MODEL, D_EXPERT),
             w_exp_down[0].reshape(N_EXPERTS, D_EXPERT, D_MODEL))
    out = _combine(pos1, pos2, h1, route, norm_f_g.reshape(1, D_MODEL), y)
    return out.reshape(1, seq, D_MODEL)
```
